```python
import math
import numpy as np
import jax
import jax.numpy as jnp
from jax import lax

D_MODEL = 2048
BATCH = 8
SEQ = 4096
DEPTH = 4

CTX_LEN = 256
GRID_W = 64
MIX_W = D_MODEL // 2
N_EVEN = (DEPTH + 1) // 2
N_ODD = DEPTH // 2
N_MOD = 6
NORM_EPS = 1e-6

S5_GROUP = 16
S5_GROUPS = MIX_W // S5_GROUP
S5_STATE = 64
S5_DT_MIN = 1e-3
S5_DT_MAX = 1e-1

NA_HEADS = 8
NA_HEAD_DIM = MIX_W // NA_HEADS
WIN_ROWS = 8
WIN_COLS = 16

RK_HEAD = 64
RK_HEADS = MIX_W // RK_HEAD
RK_DECAY_RANK = 64
RK_ICLR_RANK = 64
RK_GATE_RANK = 128
RK_GN_EPS = 64e-5
RK_SPLITS = (MIX_W, MIX_W, MIX_W, RK_GATE_RANK, 2 * RK_DECAY_RANK, 2 * RK_ICLR_RANK)
RK_IN = sum(RK_SPLITS)

GQ_HEAD_DIM = 128
GQ_HEADS = MIX_W // GQ_HEAD_DIM
GQ_KV_HEADS = GQ_HEADS // 4
KV_W = GQ_KV_HEADS * GQ_HEAD_DIM
ROPE_THETA = 10000.0
Q_BLOCK = 128

EVEN_SPLITS = (MIX_W, MIX_W, MIX_W, MIX_W)
EVEN_IN = sum(EVEN_SPLITS)
ODD_SPLITS = (RK_IN, MIX_W, KV_W, KV_W)
ODD_IN = sum(ODD_SPLITS)

N_EXPERTS = 32
TOP_K = 4
D_EXPERT = 3 * D_MODEL // 8
SWIGLU_ALPHA = 1.702
SWIGLU_LIMIT = 7.0
MOE_BLOCK = 128

F32 = jnp.float32

kernel_name = 'hybrid_s5_natten_rwkv7_gqa_moe_dit'


def rms_norm(x, g, eps=NORM_EPS):
    xf = x.astype(F32)
    xf = xf * lax.rsqrt(jnp.mean(xf * xf, axis=-1, keepdims=True) + eps)
    return (xf * g.astype(F32)).astype(x.dtype)


def split_cols(z, sizes):
    return jnp.split(z, np.cumsum(sizes)[:-1].tolist(), axis=-1)


def ctx_attention(q, k, v):
    b, l, h, e = q.shape
    hk = k.shape[2]
    qg = q.reshape(b, l, hk, h // hk, e)
    s = jnp.einsum('bqhge,bkhe->bhgqk', qg, k, preferred_element_type=F32) * (e ** -0.5)
    p = jax.nn.softmax(s, axis=-1).astype(v.dtype)
    return jnp.einsum('bhgqk,bkhe->bqhge', p, v).reshape(b, l, h, e)


def s5_discretize(lam_re, lam_im, log_dt, b_re, b_im):
    lam = lax.complex(lam_re.astype(F32), lam_im.astype(F32))
    dt = jnp.exp(log_dt.astype(F32))[:, None]
    lam_bar = jnp.exp(lam * dt)
    b = lax.complex(b_re.astype(F32), b_im.astype(F32))
    b_bar = ((lam_bar - 1.0) / lam)[..., None] * b
    return lam_bar, b_bar


def s5_scan(u, lam_bar, b_bar, h0, reverse):
    bu = jnp.einsum('btgh,gph->btgp', u.astype(jnp.complex64), b_bar)
    if h0 is not None:
        bu = bu.at[:, -1 if reverse else 0].add(lam_bar * h0)
    a = jnp.broadcast_to(lam_bar, bu.shape)

    def combine(left, right):
        a_l, b_l = left
        a_r, b_r = right
        return a_l * a_r, a_r * b_l + b_r

    _, h = lax.associative_scan(combine, (a, bu), axis=1, reverse=reverse)
    return h


def s5_readout(h, c_re, c_im):
    c = lax.complex(c_re.astype(F32), c_im.astype(F32))
    return jnp.real(jnp.einsum('btgp,ghp->btgh', h, c))


def s5_mixer(ux, uc, lam_re, lam_im, log_dt, b_re, b_im, c_re, c_im, d_skip, glu_w, glu_b, with_ctx):
    def grouped(u):
        return u.astype(F32).reshape(u.shape[0], u.shape[1], S5_GROUPS, S5_GROUP)

    def glu(y):
        g = jax.nn.gelu(y)
        return g * jax.nn.sigmoid(g @ glu_w.astype(F32) + glu_b.astype(F32))

    gx, gc = grouped(ux), grouped(uc)
    d_skip = d_skip.astype(F32)
    yx = d_skip * ux.astype(F32)
    yc = d_skip * uc.astype(F32) if with_ctx else None
    for d, rev in ((0, False), (1, True)):
        lam_bar, b_bar = s5_discretize(lam_re[d], lam_im[d], log_dt[d], b_re[d], b_im[d])
        hc = s5_scan(gc, lam_bar, b_bar, None, rev)
        hx = s5_scan(gx, lam_bar, b_bar, hc[:, 0] if rev else hc[:, -1], rev)
        yx = yx + s5_readout(hx, c_re[d], c_im[d]).reshape(yx.shape)
        if with_ctx:
            yc = yc + s5_readout(hc, c_re[d], c_im[d]).reshape(yc.shape)
    out_c = glu(yc).astype(uc.dtype) if with_ctx else None
    return glu(yx).astype(ux.dtype), out_c


def neighborhood_attention(qx, kx, vx, qc, kc, vc, rpb, with_ctx):
    bsz, seq, nh, hd = qx.shape
    rows = seq // GRID_W
    kr = min(WIN_ROWS, rows)
    scale = hd ** -0.5
    r = jnp.arange(rows)
    row_start = jnp.clip(r - kr // 2, 0, rows - kr)
    row_idx = row_start[:, None] + jnp.arange(kr)[None, :]
    col = jnp.arange(GRID_W)
    col_start = jnp.clip(col - WIN_COLS // 2, 0, GRID_W - WIN_COLS)
    col_ok = (col[None, :] >= col_start[:, None]) & (col[None, :] < col_start[:, None] + WIN_COLS)
    dr = row_idx - r[:, None] + (WIN_ROWS - 1)
    dc = jnp.clip(col[None, :] - col[:, None] + (WIN_COLS - 1), 0, 2 * WIN_COLS - 2)
    bias = rpb.astype(F32)[:, dr[:, None, :, None], dc[None, :, None, :]]
    bias = jnp.where(col_ok[None, None, :, None, :], bias, -jnp.inf)

    q = qx.reshape(bsz, rows, GRID_W, nh, hd)
    kg = kx.reshape(bsz, rows, GRID_W, nh, hd)[:, row_idx]
    vg = vx.reshape(bsz, rows, GRID_W, nh, hd)[:, row_idx]
    s_win = jnp.einsum('brqhe,brkwhe->bhrqkw', q, kg, preferred_element_type=F32) * scale + bias[None]
    s_ctx = jnp.einsum('brqhe,blhe->bhrql', q, kc, preferred_element_type=F32) * scale
    n_win = kr * GRID_W
    s = jnp.concatenate([s_win.reshape(bsz, nh, rows, GRID_W, n_win), s_ctx], axis=-1)
    p = jax.nn.softmax(s, axis=-1)
    p_win = p[..., :n_win].reshape(s_win.shape).astype(vx.dtype)
    p_ctx = p[..., n_win:].astype(vx.dtype)
    out = (jnp.einsum('bhrqkw,brkwhe->brqhe', p_win, vg)
           + jnp.einsum('bhrql,blhe->brqhe', p_ctx, vc))
    out_x = out.reshape(bsz, seq, nh * hd)
    out_c = ctx_attention(qc, kc, vc).reshape(bsz, -1, nh * hd) if with_ctx else None
    return out_x, out_c


def even_mixer(hx, hc, w_in, w_out, lam_re, lam_im, log_dt, b_re, b_im, c_re, c_im, d_skip,
               glu_w, glu_b, rpb, with_ctx):
    zx = hx @ w_in
    zc = hc @ w_in
    ux, qx, kx, vx = split_cols(zx, EVEN_SPLITS)
    uc, qc, kc, vc = split_cols(zc, EVEN_SPLITS)
    ya_x, ya_c = s5_mixer(ux, uc, lam_re, lam_im, log_dt, b_re, b_im, c_re, c_im, d_skip,
                          glu_w, glu_b, with_ctx)

    def heads(t):
        return t.reshape(t.shape[0], t.shape[1], NA_HEADS, NA_HEAD_DIM)

    yb_x, yb_c = neighborhood_attention(heads(qx), heads(kx), heads(vx), heads(qc), heads(kc),
                                        heads(vc), rpb, with_ctx)
    out_x = jnp.concatenate([ya_x, yb_x], axis=-1) @ w_out
    out_c = jnp.concatenate([ya_c, yb_c], axis=-1) @ w_out if with_ctx else None
    return out_x, out_c


def token_shift(z, mu):
    zp = jnp.pad(z, ((0, 0), (1, 0), (0, 0)))[:, :-1]
    zn = jnp.pad(z, ((0, 0), (0, 1), (0, 0)))[:, 1:]
    return z + mu[0] * (zp - z) + mu[1] * (zn - z)


def rwkv7_streams(z, mu, g_up, w0, w_up, a0, a_up, k_k, k_a):
    bsz, t, _ = z.shape
    z = token_shift(z.astype(F32), mu.astype(F32))
    r, k, v, g_lo, w_lo, a_lo = split_cols(z, RK_SPLITS)

    def heads(u):
        return u.reshape(bsz, t, RK_HEADS, RK_HEAD)

    gate = jax.nn.sigmoid(g_lo) @ g_up.astype(F32)
    w_lo = w_lo.reshape(bsz, t, 2, RK_DECAY_RANK)
    a_lo = a_lo.reshape(bsz, t, 2, RK_ICLR_RANK)
    dirs = []
    for d in range(2):
        w_log = -jax.nn.softplus(-(w0[d] + jnp.tanh(w_lo[:, :, d]) @ w_up[d])) - 0.5
        decay = jnp.exp(-jnp.exp(w_log))
        iclr = jax.nn.sigmoid(a0[d] + a_lo[:, :, d] @ a_up[d])
        kk = heads(k * k_k[d])
        kk = kk / jnp.maximum(jnp.sqrt(jnp.sum(kk * kk, axis=-1, keepdims=True)), 1e-12)
        k_d = heads(k * (1.0 + (iclr - 1.0) * k_a[d]))
        dirs.append((heads(decay), k_d, -kk, kk * heads(iclr)))
    return heads(r), heads(v), gate, dirs


def rwkv7_scan(r, w, k, v, a, b, s0, reverse):
    def step(s, inp):
        r_t, w_t, k_t, v_t, a_t, b_t = inp
        sa = jnp.einsum('bhvk,bhk->bhv', s, a_t)
        s = s * w_t[:, :, None, :] + sa[..., None] * b_t[:, :, None, :] + v_t[..., None] * k_t[:, :, None, :]
        return s, jnp.einsum('bhvk,bhk->bhv', s, r_t)

    xs = tuple(jnp.moveaxis(t, 1, 0) for t in (r, w, k, v, a, b))
    s_final, ys = lax.scan(step, s0, xs, reverse=reverse)
    return s_final, jnp.moveaxis(ys, 0, 1)


def head_group_norm(y, w, b):
    mu = jnp.mean(y, axis=-1, keepdims=True)
    var = jnp.mean(jnp.square(y - mu), axis=-1, keepdims=True)
    yn = (y - mu) * lax.rsqrt(var + RK_GN_EPS)
    return yn.reshape(y.shape[0], y.shape[1], -1) * w.astype(F32) + b.astype(F32)


def rwkv7_mixer(zx, zc, mu, g_up, w0, w_up, a0, a_up, k_k, k_a, r_k, ln_w, ln_b, with_ctx):
    params = (mu, g_up, w0, w_up, a0, a_up, k_k, k_a)
    rx, vx, gx, dirs_x = rwkv7_streams(zx, *params)
    rc, vc, gc, dirs_c = rwkv7_streams(zc, *params)
    r_k = r_k.astype(F32)
    s0 = jnp.zeros((zx.shape[0], RK_HEADS, RK_HEAD, RK_HEAD), F32)
    ys_x, bon_x, ys_c, bon_c = [], [], [], []
    for d, rev in ((0, False), (1, True)):
        w_c, k_c, a_c, b_c = dirs_c[d]
        s_ctx, y_c = rwkv7_scan(rc, w_c, k_c, vc, a_c, b_c, s0, rev)
        w_x, k_x, a_x, b_x = dirs_x[d]
        _, y_x = rwkv7_scan(rx, w_x, k_x, vx, a_x, b_x, s_ctx, rev)
        ys_x.append(y_x)
        bon_x.append(jnp.sum(rx * k_x * r_k, axis=-1, keepdims=True) * vx)
        if with_ctx:
            ys_c.append(y_c)
            bon_c.append(jnp.sum(rc * k_c * r_k, axis=-1, keepdims=True) * vc)

    def finish(ys, bons, gate):
        y = head_group_norm(ys[0] + ys[1], ln_w, ln_b) + (bons[0] + bons[1]).reshape(gate.shape)
        return y * gate

    out_x = finish(ys_x, bon_x, gx).astype(zx.dtype)
    out_c = finish(ys_c, bon_c, gc).astype(zc.dtype) if with_ctx else None
    return out_x, out_c


def rope_2d(t, row_pos, col_pos):
    e = t.shape[-1]
    half = e // 2
    inv = ROPE_THETA ** (-jnp.arange(0, half, 2, dtype=F32) / half)

    def rot(u, pos):
        ang = pos.astype(F32)[:, None] * inv[None, :]
        cos = jnp.cos(ang)[None, :, None, :]
        sin = jnp.sin(ang)[None, :, None, :]
        u1, u2 = u[..., :half // 2], u[..., half // 2:]
        return jnp.concatenate([u1 * cos - u2 * sin, u2 * cos + u1 * sin], axis=-1)

    tf = t.astype(F32)
    return jnp.concatenate([rot(tf[..., :half], row_pos), rot(tf[..., half:], col_pos)], axis=-1).astype(t.dtype)


def gqa_mixer(qx, kx, vx, qc, kc, vc, q_norm, k_norm, with_ctx):
    bsz, seq, nh, hd = qx.shape
    nkv = kx.shape[2]
    grp = nh // nkv
    pos = jnp.arange(seq)
    qx = rope_2d(rms_norm(qx, q_norm), pos // GRID_W, pos % GRID_W)
    kx = rope_2d(rms_norm(kx, k_norm), pos // GRID_W, pos % GRID_W)
    kc = rms_norm(kc, k_norm)
    k_all = jnp.concatenate([kx, kc], axis=1)
    v_all = jnp.concatenate([vx, vc], axis=1)
    n_blk = seq // Q_BLOCK
    q_blocks = jnp.moveaxis(qx.reshape(bsz, n_blk, Q_BLOCK, nkv, grp, hd), 1, 0)
    scale = hd ** -0.5

    def attend(q_blk):
        s = jnp.einsum('bqhge,bkhe->bhgqk', q_blk, k_all, preferred_element_type=F32) * scale
        p = jax.nn.softmax(s, axis=-1).astype(v_all.dtype)
        return jnp.einsum('bhgqk,bkhe->bqhge', p, v_all)

    out_x = jnp.moveaxis(lax.map(attend, q_blocks), 0, 1).reshape(bsz, seq, nh * hd)
    out_c = None
    if with_ctx:
        out_c = ctx_attention(rms_norm(qc, q_norm), kc, vc).reshape(bsz, -1, nh * hd)
    return out_x, out_c


def odd_mixer(hx, hc, w_in, w_out, mu, g_up, w0, w_up, a0, a_up, k_k, k_a, r_k, ln_w, ln_b,
              q_norm, k_norm, with_ctx):
    zx = hx @ w_in
    zc = hc @ w_in
    rkx, qx, kx, vx = split_cols(zx, ODD_SPLITS)
    rkc, qc, kc, vc = split_cols(zc, ODD_SPLITS)
    yc_x, yc_c = rwkv7_mixer(rkx, rkc, mu, g_up, w0, w_up, a0, a_up, k_k, k_a, r_k, ln_w, ln_b, with_ctx)

    def heads(t, n):
        return t.reshape(t.shape[0], t.shape[1], n, GQ_HEAD_DIM)

    yd_x, yd_c = gqa_mixer(heads(qx, GQ_HEADS), heads(kx, GQ_KV_HEADS), heads(vx, GQ_KV_HEADS),
                           heads(qc, GQ_HEADS), heads(kc, GQ_KV_HEADS), heads(vc, GQ_KV_HEADS),
                           q_norm, k_norm, with_ctx)
    out_x = jnp.concatenate([yc_x, yd_x], axis=-1) @ w_out
    out_c = jnp.concatenate([yc_c, yd_c], axis=-1) @ w_out if with_ctx else None
    return out_x, out_c


def swiglu_clamped(z):
    z_glu, z_lin = z[..., ::2], z[..., 1::2]
    z_glu = jnp.minimum(z_glu, SWIGLU_LIMIT)
    z_lin = jnp.clip(z_lin, -SWIGLU_LIMIT, SWIGLU_LIMIT)
    return z_glu * jax.nn.sigmoid(SWIGLU_ALPHA * z_glu) * (z_lin + 1.0)


def moe_ffn(h, w_router, b_router, w1, b1, w2, b2):
    n, dm = h.shape
    logits = jnp.dot(h, w_router, preferred_element_type=F32) + b_router.astype(F32)
    top_val, top_idx = lax.top_k(logits, TOP_K)
    gates = jax.nn.softmax(top_val, axis=-1)
    flat_e = top_idx.reshape(-1)
    order = jnp.argsort(flat_e)
    sorted_e = flat_e[order]
    sorted_tok = order // TOP_K
    sorted_gate = gates.reshape(-1)[order]
    counts = jnp.bincount(flat_e, length=N_EXPERTS)
    padded = (counts + MOE_BLOCK - 1) // MOE_BLOCK * MOE_BLOCK
    pad_end = jnp.cumsum(padded)
    pad_start = pad_end - padded
    grp_start = jnp.cumsum(counts) - counts
    dest = pad_start[sorted_e] + jnp.arange(n * TOP_K) - grp_start[sorted_e]
    n_blocks = -(-(n * TOP_K) // MOE_BLOCK) + N_EXPERTS
    cap = n_blocks * MOE_BLOCK
    slot_tok = jnp.full((cap,), n, jnp.int32).at[dest].set(sorted_tok.astype(jnp.int32))
    slot_gate = jnp.zeros((cap,), F32).at[dest].set(sorted_gate)
    block_exp = jnp.minimum(jnp.searchsorted(pad_end, jnp.arange(n_blocks) * MOE_BLOCK, side='right'),
                            N_EXPERTS - 1)
    h_pad = jnp.concatenate([h, jnp.zeros((1, dm), h.dtype)], axis=0)

    def expert_block(args):
        tok_blk, gate_blk, e = args
        z = h_pad[tok_blk] @ w1[e] + b1[e]
        y = swiglu_clamped(z) @ w2[e] + b2[e]
        return y.astype(F32) * gate_blk[:, None]

    yb = lax.map(expert_block, (slot_tok.reshape(n_blocks, MOE_BLOCK),
                                slot_gate.reshape(n_blocks, MOE_BLOCK), block_exp))
    y = jnp.zeros((n + 1, dm), F32).at[slot_tok].add(yb.reshape(cap, dm))
    return y[:n].astype(h.dtype)


def setup_inputs(seed: int = 0) -> dict:
    key = jax.random.key(seed)
    keys = iter(jax.random.split(key, 64))

    def nrm(shape, std):
        return jax.random.normal(next(keys), shape, F32) * std

    def uni(shape, lo, hi):
        return jax.random.uniform(next(keys), shape, F32, lo, hi)

    d = D_MODEL
    g, p, h = S5_GROUPS, S5_STATE, S5_GROUP
    return {
        'x': nrm((BATCH, SEQ, d), 1.0),
        'c': nrm((BATCH, d), 1.0),
        'ctx': nrm((BATCH, CTX_LEN, d), 1.0),
        'c_ctx': nrm((d,), 1.0),
        'w_mod': nrm((DEPTH, d, N_MOD * d), 0.5 * d ** -0.5),
        'b_mod': nrm((DEPTH, N_MOD * d), 0.02),
        'norm1': 1.0 + nrm((DEPTH, d), 0.02),
        'norm2': 1.0 + nrm((DEPTH, d), 0.02),
        'final_norm': 1.0 + nrm((d,), 0.02),
        'ev_w_in': nrm((N_EVEN, d, EVEN_IN), d ** -0.5),
        'ev_w_out': nrm((N_EVEN, 2 * MIX_W, d), (2 * MIX_W) ** -0.5),
        's5_lam_re': -0.5 + nrm((N_EVEN, 2, g, p), 0.01),
        's5_lam_im': math.pi * jnp.arange(p, dtype=F32) + nrm((N_EVEN, 2, g, p), 0.01),
        's5_log_dt': uni((N_EVEN, 2, g), math.log(S5_DT_MIN), math.log(S5_DT_MAX)),
        's5_b_re': nrm((N_EVEN, 2, g, p, h), (2 * h) ** -0.5),
        's5_b_im': nrm((N_EVEN, 2, g, p, h), (2 * h) ** -0.5),
        's5_c_re': nrm((N_EVEN, 2, g, h, p), p ** -0.5),
        's5_c_im': nrm((N_EVEN, 2, g, h, p), p ** -0.5),
        's5_d': nrm((N_EVEN, MIX_W), 0.5),
        's5_glu_w': nrm((N_EVEN, MIX_W, MIX_W), MIX_W ** -0.5),
        's5_glu_b': nrm((N_EVEN, MIX_W), 0.02),
        'na_rpb': nrm((N_EVEN, NA_HEADS, 2 * WIN_ROWS - 1, 2 * WIN_COLS - 1), 0.1),
        'od_w_in': nrm((N_ODD, d, ODD_IN), d ** -0.5),
        'od_w_out': nrm((N_ODD, 2 * MIX_W, d), (2 * MIX_W) ** -0.5),
        'rk_mu': uni((N_ODD, 2, RK_IN), 0.0, 0.5),
        'rk_g_up': nrm((N_ODD, RK_GATE_RANK, MIX_W), RK_GATE_RANK ** -0.5),
        'rk_w0': uni((N_ODD, 2, MIX_W), -6.0, -1.0),
        'rk_w_up': nrm((N_ODD, 2, RK_DECAY_RANK, MIX_W), 0.1 * RK_DECAY_RANK ** -0.5),
        'rk_a0': nrm((N_ODD, 2, MIX_W), 0.1),
        'rk_a_up': nrm((N_ODD, 2, RK_ICLR_RANK, MIX_W), 0.1 * RK_ICLR_RANK ** -0.5),
        'rk_k_k': 0.85 + nrm((N_ODD, 2, MIX_W), 0.02),
        'rk_k_a': 1.0 + nrm((N_ODD, 2, MIX_W), 0.02),
        'rk_r_k': nrm((N_ODD, RK_HEADS, RK_HEAD), 0.1),
        'rk_ln_w': 1.0 + nrm((N_ODD, MIX_W), 0.02),
        'rk_ln_b': nrm((N_ODD, MIX_W), 0.02),
        'gq_q_norm': 1.0 + nrm((N_ODD, GQ_HEAD_DIM), 0.02),
        'gq_k_norm': 1.0 + nrm((N_ODD, GQ_HEAD_DIM), 0.02),
        'moe_w_router': nrm((DEPTH, d, N_EXPERTS), d ** -0.5),
        'moe_b_router': nrm((DEPTH, N_EXPERTS), 0.01),
        'moe_w1': nrm((DEPTH, N_EXPERTS, d, 2 * D_EXPERT), d ** -0.5),
        'moe_b1': nrm((DEPTH, N_EXPERTS, 2 * D_EXPERT), 0.02),
        'moe_w2': nrm((DEPTH, N_EXPERTS, D_EXPERT, d), D_EXPERT ** -0.5),
        'moe_b2': nrm((DEPTH, N_EXPERTS, d), 0.02),
    }


def reference(x, c, ctx, c_ctx, w_mod, b_mod, norm1, norm2, final_norm,
              ev_w_in, ev_w_out, s5_lam_re, s5_lam_im, s5_log_dt, s5_b_re, s5_b_im, s5_c_re, s5_c_im,
              s5_d, s5_glu_w, s5_glu_b, na_rpb,
              od_w_in, od_w_out, rk_mu, rk_g_up, rk_w0, rk_w_up, rk_a0, rk_a_up, rk_k_k, rk_k_a, rk_r_k,
              rk_ln_w, rk_ln_b, gq_q_norm, gq_k_norm,
              moe_w_router, moe_b_router, moe_w1, moe_b1, moe_w2, moe_b2):
    bsz, seq, dm = x.shape
    n_ctx = ctx.shape[1]
    h_x, h_c = x, ctx
    cond_x = jax.nn.silu(c)
    cond_c = jax.nn.silu(c_ctx)
    for i in range(DEPTH):
        last = i == DEPTH - 1
        j = i // 2
        mod_x = (cond_x @ w_mod[i] + b_mod[i])[:, None, :]
        mod_c = cond_c @ w_mod[i] + b_mod[i]
        sh1x, sc1x, g1x, sh2x, sc2x, g2x = jnp.split(mod_x, N_MOD, axis=-1)
        sh1c, sc1c, g1c, sh2c, sc2c, g2c = jnp.split(mod_c, N_MOD, axis=-1)
        ax = rms_norm(h_x, norm1[i]) * (1.0 + sc1x) + sh1x
        ac = rms_norm(h_c, norm1[i]) * (1.0 + sc1c) + sh1c
        if i % 2 == 0:
            ox, oc = even_mixer(ax, ac, ev_w_in[j], ev_w_out[j], s5_lam_re[j], s5_lam_im[j], s5_log_dt[j],
                                s5_b_re[j], s5_b_im[j], s5_c_re[j], s5_c_im[j], s5_d[j], s5_glu_w[j],
                                s5_glu_b[j], na_rpb[j], not last)
        else:
            ox, oc = odd_mixer(ax, ac, od_w_in[j], od_w_out[j], rk_mu[j], rk_g_up[j], rk_w0[j], rk_w_up[j],
                               rk_a0[j], rk_a_up[j], rk_k_k[j], rk_k_a[j], rk_r_k[j], rk_ln_w[j], rk_ln_b[j],
                               gq_q_norm[j], gq_k_norm[j], not last)
        h_x = h_x + g1x * ox
        fx = (rms_norm(h_x, norm2[i]) * (1.0 + sc2x) + sh2x).reshape(bsz * seq, dm)
        moe_args = (moe_w_router[i], moe_b_router[i], moe_w1[i], moe_b1[i], moe_w2[i], moe_b2[i])
        if last:
            y = moe_ffn(fx, *moe_args)
            h_x = h_x + g2x * y.reshape(bsz, seq, dm)
        else:
            h_c = h_c + g1c * oc
            fc = (rms_norm(h_c, norm2[i]) * (1.0 + sc2c) + sh2c).reshape(bsz * n_ctx, dm)
            y = moe_ffn(jnp.concatenate([fx, fc], axis=0), *moe_args)
            h_x = h_x + g2x * y[:bsz * seq].reshape(bsz, seq, dm)
            h_c = h_c + g2c * y[bsz * seq:].reshape(bsz, n_ctx, dm)
    return rms_norm(h_x, final_norm)
```

```python
import functools
import math

import numpy as np
import jax
import jax.numpy as jnp
from jax import lax
from jax.experimental import pallas as pl
from jax.experimental.pallas import tpu as pltpu

F32 = jnp.float32
BF16 = jnp.bfloat16

V7X_LANES = 128
V7X_SUBLANES = 8
V7X_VMEM_BYTES = 64 * 1024 * 1024
VMEM_LIMIT = 56 * 1024 * 1024

GRID_W = 64
N_MOD = 6
NORM_EPS = 1e-6
S5_GROUP = 16
S5_STATE = 64
S5_CHUNK = 16
NA_HEADS = 8
WIN_ROWS = 8
WIN_COLS = 16
NA_QROWS = 4
NA_KROWS = NA_QROWS + WIN_ROWS - 1
RK_HEAD = 64
RK_DECAY_RANK = 64
RK_ICLR_RANK = 64
RK_GATE_RANK = 128
RK_GN_EPS = 64e-5
RK_TBLOCK = 128
GQ_HEAD_DIM = 128
ROPE_THETA = 10000.0
GQ_TQ = 128
GQ_TK = 256
N_EXPERTS = 32
TOP_K = 4
SWIGLU_ALPHA = 1.702
SWIGLU_LIMIT = 7.0
MOE_TM = 256
NEG_BIG = -1e30
HI = lax.Precision.HIGHEST


def _params(*sem):
    return pltpu.CompilerParams(dimension_semantics=sem, vmem_limit_bytes=VMEM_LIMIT)


def _tile(n, target, align):
    best = None
    t = align
    while t <= min(n, target):
        if n % t == 0:
            best = t
        t += align
    return n if best is None else best


def _mm_kernel(x_ref, w_ref, o_ref):
    o_ref[...] = jnp.dot(x_ref[...], w_ref[...], preferred_element_type=F32).astype(o_ref.dtype)


def matmul(x, w, out_dtype=F32):
    m, k = x.shape
    n = w.shape[1]
    x = x.astype(BF16)
    w = w.astype(BF16)
    m_pad = -(-m // 16) * 16
    if m_pad != m:
        x = jnp.pad(x, ((0, m_pad - m), (0, 0)))
    tm = _tile(m_pad, 512, 16)
    tn = _tile(n, 2048, V7X_LANES)
    out = pl.pallas_call(
        _mm_kernel,
        grid=(n // tn, m_pad // tm),
        in_specs=[pl.BlockSpec((tm, k), lambda j, i: (i, 0)),
                  pl.BlockSpec((k, tn), lambda j, i: (0, j))],
        out_specs=pl.BlockSpec((tm, tn), lambda j, i: (i, j)),
        out_shape=jax.ShapeDtypeStruct((m_pad, n), out_dtype),
        compiler_params=_params("parallel", "parallel"),
        name="matmul",
    )(x, w)
    return out[:m] if m_pad != m else out


def _flash_kernel(q_ref, k_ref, v_ref, o_ref, m_sc, l_sc, acc_sc, *, grp, hd, tq, tk, n_ctx_qblocks,
                  nk_ctx, nk_all):
    qi = pl.program_id(2)
    nk = jnp.where(qi < n_ctx_qblocks, nk_ctx, nk_all)
    q = jnp.concatenate([q_ref[0, :, g * hd:(g + 1) * hd] for g in range(grp)], axis=0)
    m_sc[...] = jnp.full(m_sc.shape, NEG_BIG, F32)
    l_sc[...] = jnp.zeros(l_sc.shape, F32)
    acc_sc[...] = jnp.zeros(acc_sc.shape, F32)

    def body(j, carry):
        off = pl.multiple_of(j * tk, tk)
        k = k_ref[0, pl.ds(off, tk), :]
        v = v_ref[0, pl.ds(off, tk), :]
        s = lax.dot_general(q, k, (((1,), (1,)), ((), ())), preferred_element_type=F32)
        m_prev = m_sc[...]
        m_new = jnp.maximum(m_prev, jnp.max(s, axis=-1, keepdims=True))
        alpha = jnp.exp(m_prev - m_new)
        p = jnp.exp(s - m_new)
        l_sc[...] = alpha * l_sc[...] + jnp.sum(p, axis=-1, keepdims=True)
        acc_sc[...] = alpha * acc_sc[...] + jnp.dot(p.astype(BF16), v, preferred_element_type=F32)
        m_sc[...] = m_new
        return carry

    lax.fori_loop(0, nk, body, 0)
    o = acc_sc[...] / l_sc[...]
    for g in range(grp):
        o_ref[0, :, g * hd:(g + 1) * hd] = o[g * tq:(g + 1) * tq].astype(o_ref.dtype)


def gqa_attention(q, k, v, n_ctx, n_kv_heads):
    b, t, qw = q.shape
    hd = GQ_HEAD_DIM
    grp = qw // hd // n_kv_heads
    tq, tk = GQ_TQ, GQ_TK
    assert n_ctx % tq == 0 and n_ctx % tk == 0 and t % tk == 0 and t % tq == 0
    kern = functools.partial(_flash_kernel, grp=grp, hd=hd, tq=tq, tk=tk, n_ctx_qblocks=n_ctx // tq,
                             nk_ctx=n_ctx // tk, nk_all=t // tk)
    rows = grp * tq
    return pl.pallas_call(
        kern,
        grid=(b, n_kv_heads, t // tq),
        in_specs=[pl.BlockSpec((1, tq, grp * hd), lambda bi, h, i: (bi, i, h)),
                  pl.BlockSpec((1, t, hd), lambda bi, h, i: (bi, 0, h)),
                  pl.BlockSpec((1, t, hd), lambda bi, h, i: (bi, 0, h))],
        out_specs=pl.BlockSpec((1, tq, grp * hd), lambda bi, h, i: (bi, i, h)),
        out_shape=jax.ShapeDtypeStruct((b, t, qw), BF16),
        scratch_shapes=[pltpu.VMEM((rows, 1), F32), pltpu.VMEM((rows, 1), F32),
                        pltpu.VMEM((rows, hd), F32)],
        compiler_params=_params("parallel", "parallel", "parallel"),
        name="gqa_attention",
    )(q, k, v)


def _na_kernel(q_ref, k_ref, v_ref, bias_ref, o_ref, *, n_ctx, rows, n_blocks):
    j = pl.program_id(2)
    q = q_ref[0]
    kc = k_ref[0, 0:n_ctx, :]
    vc = v_ref[0, 0:n_ctx, :]
    dn = (((1,), (1,)), ((), ()))
    s_ctx = lax.dot_general(q, kc, dn, preferred_element_type=F32)

    @pl.when(j == 0)
    def _():
        m = jnp.max(s_ctx, axis=-1, keepdims=True)
        p = jnp.exp(s_ctx - m)
        l = jnp.sum(p, axis=-1, keepdims=True)
        o = jnp.dot(p.astype(BF16), vc, preferred_element_type=F32) / l
        o_ref[0] = o.astype(o_ref.dtype)

    @pl.when(j > 0)
    def _():
        jj = j - 1
        ks = jnp.clip(NA_QROWS * jj - WIN_ROWS // 2, 0, rows - NA_KROWS)
        off = pl.multiple_of(n_ctx + ks * GRID_W, GRID_W)
        kw = k_ref[0, pl.ds(off, NA_KROWS * GRID_W), :]
        vw = v_ref[0, pl.ds(off, NA_KROWS * GRID_W), :]
        pat = jnp.where(jj == 0, 0, jnp.where(jj == n_blocks - 1, 2, 1))
        s_win = lax.dot_general(q, kw, dn, preferred_element_type=F32) + bias_ref[pat]
        m = jnp.maximum(jnp.max(s_win, axis=-1, keepdims=True), jnp.max(s_ctx, axis=-1, keepdims=True))
        p_win = jnp.exp(s_win - m)
        p_ctx = jnp.exp(s_ctx - m)
        l = jnp.sum(p_win, axis=-1, keepdims=True) + jnp.sum(p_ctx, axis=-1, keepdims=True)
        o = (jnp.dot(p_win.astype(BF16), vw, preferred_element_type=F32)
             + jnp.dot(p_ctx.astype(BF16), vc, preferred_element_type=F32)) / l
        o_ref[0] = o.astype(o_ref.dtype)


def _na_bias(rpb, rows):
    n_blocks = rows // NA_QROWS
    col = jnp.arange(GRID_W)
    col_start = jnp.clip(col - WIN_COLS // 2, 0, GRID_W - WIN_COLS)
    col_ok = (col[None, :] >= col_start[:, None]) & (col[None, :] < col_start[:, None] + WIN_COLS)
    dc = jnp.clip(col[None, :] - col[:, None] + (WIN_COLS - 1), 0, 2 * WIN_COLS - 2)
    pats = []
    for jj in (0, 1, n_blocks - 1):
        ks = min(max(NA_QROWS * jj - WIN_ROWS // 2, 0), rows - NA_KROWS)
        r = NA_QROWS * jj + jnp.arange(NA_QROWS)
        kr = ks + jnp.arange(NA_KROWS)
        r_start = jnp.clip(r - WIN_ROWS // 2, 0, rows - WIN_ROWS)
        row_ok = (kr[None, :] >= r_start[:, None]) & (kr[None, :] < r_start[:, None] + WIN_ROWS)
        dr = jnp.clip(kr[None, :] - r[:, None] + (WIN_ROWS - 1), 0, 2 * WIN_ROWS - 2)
        bias = rpb.astype(F32)[:, dr[:, None, :, None], dc[None, :, None, :]]
        ok = row_ok[:, None, :, None] & col_ok[None, :, None, :]
        bias = jnp.where(ok[None], bias, NEG_BIG)
        pats.append(bias.reshape(rpb.shape[0], NA_QROWS * GRID_W, NA_KROWS * GRID_W))
    return jnp.stack(pats, axis=0)


def neighborhood_attention(q, k, v, rpb, n_ctx):
    b, t, width = q.shape
    hd = width // NA_HEADS
    seq = t - n_ctx
    rows = seq // GRID_W
    qb = NA_QROWS * GRID_W
    assert n_ctx == qb and rows % NA_QROWS == 0 and rows >= NA_KROWS and WIN_ROWS <= rows
    n_blocks = rows // NA_QROWS
    bias = _na_bias(rpb, rows)
    kern = functools.partial(_na_kernel, n_ctx=n_ctx, rows=rows, n_blocks=n_blocks)
    kb = NA_KROWS * GRID_W
    return pl.pallas_call(
        kern,
        grid=(NA_HEADS, b, n_blocks + 1),
        in_specs=[pl.BlockSpec((1, qb, hd), lambda h, bi, j: (bi, j, h)),
                  pl.BlockSpec((1, t, hd), lambda h, bi, j: (bi, 0, h)),
                  pl.BlockSpec((1, t, hd), lambda h, bi, j: (bi, 0, h)),
                  pl.BlockSpec((3, None, qb, kb), lambda h, bi, j: (0, h, 0, 0))],
        out_specs=pl.BlockSpec((1, qb, hd), lambda h, bi, j: (bi, j, h)),
        out_shape=jax.ShapeDtypeStruct((b, t, width), BF16),
        compiler_params=_params("parallel", "parallel", "parallel"),
        name="neighborhood_attention",
    )(q, k, v, bias)


def _s5_kernel(u_ref, t_ref, n_ref, m_ref, d_ref, y_ref, hl_sc, hp_sc, *, nc_ctx, nc_all, bsz):
    u = u_ref[0]
    for part in range(4):
        hl_sc[part] = jnp.dot(u, n_ref[0, part], preferred_element_type=F32)
    zero = jnp.zeros((bsz, S5_STATE), F32)

    def scan(direction):
        dr = d_ref[0, 2 * direction:2 * direction + 1, :]
        di = d_ref[0, 2 * direction + 1:2 * direction + 2, :]

        def body(step, carry):
            hr, hi = carry
            if direction == 0:
                c = step
            else:
                c = jnp.where(step < nc_ctx, nc_ctx - 1 - step, nc_all - 1 - (step - nc_ctx))
            r0 = pl.multiple_of(c * bsz, bsz)
            hp_sc[2 * direction, pl.ds(r0, bsz), :] = hr
            hp_sc[2 * direction + 1, pl.ds(r0, bsz), :] = hi
            lr = hl_sc[2 * direction, pl.ds(r0, bsz), :]
            li = hl_sc[2 * direction + 1, pl.ds(r0, bsz), :]
            return dr * hr - di * hi + lr, dr * hi + di * hr + li

        lax.fori_loop(0, nc_all, body, (zero, zero))

    scan(0)
    scan(1)
    y = jnp.dot(u, t_ref[0], preferred_element_type=F32)
    for part in range(4):
        y = y + jnp.dot(hp_sc[part].astype(BF16), m_ref[0, part], preferred_element_type=F32)
    y_ref[0] = y


def _s5_tables(lam_re, lam_im, log_dt, b_re, b_im, c_re, c_im):
    c = S5_CHUNK
    g, p = lam_re.shape[1], lam_re.shape[2]
    h = b_re.shape[-1]
    toe = 0.0
    n_parts, m_parts, d_parts = [], [], []
    tt = jnp.arange(c)
    for d in range(2):
        lr, li = lam_re[d].astype(F32), lam_im[d].astype(F32)
        dt = jnp.exp(log_dt[d].astype(F32))[:, None]

        def lam_pow(e):
            e = jnp.asarray(e, F32)[..., None, None]
            mag = jnp.exp(e * lr * dt)
            return mag * jnp.cos(e * li * dt), mag * jnp.sin(e * li * dt)

        lbr, lbi = lam_pow(1)
        nr, ni = lbr - 1.0, lbi
        den = lr * lr + li * li
        fr, fi = (nr * lr + ni * li) / den, (ni * lr - nr * li) / den
        br, bi = b_re[d].astype(F32), b_im[d].astype(F32)
        bbr = fr[..., None] * br - fi[..., None] * bi
        bbi = fr[..., None] * bi + fi[..., None] * br
        cr, ci = c_re[d].astype(F32), c_im[d].astype(F32)
        pr, pi = lam_pow(tt)
        lbbr = pr[..., None] * bbr - pi[..., None] * bbi
        lbbi = pr[..., None] * bbi + pi[..., None] * bbr
        kern = (jnp.einsum('gop,tgpi->tgoi', cr, lbbr, precision=HI)
                - jnp.einsum('gop,tgpi->tgoi', ci, lbbi, precision=HI))
        lag = (tt[None, :] - tt[:, None]) if d == 0 else (tt[:, None] - tt[None, :])
        ok = lag >= 0
        kt = kern[jnp.clip(lag, 0, c - 1)]
        kt = jnp.where(ok[:, :, None, None, None], kt, 0.0)
        toe = toe + jnp.transpose(kt, (2, 0, 4, 1, 3)).reshape(g, c * h, c * h)
        e_in = (c - 1 - tt) if d == 0 else tt
        qr, qi = lam_pow(e_in)
        n_r = qr[..., None] * bbr - qi[..., None] * bbi
        n_i = qr[..., None] * bbi + qi[..., None] * bbr
        n_parts += [jnp.transpose(n_r, (1, 0, 3, 2)).reshape(g, c * h, p),
                    jnp.transpose(n_i, (1, 0, 3, 2)).reshape(g, c * h, p)]
        e_out = (tt + 1) if d == 0 else (c - tt)
        sr, si = lam_pow(e_out)
        clr = cr[None] * sr[:, :, None, :] - ci[None] * si[:, :, None, :]
        cli = cr[None] * si[:, :, None, :] + ci[None] * sr[:, :, None, :]
        m_parts += [jnp.transpose(clr, (1, 3, 0, 2)).reshape(g, p, c * h),
                    -jnp.transpose(cli, (1, 3, 0, 2)).reshape(g, p, c * h)]
        dcr, dci = lam_pow(c)
        d_parts += [dcr, dci]
    return (toe.astype(BF16), jnp.stack(n_parts, 1).astype(BF16), jnp.stack(m_parts, 1).astype(BF16),
            jnp.stack(d_parts, 1))


def s5_scan(u, n_ctx, lam_re, lam_im, log_dt, b_re, b_im, c_re, c_im):
    bsz, t, width = u.shape
    g = width // S5_GROUP
    c = S5_CHUNK
    assert bsz == V7X_SUBLANES and t % c == 0 and n_ctx % c == 0
    nc = t // c
    rows = nc * bsz
    toe, n_tab, m_tab, d_tab = _s5_tables(lam_re, lam_im, log_dt, b_re, b_im, c_re, c_im)
    uc = u.astype(BF16).reshape(bsz, nc, c, g, S5_GROUP)
    uc = jnp.transpose(uc, (3, 1, 0, 2, 4)).reshape(g, rows, c * S5_GROUP)
    kern = functools.partial(_s5_kernel, nc_ctx=n_ctx // c, nc_all=nc, bsz=bsz)
    cw = c * S5_GROUP
    y = pl.pallas_call(
        kern,
        grid=(g,),
        in_specs=[pl.BlockSpec((1, rows, cw), lambda i: (i, 0, 0)),
                  pl.BlockSpec((1, cw, cw), lambda i: (i, 0, 0)),
                  pl.BlockSpec((1, 4, cw, S5_STATE), lambda i: (i, 0, 0, 0)),
                  pl.BlockSpec((1, 4, S5_STATE, cw), lambda i: (i, 0, 0, 0)),
                  pl.BlockSpec((1, 4, S5_STATE), lambda i: (i, 0, 0))],
        out_specs=pl.BlockSpec((1, rows, cw), lambda i: (i, 0, 0)),
        out_shape=jax.ShapeDtypeStruct((g, rows, cw), F32),
        scratch_shapes=[pltpu.VMEM((4, rows, S5_STATE), F32), pltpu.VMEM((4, rows, S5_STATE), F32)],
        compiler_params=_params("parallel"),
        name="s5_scan",
    )(uc, toe, n_tab, m_tab, d_tab)
    y = y.reshape(g, nc, bsz, c, S5_GROUP)
    return jnp.transpose(y, (2, 1, 3, 0, 4)).reshape(bsz, t, width)


def _rwkv_kernel(r_ref, v_ref, w_ref, k_ref, a_ref, b_ref, y_ref, s_sc, vt_sc, yt_sc, *, heads, tb):
    d = pl.program_id(0)
    n = pl.program_id(2)
    pairs = heads // 2

    @pl.when(n == 0)
    def _():
        s_sc[...] = jnp.zeros(s_sc.shape, F32)

    for p in range(pairs):
        vt_sc[p] = v_ref[0, :, p * 128:(p + 1) * 128].T
    yt_sc[...] = jnp.zeros(yt_sc.shape, F32)
    lane_t = lax.broadcasted_iota(jnp.int32, (RK_HEAD, tb), 1)
    lane_k = lax.broadcasted_iota(jnp.int32, (1, 2 * RK_HEAD), 1)
    half = [lane_k < RK_HEAD, lane_k >= RK_HEAD]

    def step(t, carry):
        te = jnp.where(d == 0, t, tb - 1 - t)
        r_row = r_ref[0, pl.ds(te, 1), :]
        w_row = w_ref[0, 0, pl.ds(te, 1), :]
        k_row = k_ref[0, 0, pl.ds(te, 1), :]
        a_row = a_ref[0, 0, pl.ds(te, 1), :]
        b_row = b_ref[0, 0, pl.ds(te, 1), :]
        sel = lane_t == te
        for j in range(heads):
            p, odd = j // 2, j % 2
            sl = slice(p * 128, (p + 1) * 128)
            rows = slice(odd * RK_HEAD, (odd + 1) * RK_HEAD)
            s = s_sc[j]
            k_ = jnp.where(half[odd], k_row[:, sl], 0.0)
            b_ = jnp.where(half[odd], b_row[:, sl], 0.0)
            vcol = jnp.sum(jnp.where(sel, vt_sc[p, rows, :], 0.0), axis=1, keepdims=True)
            sa = jnp.sum(s * a_row[:, sl], axis=1, keepdims=True)
            s = s * w_row[:, sl] + sa * b_ + vcol * k_
            s_sc[j] = s
            ycol = jnp.sum(s * r_row[:, sl], axis=1, keepdims=True)
            yt_sc[p, rows, :] = jnp.where(sel, ycol, yt_sc[p, rows, :])
        return carry

    lax.fori_loop(0, tb, step, 0)
    for p in range(pairs):
        y_ref[0, 0, :, p * 128:(p + 1) * 128] = yt_sc[p].T


def rwkv7_scan(r, v, w, k, a, b, n_ctx):
    bsz, t, width = r.shape
    heads = width // RK_HEAD
    tb = RK_TBLOCK
    assert heads % 2 == 0 and n_ctx % tb == 0 and t % tb == 0
    ncb, nb = n_ctx // tb, t // tb

    def tblock(d, n):
        rev = jnp.where(n < ncb, ncb - 1 - n, ncb + (nb - 1 - n))
        return jnp.where(d == 0, n, rev)

    shared = pl.BlockSpec((1, tb, width), lambda d, bi, n: (bi, tblock(d, n), 0))
    per_dir = pl.BlockSpec((1, 1, tb, width), lambda d, bi, n: (d, bi, tblock(d, n), 0))
    kern = functools.partial(_rwkv_kernel, heads=heads, tb=tb)
    return pl.pallas_call(
        kern,
        grid=(2, bsz, nb),
        in_specs=[shared, shared, per_dir, per_dir, per_dir, per_dir],
        out_specs=per_dir,
        out_shape=jax.ShapeDtypeStruct((2, bsz, t, width), F32),
        scratch_shapes=[pltpu.VMEM((heads, RK_HEAD, 2 * RK_HEAD), F32),
                        pltpu.VMEM((heads // 2, 2 * RK_HEAD, tb), F32),
                        pltpu.VMEM((heads // 2, 2 * RK_HEAD, tb), F32)],
        compiler_params=_params("parallel", "parallel", "arbitrary"),
        name="rwkv7_scan",
    )(r, v, w, k, a, b)


def _moe_kernel(be_ref, nb_ref, x_ref, g_ref, w1g_ref, w1l_ref, b1g_ref, b1l_ref, w2_ref, b2_ref, o_ref):
    i = pl.program_id(0)

    @pl.when(i < nb_ref[0])
    def _():
        x = x_ref[...]
        zg = jnp.dot(x, w1g_ref[0], preferred_element_type=F32) + b1g_ref[0]
        zl = jnp.dot(x, w1l_ref[0], preferred_element_type=F32) + b1l_ref[0]
        zg = jnp.minimum(zg, SWIGLU_LIMIT)
        zl = jnp.clip(zl, -SWIGLU_LIMIT, SWIGLU_LIMIT)
        act = zg * jax.nn.sigmoid(SWIGLU_ALPHA * zg) * (zl + 1.0)
        y = jnp.dot(act.astype(BF16), w2_ref[0], preferred_element_type=F32) + b2_ref[0]
        o_ref[...] = y * g_ref[...]

    @pl.when(i >= nb_ref[0])
    def _():
        o_ref[...] = jnp.zeros(o_ref.shape, F32)


def moe_ffn(h, w_router, b_router, w1, b1, w2, b2):
    n, dm = h.shape
    n_exp = w_router.shape[1]
    de = w2.shape[1]
    tm = MOE_TM
    logits = jnp.dot(h, w_router, precision=HI, preferred_element_type=F32) + b_router.astype(F32)
    top_val, top_idx = lax.top_k(logits, TOP_K)
    gates = jax.nn.softmax(top_val, axis=-1)
    flat_e = top_idx.reshape(-1).astype(jnp.int32)
    onehot = (flat_e[:, None] == jnp.arange(n_exp, dtype=jnp.int32)[None, :]).astype(jnp.int32)
    csum = jnp.cumsum(onehot, axis=0)
    rank = jnp.take_along_axis(csum, flat_e[:, None], axis=1)[:, 0] - 1
    counts = csum[-1]
    padded = (counts + tm - 1) // tm * tm
    pad_end = jnp.cumsum(padded)
    pad_start = pad_end - padded
    dest = pad_start[flat_e] + rank
    n_blocks = -(-(n * TOP_K) // tm) + n_exp
    cap = n_blocks * tm
    tok = jnp.arange(n * TOP_K, dtype=jnp.int32) // TOP_K
    slot_tok = jnp.full((cap,), n, jnp.int32).at[dest].set(tok)
    slot_gate = jnp.zeros((cap,), F32).at[dest].set(gates.reshape(-1))
    block_exp = jnp.minimum(jnp.searchsorted(pad_end, jnp.arange(n_blocks) * tm, side='right'),
                            n_exp - 1).astype(jnp.int32)
    n_used = (pad_end[-1] // tm).astype(jnp.int32).reshape(1)
    h_pad = jnp.concatenate([h.astype(BF16), jnp.zeros((1, dm), BF16)], axis=0)
    x_sorted = h_pad[slot_tok]

    w1g = w1[:, :, 0::2].astype(BF16)
    w1l = w1[:, :, 1::2].astype(BF16)
    b1g = b1[:, None, 0::2].astype(F32)
    b1l = b1[:, None, 1::2].astype(F32)
    w2b = w2.astype(BF16)
    b2r = b2[:, None, :].astype(F32)

    grid_spec = pltpu.PrefetchScalarGridSpec(
        num_scalar_prefetch=2,
        grid=(n_blocks,),
        in_specs=[pl.BlockSpec((tm, dm), lambda i, be, nb: (i, 0)),
                  pl.BlockSpec((tm, 1), lambda i, be, nb: (i, 0)),
                  pl.BlockSpec((1, dm, de), lambda i, be, nb: (be[i], 0, 0)),
                  pl.BlockSpec((1, dm, de), lambda i, be, nb: (be[i], 0, 0)),
                  pl.BlockSpec((1, 1, de), lambda i, be, nb: (be[i], 0, 0)),
                  pl.BlockSpec((1, 1, de), lambda i, be, nb: (be[i], 0, 0)),
                  pl.BlockSpec((1, de, dm), lambda i, be, nb: (be[i], 0, 0)),
                  pl.BlockSpec((1, 1, dm), lambda i, be, nb: (be[i], 0, 0))],
        out_specs=pl.BlockSpec((tm, dm), lambda i, be, nb: (i, 0)),
    )
    yb = pl.pallas_call(
        _moe_kernel,
        grid_spec=grid_spec,
        out_shape=jax.ShapeDtypeStruct((cap, dm), F32),
        compiler_params=_params("arbitrary"),
        name="moe_experts",
    )(block_exp, n_used, x_sorted, slot_gate[:, None], w1g, w1l, b1g, b1l, w2b, b2r)
    return yb[dest].reshape(n, TOP_K, dm).sum(axis=1)


def _rms(x, g, eps=NORM_EPS):
    xf = x.astype(F32)
    return xf * lax.rsqrt(jnp.mean(xf * xf, axis=-1, keepdims=True) + eps) * g.astype(F32)


def _modulated(h, n_ctx, norm_w, scale_c, shift_c, scale_x, shift_x):
    hn = _rms(h, norm_w)
    ac = hn[:, :n_ctx] * (1.0 + scale_c) + shift_c
    ax = hn[:, n_ctx:] * (1.0 + scale_x[:, None, :]) + shift_x[:, None, :]
    return jnp.concatenate([ac, ax], axis=1)


def _gated_add(h, n_ctx, out, gate_c, gate_x):
    return jnp.concatenate([h[:, :n_ctx] + gate_c * out[:, :n_ctx],
                            h[:, n_ctx:] + gate_x[:, None, :] * out[:, n_ctx:]], axis=1)


def _even_mixer(a, n_ctx, w_in, w_out, lam_re, lam_im, log_dt, b_re, b_im, c_re, c_im, d_skip,
                glu_w, glu_b, rpb):
    bsz, t, dm = a.shape
    mix_w = dm // 2
    z = matmul(a.reshape(bsz * t, dm), w_in).reshape(bsz, t, 4 * mix_w)
    u = z[..., :mix_w]
    hd = mix_w // NA_HEADS
    q = (z[..., mix_w:2 * mix_w] * (hd ** -0.5)).astype(BF16)
    k = z[..., 2 * mix_w:3 * mix_w].astype(BF16)
    v = z[..., 3 * mix_w:].astype(BF16)
    y = d_skip.astype(F32) * u + s5_scan(u, n_ctx, lam_re, lam_im, log_dt, b_re, b_im, c_re, c_im)
    gl = jax.nn.gelu(y)
    ya = gl * jax.nn.sigmoid(matmul(gl.reshape(bsz * t, mix_w), glu_w).reshape(bsz, t, mix_w)
                             + glu_b.astype(F32))
    yb = neighborhood_attention(q, k, v, rpb, n_ctx)
    cat = jnp.concatenate([ya.astype(BF16), yb], axis=-1)
    return matmul(cat.reshape(bsz * t, dm), w_out).reshape(bsz, t, dm)


def _token_shift(z, mu):
    zp = jnp.pad(z, ((0, 0), (1, 0), (0, 0)))[:, :-1]
    zn = jnp.pad(z, ((0, 0), (0, 1), (0, 0)))[:, 1:]
    return z + mu[0] * (zp - z) + mu[1] * (zn - z)


def _rope_2d(tq, row_pos, col_pos):
    e = tq.shape[-1]
    half = e // 2
    inv = ROPE_THETA ** (-jnp.arange(0, half, 2, dtype=F32) / half)

    def rot(u, pos):
        ang = pos.astype(F32)[:, None] * inv[None, :]
        cos = jnp.cos(ang)[None, :, None, :]
        sin = jnp.sin(ang)[None, :, None, :]
        u1, u2 = u[..., :half // 2], u[..., half // 2:]
        return jnp.concatenate([u1 * cos - u2 * sin, u2 * cos + u1 * sin], axis=-1)

    return jnp.concatenate([rot(tq[..., :half], row_pos), rot(tq[..., half:], col_pos)], axis=-1)


def _odd_mixer(a, n_ctx, w_in, w_out, mu, g_up, w0, w_up, a0, a_up, k_k, k_a, r_k, ln_w, ln_b,
               q_norm, k_norm):
    bsz, t, dm = a.shape
    mix_w = dm // 2
    heads = mix_w // RK_HEAD
    rk_in = 3 * mix_w + RK_GATE_RANK + 2 * RK_DECAY_RANK + 2 * RK_ICLR_RANK
    gq_heads = mix_w // GQ_HEAD_DIM
    kv_heads = gq_heads // 4
    kv_w = kv_heads * GQ_HEAD_DIM
    z = matmul(a.reshape(bsz * t, dm), w_in).reshape(bsz, t, rk_in + mix_w + 2 * kv_w)

    zr = z[..., :rk_in]
    muf = mu.astype(F32)
    zs = jnp.concatenate([_token_shift(zr[:, :n_ctx], muf), _token_shift(zr[:, n_ctx:], muf)], axis=1)
    r = zs[..., :mix_w]
    kk0 = zs[..., mix_w:2 * mix_w]
    v = zs[..., 2 * mix_w:3 * mix_w]
    o = 3 * mix_w
    g_lo = zs[..., o:o + RK_GATE_RANK]
    w_lo = zs[..., o + RK_GATE_RANK:o + RK_GATE_RANK + 2 * RK_DECAY_RANK]
    a_lo = zs[..., o + RK_GATE_RANK + 2 * RK_DECAY_RANK:]
    m = bsz * t

    def hshape(x):
        return x.reshape(bsz, t, heads, RK_HEAD)

    gate = matmul(jax.nn.sigmoid(g_lo).reshape(m, RK_GATE_RANK), g_up).reshape(bsz, t, mix_w)
    ws, ks, as_, bs = [], [], [], []
    for d in range(2):
        wl = jnp.tanh(w_lo[..., d * RK_DECAY_RANK:(d + 1) * RK_DECAY_RANK]).reshape(m, RK_DECAY_RANK)
        w_log = -jax.nn.softplus(-(w0[d] + matmul(wl, w_up[d]).reshape(bsz, t, mix_w))) - 0.5
        decay = jnp.exp(-jnp.exp(w_log))
        al = a_lo[..., d * RK_ICLR_RANK:(d + 1) * RK_ICLR_RANK].reshape(m, RK_ICLR_RANK)
        iclr = jax.nn.sigmoid(a0[d] + matmul(al, a_up[d]).reshape(bsz, t, mix_w))
        kk = hshape(kk0 * k_k[d])
        kk = kk / jnp.maximum(jnp.sqrt(jnp.sum(kk * kk, axis=-1, keepdims=True)), 1e-12)
        kk = kk.reshape(bsz, t, mix_w)
        k_d = kk0 * (1.0 + (iclr - 1.0) * k_a[d])
        ws.append(decay)
        ks.append(k_d)
        as_.append(-kk)
        bs.append(kk * iclr)
    w_s, k_s, a_s, b_s = (jnp.stack(x, axis=0) for x in (ws, ks, as_, bs))
    ys = rwkv7_scan(r, v, w_s, k_s, a_s, b_s, n_ctx)
    ysum = hshape(ys[0] + ys[1])
    mean = jnp.mean(ysum, axis=-1, keepdims=True)
    var = jnp.mean(jnp.square(ysum - mean), axis=-1, keepdims=True)
    yn = ((ysum - mean) * lax.rsqrt(var + RK_GN_EPS)).reshape(bsz, t, mix_w)
    yn = yn * ln_w.astype(F32) + ln_b.astype(F32)
    rkf = r_k.astype(F32)
    bonus = 0.0
    for d in range(2):
        bonus = bonus + jnp.sum(hshape(r) * hshape(k_s[d]) * rkf, axis=-1, keepdims=True) * hshape(v)
    yc = (yn + bonus.reshape(bsz, t, mix_w)) * gate

    o = rk_in
    q = z[..., o:o + mix_w].reshape(bsz, t, gq_heads, GQ_HEAD_DIM)
    k = z[..., o + mix_w:o + mix_w + kv_w].reshape(bsz, t, kv_heads, GQ_HEAD_DIM)
    vv = z[..., o + mix_w + kv_w:]
    qn = _rms(q, q_norm)
    kn = _rms(k, k_norm)
    pos = jnp.arange(t - n_ctx)
    qn = jnp.concatenate([qn[:, :n_ctx], _rope_2d(qn[:, n_ctx:], pos // GRID_W, pos % GRID_W)], axis=1)
    kn = jnp.concatenate([kn[:, :n_ctx], _rope_2d(kn[:, n_ctx:], pos // GRID_W, pos % GRID_W)], axis=1)
    qb = (qn * (GQ_HEAD_DIM ** -0.5)).reshape(bsz, t, mix_w).astype(BF16)
    kb = kn.reshape(bsz, t, kv_w).astype(BF16)
    yd = gqa_attention(qb, kb, vv.astype(BF16), n_ctx, kv_heads)

    cat = jnp.concatenate([yc.astype(BF16), yd], axis=-1)
    return matmul(cat.reshape(bsz * t, dm), w_out).reshape(bsz, t, dm)


def kernel(x, c, ctx, c_ctx, w_mod, b_mod, norm1, norm2, final_norm, ev_w_in, ev_w_out, s5_lam_re, s5_lam_im, s5_log_dt, s5_b_re, s5_b_im, s5_c_re, s5_c_im, s5_d, s5_glu_w, s5_glu_b, na_rpb, od_w_in, od_w_out, rk_mu, rk_g_up, rk_w0, rk_w_up, rk_a0, rk_a_up, rk_k_k, rk_k_a, rk_r_k, rk_ln_w, rk_ln_b, gq_q_norm, gq_k_norm, moe_w_router, moe_b_router, moe_w1, moe_b1, moe_w2, moe_b2):
    bsz, seq, dm = x.shape
    n_ctx = ctx.shape[1]
    depth = w_mod.shape[0]
    t = n_ctx + seq
    h = jnp.concatenate([ctx, x], axis=1).astype(F32)
    cond = jnp.concatenate([jax.nn.silu(c), jax.nn.silu(c_ctx)[None, :]], axis=0)
    for i in range(depth):
        last = i == depth - 1
        j = i // 2
        mod = matmul(cond, w_mod[i]) + b_mod[i].astype(F32)
        sh1, sc1, g1, sh2, sc2, g2 = jnp.split(mod, N_MOD, axis=-1)
        a = _modulated(h, n_ctx, norm1[i], sc1[bsz], sh1[bsz], sc1[:bsz], sh1[:bsz])
        if i % 2 == 0:
            out = _even_mixer(a, n_ctx, ev_w_in[j], ev_w_out[j], s5_lam_re[j], s5_lam_im[j], s5_log_dt[j],
                              s5_b_re[j], s5_b_im[j], s5_c_re[j], s5_c_im[j], s5_d[j], s5_glu_w[j],
                              s5_glu_b[j], na_rpb[j])
        else:
            out = _odd_mixer(a, n_ctx, od_w_in[j], od_w_out[j], rk_mu[j], rk_g_up[j], rk_w0[j], rk_w_up[j],
                             rk_a0[j], rk_a_up[j], rk_k_k[j], rk_k_a[j], rk_r_k[j], rk_ln_w[j], rk_ln_b[j],
                             gq_q_norm[j], gq_k_norm[j])
        h = _gated_add(h, n_ctx, out, g1[bsz], g1[:bsz])
        f = _modulated(h, n_ctx, norm2[i], sc2[bsz], sh2[bsz], sc2[:bsz], sh2[:bsz])
        moe_args = (moe_w_router[i], moe_b_router[i], moe_w1[i], moe_b1[i], moe_w2[i], moe_b2[i])
        if last:
            y = moe_ffn(f[:, n_ctx:].reshape(bsz * seq, dm), *moe_args).reshape(bsz, seq, dm)
            hx = h[:, n_ctx:] + g2[:bsz][:, None, :] * y
            return _rms(hx, final_norm).astype(x.dtype)
        y = moe_ffn(f.reshape(bsz * t, dm), *moe_args).reshape(bsz, t, dm)
        h = _gated_add(h, n_ctx, y, g2[bsz], g2[:bsz])
```

```python
import functools
import math

import numpy as np
import jax
import jax.numpy as jnp
from jax import lax
from jax.experimental import pallas as pl
from jax.experimental.pallas import tpu as pltpu

F32 = jnp.float32
BF16 = jnp.bfloat16

V7X_LANES = 128
V7X_SUBLANES = 8
V7X_VMEM_BYTES = 64 * 1024 * 1024
VMEM_LIMIT = 56 * 1024 * 1024

GRID_W = 64
N_MOD = 6
NORM_EPS = 1e-6
S5_GROUP = 16
S5_STATE = 64
S5_CHUNK = 16
NA_HEADS = 8
WIN_ROWS = 8
WIN_COLS = 16
NA_QROWS = 4
NA_KROWS = NA_QROWS + WIN_ROWS - 1
RK_HEAD = 64
RK_DECAY_RANK = 64
RK_ICLR_RANK = 64
RK_GATE_RANK = 128
RK_GN_EPS = 64e-5
RK_CHUNK = 64
GQ_HEAD_DIM = 128
ROPE_THETA = 10000.0
GQ_TQ = 128
GQ_TK = 2048
N_EXPERTS = 32
TOP_K = 4
SWIGLU_ALPHA = 1.702
SWIGLU_LIMIT = 7.0
MOE_TM = 256
NEG_BIG = -1e30
HI = lax.Precision.HIGHEST


def _params(*sem):
    return pltpu.CompilerParams(dimension_semantics=sem, vmem_limit_bytes=VMEM_LIMIT)


def _tile(n, target, align):
    best = None
    t = align
    while t <= min(n, target):
        if n % t == 0:
            best = t
        t += align
    return n if best is None else best


def _mm_kernel(x_ref, w_ref, o_ref):
    o_ref[...] = jnp.dot(x_ref[...], w_ref[...], preferred_element_type=F32).astype(o_ref.dtype)


def matmul(x, w, out_dtype=F32):
    m, k = x.shape
    n = w.shape[1]
    x = x.astype(BF16)
    w = w.astype(BF16)
    m_pad = -(-m // 16) * 16
    if m_pad != m:
        x = jnp.pad(x, ((0, m_pad - m), (0, 0)))
    tm = _tile(m_pad, 512, 16)
    tn = _tile(n, 2048, V7X_LANES)
    out = pl.pallas_call(
        _mm_kernel,
        grid=(n // tn, m_pad // tm),
        in_specs=[pl.BlockSpec((tm, k), lambda j, i: (i, 0)),
                  pl.BlockSpec((k, tn), lambda j, i: (0, j))],
        out_specs=pl.BlockSpec((tm, tn), lambda j, i: (i, j)),
        out_shape=jax.ShapeDtypeStruct((m_pad, n), out_dtype),
        compiler_params=_params("parallel", "parallel"),
        name="matmul",
    )(x, w)
    return out[:m] if m_pad != m else out


def _flash_kernel(q_ref, k_ref, v_ref, o_ref, m_sc, l_sc, acc_sc, *, grp, hd, tq, tk, n_ctx,
                  n_ctx_qblocks, n_lat_chunks):
    qi = pl.program_id(2)
    q = jnp.concatenate([q_ref[0, :, g * hd:(g + 1) * hd] for g in range(grp)], axis=0)

    scale = hd ** -0.5
    s = lax.dot_general(q, k_ref[0, 0:n_ctx, :], (((1,), (1,)), ((), ())),
                        preferred_element_type=F32) * scale
    m0 = jnp.max(s, axis=-1, keepdims=True)
    p = jnp.exp(s - m0)
    m_sc[...] = m0
    l_sc[...] = jnp.sum(p, axis=-1, keepdims=True)
    acc_sc[...] = jnp.dot(p.astype(BF16), v_ref[0, 0:n_ctx, :], preferred_element_type=F32)

    def body(j, carry):
        off = pl.multiple_of(n_ctx + j * tk, V7X_LANES)
        k = k_ref[0, pl.ds(off, tk), :]
        v = v_ref[0, pl.ds(off, tk), :]
        s = lax.dot_general(q, k, (((1,), (1,)), ((), ())), preferred_element_type=F32) * scale
        m_prev = m_sc[...]
        m_new = jnp.maximum(m_prev, jnp.max(s, axis=-1, keepdims=True))
        alpha = jnp.exp(m_prev - m_new)
        p = jnp.exp(s - m_new)
        l_sc[...] = alpha * l_sc[...] + jnp.sum(p, axis=-1, keepdims=True)
        acc_sc[...] = alpha * acc_sc[...] + jnp.dot(p.astype(BF16), v, preferred_element_type=F32)
        m_sc[...] = m_new
        return carry

    lax.fori_loop(0, jnp.where(qi < n_ctx_qblocks, 0, n_lat_chunks), body, 0)
    o = acc_sc[...] / l_sc[...]
    for g in range(grp):
        o_ref[0, :, g * hd:(g + 1) * hd] = o[g * tq:(g + 1) * tq].astype(o_ref.dtype)


def gqa_attention(q, k, v, n_ctx, n_kv_heads):
    b, t, qw = q.shape
    hd = GQ_HEAD_DIM
    grp = qw // hd // n_kv_heads
    tq = GQ_TQ
    tk = _tile(t - n_ctx, GQ_TK, V7X_LANES)
    assert n_ctx % tq == 0 and t % tq == 0 and n_ctx % V7X_LANES == 0
    kern = functools.partial(_flash_kernel, grp=grp, hd=hd, tq=tq, tk=tk, n_ctx=n_ctx,
                             n_ctx_qblocks=n_ctx // tq, n_lat_chunks=(t - n_ctx) // tk)
    rows = grp * tq
    return pl.pallas_call(
        kern,
        grid=(b, n_kv_heads, t // tq),
        in_specs=[pl.BlockSpec((1, tq, grp * hd), lambda bi, h, i: (bi, i, h)),
                  pl.BlockSpec((1, t, hd), lambda bi, h, i: (bi, 0, h)),
                  pl.BlockSpec((1, t, hd), lambda bi, h, i: (bi, 0, h))],
        out_specs=pl.BlockSpec((1, tq, grp * hd), lambda bi, h, i: (bi, i, h)),
        out_shape=jax.ShapeDtypeStruct((b, t, qw), BF16),
        scratch_shapes=[pltpu.VMEM((rows, 1), F32), pltpu.VMEM((rows, 1), F32),
                        pltpu.VMEM((rows, hd), F32)],
        compiler_params=_params("parallel", "parallel", "parallel"),
        name="gqa_attention",
    )(q, k, v)


def _na_kernel(q_ref, k_ref, v_ref, bias_ref, o_ref, *, n_ctx, rows, n_blocks):
    j = pl.program_id(2)
    q = q_ref[0]
    kc = k_ref[0, 0:n_ctx, :]
    vc = v_ref[0, 0:n_ctx, :]
    dn = (((1,), (1,)), ((), ()))
    scale = q.shape[-1] ** -0.5
    s_ctx = lax.dot_general(q, kc, dn, preferred_element_type=F32) * scale

    @pl.when(j == 0)
    def _():
        m = jnp.max(s_ctx, axis=-1, keepdims=True)
        p = jnp.exp(s_ctx - m)
        l = jnp.sum(p, axis=-1, keepdims=True)
        o = jnp.dot(p.astype(BF16), vc, preferred_element_type=F32) / l
        o_ref[0] = o.astype(o_ref.dtype)

    @pl.when(j > 0)
    def _():
        jj = j - 1
        ks = jnp.clip(NA_QROWS * jj - WIN_ROWS // 2, 0, rows - NA_KROWS)
        off = pl.multiple_of(n_ctx + ks * GRID_W, GRID_W)
        kw = k_ref[0, pl.ds(off, NA_KROWS * GRID_W), :]
        vw = v_ref[0, pl.ds(off, NA_KROWS * GRID_W), :]
        pat = jnp.where(jj == 0, 0, jnp.where(jj == n_blocks - 1, 2, 1))
        s_win = lax.dot_general(q, kw, dn, preferred_element_type=F32) * scale + bias_ref[pat]
        m = jnp.maximum(jnp.max(s_win, axis=-1, keepdims=True), jnp.max(s_ctx, axis=-1, keepdims=True))
        p_win = jnp.exp(s_win - m)
        p_ctx = jnp.exp(s_ctx - m)
        l = jnp.sum(p_win, axis=-1, keepdims=True) + jnp.sum(p_ctx, axis=-1, keepdims=True)
        o = (jnp.dot(p_win.astype(BF16), vw, preferred_element_type=F32)
             + jnp.dot(p_ctx.astype(BF16), vc, preferred_element_type=F32)) / l
        o_ref[0] = o.astype(o_ref.dtype)


def _na_bias(rpb, rows):
    n_blocks = rows // NA_QROWS
    col = jnp.arange(GRID_W)
    col_start = jnp.clip(col - WIN_COLS // 2, 0, GRID_W - WIN_COLS)
    col_ok = (col[None, :] >= col_start[:, None]) & (col[None, :] < col_start[:, None] + WIN_COLS)
    dc = jnp.clip(col[None, :] - col[:, None] + (WIN_COLS - 1), 0, 2 * WIN_COLS - 2)
    pats = []
    for jj in (0, 1, n_blocks - 1):
        ks = min(max(NA_QROWS * jj - WIN_ROWS // 2, 0), rows - NA_KROWS)
        r = NA_QROWS * jj + jnp.arange(NA_QROWS)
        kr = ks + jnp.arange(NA_KROWS)
        r_start = jnp.clip(r - WIN_ROWS // 2, 0, rows - WIN_ROWS)
        row_ok = (kr[None, :] >= r_start[:, None]) & (kr[None, :] < r_start[:, None] + WIN_ROWS)
        dr = jnp.clip(kr[None, :] - r[:, None] + (WIN_ROWS - 1), 0, 2 * WIN_ROWS - 2)
        bias = rpb.astype(F32)[:, dr[:, None, :, None], dc[None, :, None, :]]
        ok = row_ok[:, None, :, None] & col_ok[None, :, None, :]
        bias = jnp.where(ok[None], bias, NEG_BIG)
        pats.append(bias.reshape(rpb.shape[0], NA_QROWS * GRID_W, NA_KROWS * GRID_W))
    return jnp.stack(pats, axis=0)


def neighborhood_attention(qkv, rpb, n_ctx):
    b, t, width3 = qkv.shape
    width = width3 // 3
    hd = width // NA_HEADS
    seq = t - n_ctx
    rows = seq // GRID_W
    qb = NA_QROWS * GRID_W
    assert n_ctx == qb and rows % NA_QROWS == 0 and rows >= NA_KROWS and WIN_ROWS <= rows
    n_blocks = rows // NA_QROWS
    bias = _na_bias(rpb, rows)
    kern = functools.partial(_na_kernel, n_ctx=n_ctx, rows=rows, n_blocks=n_blocks)
    kb = NA_KROWS * GRID_W
    return pl.pallas_call(
        kern,
        grid=(NA_HEADS, b, n_blocks + 1),
        in_specs=[pl.BlockSpec((1, qb, hd), lambda h, bi, j: (bi, j, h)),
                  pl.BlockSpec((1, t, hd), lambda h, bi, j: (bi, 0, NA_HEADS + h)),
                  pl.BlockSpec((1, t, hd), lambda h, bi, j: (bi, 0, 2 * NA_HEADS + h)),
                  pl.BlockSpec((3, None, qb, kb), lambda h, bi, j: (0, h, 0, 0))],
        out_specs=pl.BlockSpec((1, qb, hd), lambda h, bi, j: (bi, j, h)),
        out_shape=jax.ShapeDtypeStruct((b, t, width), BF16),
        compiler_params=_params("parallel", "parallel", "parallel"),
        name="neighborhood_attention",
    )(qkv, qkv, qkv, bias)


def _s5_kernel(u_ref, t_ref, n_ref, m_ref, d_ref, y_ref, hl_sc, hp_sc, *, nc_ctx, nc_all, bsz):
    u = u_ref[0]
    for part in range(4):
        hl_sc[part] = jnp.dot(u, n_ref[0, part], preferred_element_type=F32)
    zero = jnp.zeros((bsz, S5_STATE), F32)

    def scan(direction):
        dr = d_ref[0, 2 * direction:2 * direction + 1, :]
        di = d_ref[0, 2 * direction + 1:2 * direction + 2, :]

        def body(step, carry):
            hr, hi = carry
            if direction == 0:
                c = step
            else:
                c = jnp.where(step < nc_ctx, nc_ctx - 1 - step, nc_all - 1 - (step - nc_ctx))
            r0 = pl.multiple_of(c * bsz, bsz)
            hp_sc[2 * direction, pl.ds(r0, bsz), :] = hr
            hp_sc[2 * direction + 1, pl.ds(r0, bsz), :] = hi
            lr = hl_sc[2 * direction, pl.ds(r0, bsz), :]
            li = hl_sc[2 * direction + 1, pl.ds(r0, bsz), :]
            return dr * hr - di * hi + lr, dr * hi + di * hr + li

        lax.fori_loop(0, nc_all, body, (zero, zero))

    scan(0)
    scan(1)
    y = jnp.dot(u, t_ref[0], preferred_element_type=F32)
    for part in range(4):
        y = y + jnp.dot(hp_sc[part].astype(BF16), m_ref[0, part], preferred_element_type=F32)
    y_ref[0] = y


def _s5_tables(lam_re, lam_im, log_dt, b_re, b_im, c_re, c_im):
    c = S5_CHUNK
    g, p = lam_re.shape[1], lam_re.shape[2]
    h = b_re.shape[-1]
    toe = 0.0
    n_parts, m_parts, d_parts = [], [], []
    tt = jnp.arange(c)
    for d in range(2):
        lr, li = lam_re[d].astype(F32), lam_im[d].astype(F32)
        dt = jnp.exp(log_dt[d].astype(F32))[:, None]

        def lam_pow(e):
            e = jnp.asarray(e, F32)[..., None, None]
            mag = jnp.exp(e * lr * dt)
            return mag * jnp.cos(e * li * dt), mag * jnp.sin(e * li * dt)

        lbr, lbi = lam_pow(1)
        nr, ni = lbr - 1.0, lbi
        den = lr * lr + li * li
        fr, fi = (nr * lr + ni * li) / den, (ni * lr - nr * li) / den
        br, bi = b_re[d].astype(F32), b_im[d].astype(F32)
        bbr = fr[..., None] * br - fi[..., None] * bi
        bbi = fr[..., None] * bi + fi[..., None] * br
        cr, ci = c_re[d].astype(F32), c_im[d].astype(F32)
        pr, pi = lam_pow(tt)
        lbbr = pr[..., None] * bbr - pi[..., None] * bbi
        lbbi = pr[..., None] * bbi + pi[..., None] * bbr
        kern = (jnp.einsum('gop,tgpi->tgoi', cr, lbbr, precision=HI)
                - jnp.einsum('gop,tgpi->tgoi', ci, lbbi, precision=HI))
        lag = (tt[None, :] - tt[:, None]) if d == 0 else (tt[:, None] - tt[None, :])
        ok = lag >= 0
        kt = kern[jnp.clip(lag, 0, c - 1)]
        kt = jnp.where(ok[:, :, None, None, None], kt, 0.0)
        toe = toe + jnp.transpose(kt, (2, 0, 4, 1, 3)).reshape(g, c * h, c * h)
        e_in = (c - 1 - tt) if d == 0 else tt
        qr, qi = lam_pow(e_in)
        n_r = qr[..., None] * bbr - qi[..., None] * bbi
        n_i = qr[..., None] * bbi + qi[..., None] * bbr
        n_parts += [jnp.transpose(n_r, (1, 0, 3, 2)).reshape(g, c * h, p),
                    jnp.transpose(n_i, (1, 0, 3, 2)).reshape(g, c * h, p)]
        e_out = (tt + 1) if d == 0 else (c - tt)
        sr, si = lam_pow(e_out)
        clr = cr[None] * sr[:, :, None, :] - ci[None] * si[:, :, None, :]
        cli = cr[None] * si[:, :, None, :] + ci[None] * sr[:, :, None, :]
        m_parts += [jnp.transpose(clr, (1, 3, 0, 2)).reshape(g, p, c * h),
                    -jnp.transpose(cli, (1, 3, 0, 2)).reshape(g, p, c * h)]
        dcr, dci = lam_pow(c)
        d_parts += [dcr, dci]
    return (toe.astype(BF16), jnp.stack(n_parts, 1).astype(BF16), jnp.stack(m_parts, 1).astype(BF16),
            jnp.stack(d_parts, 1))


def s5_scan(u, n_ctx, lam_re, lam_im, log_dt, b_re, b_im, c_re, c_im):
    bsz, t, width = u.shape
    g = width // S5_GROUP
    c = S5_CHUNK
    assert bsz == V7X_SUBLANES and t % c == 0 and n_ctx % c == 0
    nc = t // c
    rows = nc * bsz
    toe, n_tab, m_tab, d_tab = _s5_tables(lam_re, lam_im, log_dt, b_re, b_im, c_re, c_im)
    uc = u.astype(BF16).reshape(bsz, nc, c, g, S5_GROUP)
    uc = jnp.transpose(uc, (3, 1, 0, 2, 4)).reshape(g, rows, c * S5_GROUP)
    kern = functools.partial(_s5_kernel, nc_ctx=n_ctx // c, nc_all=nc, bsz=bsz)
    cw = c * S5_GROUP
    y = pl.pallas_call(
        kern,
        grid=(g,),
        in_specs=[pl.BlockSpec((1, rows, cw), lambda i: (i, 0, 0)),
                  pl.BlockSpec((1, cw, cw), lambda i: (i, 0, 0)),
                  pl.BlockSpec((1, 4, cw, S5_STATE), lambda i: (i, 0, 0, 0)),
                  pl.BlockSpec((1, 4, S5_STATE, cw), lambda i: (i, 0, 0, 0)),
                  pl.BlockSpec((1, 4, S5_STATE), lambda i: (i, 0, 0))],
        out_specs=pl.BlockSpec((1, rows, cw), lambda i: (i, 0, 0)),
        out_shape=jax.ShapeDtypeStruct((g, rows, cw), F32),
        scratch_shapes=[pltpu.VMEM((4, rows, S5_STATE), F32), pltpu.VMEM((4, rows, S5_STATE), F32)],
        compiler_params=_params("parallel"),
        name="s5_scan",
    )(uc, toe, n_tab, m_tab, d_tab)
    y = y.reshape(g, nc, bsz, c, S5_GROUP)
    return jnp.transpose(y, (2, 1, 3, 0, 4)).reshape(bsz, t, width)


def _rwkv_kernel(at_ref, rt_ref, bt_ref, kt_ref, v_ref, pc_ref, y_ref, s_sc, *, heads, c):
    n = pl.program_id(2)

    @pl.when(n == 0)
    def _():
        s_sc[...] = jnp.zeros(s_sc.shape, F32)

    row = lax.broadcasted_iota(jnp.int32, (c, c), 0)
    col = lax.broadcasted_iota(jnp.int32, (c, c), 1)
    strict = row > col
    incl = row >= col
    nt = (((1,), (1,)), ((), ()))
    tn = (((0,), (0,)), ((), ()))
    n_factors = int(math.log2(c))
    hs = range(heads)
    sls = [slice(j * RK_HEAD, (j + 1) * RK_HEAD) for j in hs]

    def dot(a, b, dims=None):
        a, b = a.astype(BF16), b.astype(BF16)
        if dims is None:
            return jnp.dot(a, b, preferred_element_type=F32)
        return lax.dot_general(a, b, dims, preferred_element_type=F32)

    vv = [v_ref[0, 0, :, sl] for sl in sls]
    s0 = [s_sc[j] for j in hs]
    ar = [jnp.concatenate([at_ref[0, 0, :, sl], rt_ref[0, 0, :, sl]], axis=0) for sl in sls]
    bk = [jnp.concatenate([bt_ref[0, 0, :, sl], kt_ref[0, 0, :, sl]], axis=0) for sl in sls]
    g = [dot(ar[j], bk[j], nt) for j in hs]
    ah = [dot(ar[j], s0[j], nt) for j in hs]
    x = [jnp.where(strict, g[j][:c, :c], 0.0) for j in hs]
    a_ak = [jnp.where(strict, g[j][:c, c:], 0.0) for j in hs]
    m = [jnp.concatenate([jnp.where(incl, g[j][c:, :c], 0.0), jnp.where(incl, g[j][c:, c:], 0.0)], axis=1)
         for j in hs]
    u = [ah[j][:c] + dot(a_ak[j], vv[j]) for j in hs]
    u = [u[j] + dot(x[j], u[j]) for j in hs]
    for _ in range(n_factors - 1):
        x = [dot(x[j], x[j]) for j in hs]
        u = [u[j] + dot(x[j], u[j]) for j in hs]
    uv = [jnp.concatenate([u[j].astype(BF16), vv[j]], axis=0) for j in hs]
    for j in hs:
        y_ref[0, 0, :, sls[j]] = ah[j][c:] + dot(m[j], uv[j])
    for j in hs:
        s_new = s0[j] + dot(uv[j], bk[j], tn)
        s_sc[j] = s_new * pc_ref[0, 0, 0, :, sls[j]]


def rwkv7_scan(at, rt, bt, kt, v, pc):
    _, bsz, t, width = at.shape
    heads = width // RK_HEAD
    c = RK_CHUNK
    assert t % c == 0
    stream = pl.BlockSpec((1, 1, c, width), lambda d, bi, n: (d, bi, n, 0))
    kern = functools.partial(_rwkv_kernel, heads=heads, c=c)
    return pl.pallas_call(
        kern,
        grid=(2, bsz, t // c),
        in_specs=[stream, stream, stream, stream, stream,
                  pl.BlockSpec((1, 1, 1, 1, width), lambda d, bi, n: (d, bi, n, 0, 0))],
        out_specs=stream,
        out_shape=jax.ShapeDtypeStruct((2, bsz, t, width), F32),
        scratch_shapes=[pltpu.VMEM((heads, RK_HEAD, RK_HEAD), F32)],
        compiler_params=_params("parallel", "parallel", "arbitrary"),
        name="rwkv7_scan",
    )(at, rt, bt, kt, v, pc)


def _deinterleave_kernel(w_ref, p_ref, o_ref):
    o_ref[...] = jnp.dot(w_ref[...].astype(BF16), p_ref[...], preferred_element_type=F32).astype(o_ref.dtype)


def deinterleave_columns(w):
    r, n2 = w.shape
    perm = jnp.concatenate([jnp.arange(0, n2, 2), jnp.arange(1, n2, 2)])
    p = (jnp.arange(n2)[:, None] == perm[None, :]).astype(BF16)
    tm = _tile(r, 1024, 16)
    return pl.pallas_call(
        _deinterleave_kernel,
        grid=(r // tm,),
        in_specs=[pl.BlockSpec((tm, n2), lambda i: (i, 0)), pl.BlockSpec((n2, n2), lambda i: (0, 0))],
        out_specs=pl.BlockSpec((tm, n2), lambda i: (i, 0)),
        out_shape=jax.ShapeDtypeStruct((r, n2), BF16),
        compiler_params=_params("parallel"),
        name="deinterleave_columns",
    )(w, p)


def _moe_kernel(be_ref, nb_ref, x_ref, g_ref, w1_ref, b1g_ref, b1l_ref, w2_ref, b2_ref, o_ref):
    i = pl.program_id(0)
    de = w2_ref.shape[1]

    @pl.when(i < nb_ref[0])
    def _():
        x = x_ref[...]
        z = jnp.dot(x, w1_ref[0], preferred_element_type=F32)
        zg = z[:, :de] + b1g_ref[0]
        zl = z[:, de:] + b1l_ref[0]
        zg = jnp.minimum(zg, SWIGLU_LIMIT)
        zl = jnp.clip(zl, -SWIGLU_LIMIT, SWIGLU_LIMIT)
        act = zg * jax.nn.sigmoid(SWIGLU_ALPHA * zg) * (zl + 1.0)
        y = jnp.dot(act.astype(BF16), w2_ref[0], preferred_element_type=F32) + b2_ref[0]
        o_ref[...] = y * g_ref[...]

    @pl.when(i >= nb_ref[0])
    def _():
        o_ref[...] = jnp.zeros(o_ref.shape, F32)


def moe_ffn(h, w_router, b_router, w1, b1, w2, b2):
    n, dm = h.shape
    n_exp = w_router.shape[1]
    de = w2.shape[1]
    tm = MOE_TM
    logits = jnp.dot(h, w_router, precision=HI, preferred_element_type=F32) + b_router.astype(F32)
    top_val, top_idx = lax.top_k(logits, TOP_K)
    gates = jax.nn.softmax(top_val, axis=-1)
    flat_e = top_idx.reshape(-1).astype(jnp.int32)
    onehot = (flat_e[:, None] == jnp.arange(n_exp, dtype=jnp.int32)[None, :]).astype(jnp.int32)
    csum = jnp.cumsum(onehot, axis=0)
    rank = jnp.take_along_axis(csum, flat_e[:, None], axis=1)[:, 0] - 1
    counts = csum[-1]
    padded = (counts + tm - 1) // tm * tm
    pad_end = jnp.cumsum(padded)
    pad_start = pad_end - padded
    dest = pad_start[flat_e] + rank
    n_blocks = -(-(n * TOP_K) // tm) + n_exp
    cap = n_blocks * tm
    slot_src = jnp.full((cap,), -1, jnp.int32).at[dest].set(jnp.arange(n * TOP_K, dtype=jnp.int32))
    filled = slot_src >= 0
    slot_tok = jnp.where(filled, slot_src // TOP_K, n)
    slot_gate = jnp.where(filled, gates.reshape(-1)[jnp.maximum(slot_src, 0)], 0.0)
    block_exp = jnp.minimum(jnp.searchsorted(pad_end, jnp.arange(n_blocks) * tm, side='right'),
                            n_exp - 1).astype(jnp.int32)
    n_used = (pad_end[-1] // tm).astype(jnp.int32).reshape(1)
    h_pad = jnp.concatenate([h.astype(BF16), jnp.zeros((1, dm), BF16)], axis=0)
    x_sorted = h_pad[slot_tok]

    w1p = deinterleave_columns(w1.reshape(n_exp * dm, 2 * de)).reshape(n_exp, dm, 2 * de)
    b1g = b1[:, None, 0::2].astype(F32)
    b1l = b1[:, None, 1::2].astype(F32)
    w2b = w2.astype(BF16)
    b2r = b2[:, None, :].astype(F32)

    grid_spec = pltpu.PrefetchScalarGridSpec(
        num_scalar_prefetch=2,
        grid=(n_blocks,),
        in_specs=[pl.BlockSpec((tm, dm), lambda i, be, nb: (i, 0)),
                  pl.BlockSpec((tm, 1), lambda i, be, nb: (i, 0)),
                  pl.BlockSpec((1, dm, 2 * de), lambda i, be, nb: (be[i], 0, 0)),
                  pl.BlockSpec((1, 1, de), lambda i, be, nb: (be[i], 0, 0)),
                  pl.BlockSpec((1, 1, de), lambda i, be, nb: (be[i], 0, 0)),
                  pl.BlockSpec((1, de, dm), lambda i, be, nb: (be[i], 0, 0)),
                  pl.BlockSpec((1, 1, dm), lambda i, be, nb: (be[i], 0, 0))],
        out_specs=pl.BlockSpec((tm, dm), lambda i, be, nb: (i, 0)),
    )
    yb = pl.pallas_call(
        _moe_kernel,
        grid_spec=grid_spec,
        out_shape=jax.ShapeDtypeStruct((cap, dm), F32),
        compiler_params=_params("arbitrary"),
        name="moe_experts",
    )(block_exp, n_used, x_sorted, slot_gate[:, None], w1p, b1g, b1l, w2b, b2r)
    return yb[dest].reshape(n, TOP_K, dm).sum(axis=1)


def _rms(x, g, eps=NORM_EPS):
    xf = x.astype(F32)
    return xf * lax.rsqrt(jnp.mean(xf * xf, axis=-1, keepdims=True) + eps) * g.astype(F32)


def _modulated(h, n_ctx, norm_w, scale_c, shift_c, scale_x, shift_x):
    hn = _rms(h, norm_w)
    ac = hn[:, :n_ctx] * (1.0 + scale_c) + shift_c
    ax = hn[:, n_ctx:] * (1.0 + scale_x[:, None, :]) + shift_x[:, None, :]
    return jnp.concatenate([ac, ax], axis=1)


def _gated_add(h, n_ctx, out, gate_c, gate_x):
    return jnp.concatenate([h[:, :n_ctx] + gate_c * out[:, :n_ctx],
                            h[:, n_ctx:] + gate_x[:, None, :] * out[:, n_ctx:]], axis=1)


def _even_mixer(a, n_ctx, w_in, w_out, lam_re, lam_im, log_dt, b_re, b_im, c_re, c_im, d_skip,
                glu_w, glu_b, rpb):
    bsz, t, dm = a.shape
    mix_w = dm // 2
    a2 = a.reshape(bsz * t, dm).astype(BF16)
    u = matmul(a2, w_in[:, :mix_w]).reshape(bsz, t, mix_w)
    qkv = matmul(a2, w_in[:, mix_w:], out_dtype=BF16).reshape(bsz, t, 3 * mix_w)
    y = d_skip.astype(F32) * u + s5_scan(u, n_ctx, lam_re, lam_im, log_dt, b_re, b_im, c_re, c_im)
    gl = jax.nn.gelu(y)
    ya = gl * jax.nn.sigmoid(matmul(gl.reshape(bsz * t, mix_w), glu_w).reshape(bsz, t, mix_w)
                             + glu_b.astype(F32))
    yb = neighborhood_attention(qkv, rpb, n_ctx)
    cat = jnp.concatenate([ya.astype(BF16), yb], axis=-1)
    return matmul(cat.reshape(bsz * t, dm), w_out).reshape(bsz, t, dm)


def _token_shift(z, mu):
    zp = jnp.pad(z, ((0, 0), (1, 0), (0, 0)))[:, :-1]
    zn = jnp.pad(z, ((0, 0), (0, 1), (0, 0)))[:, 1:]
    return z + mu[0] * (zp - z) + mu[1] * (zn - z)


def _rope_2d(tq, row_pos, col_pos):
    e = tq.shape[-1]
    half = e // 2
    inv = ROPE_THETA ** (-jnp.arange(0, half, 2, dtype=F32) / half)

    def rot(u, pos):
        ang = pos.astype(F32)[:, None] * inv[None, :]
        cos = jnp.cos(ang)[None, :, None, :]
        sin = jnp.sin(ang)[None, :, None, :]
        u1, u2 = u[..., :half // 2], u[..., half // 2:]
        return jnp.concatenate([u1 * cos - u2 * sin, u2 * cos + u1 * sin], axis=-1)

    return jnp.concatenate([rot(tq[..., :half], row_pos), rot(tq[..., half:], col_pos)], axis=-1)


def _odd_mixer(a, n_ctx, w_in, w_out, mu, g_up, w0, w_up, a0, a_up, k_k, k_a, r_k, ln_w, ln_b,
               q_norm, k_norm):
    bsz, t, dm = a.shape
    mix_w = dm // 2
    heads = mix_w // RK_HEAD
    rk_in = 3 * mix_w + RK_GATE_RANK + 2 * RK_DECAY_RANK + 2 * RK_ICLR_RANK
    gq_heads = mix_w // GQ_HEAD_DIM
    kv_heads = gq_heads // 4
    kv_w = kv_heads * GQ_HEAD_DIM
    a2 = a.reshape(bsz * t, dm).astype(BF16)
    zr = matmul(a2, w_in[:, :rk_in]).reshape(bsz, t, rk_in)
    zqk = matmul(a2, w_in[:, rk_in:rk_in + mix_w + kv_w]).reshape(bsz, t, mix_w + kv_w)
    vv = matmul(a2, w_in[:, rk_in + mix_w + kv_w:], out_dtype=BF16).reshape(bsz, t, kv_w)

    muf = mu.astype(F32)
    zs = jnp.concatenate([_token_shift(zr[:, :n_ctx], muf), _token_shift(zr[:, n_ctx:], muf)], axis=1)
    r = zs[..., :mix_w]
    kk0 = zs[..., mix_w:2 * mix_w]
    v = zs[..., 2 * mix_w:3 * mix_w]
    o = 3 * mix_w
    g_lo = zs[..., o:o + RK_GATE_RANK]
    w_lo = zs[..., o + RK_GATE_RANK:o + RK_GATE_RANK + 2 * RK_DECAY_RANK]
    a_lo = zs[..., o + RK_GATE_RANK + 2 * RK_DECAY_RANK:]
    m = bsz * t

    def hshape(x):
        return x.reshape(bsz, t, heads, RK_HEAD)

    def flipseg(x):
        return jnp.concatenate([jnp.flip(x[:, :n_ctx], axis=1), jnp.flip(x[:, n_ctx:], axis=1)], axis=1)

    def chunked(x):
        return x.reshape(bsz, t // RK_CHUNK, RK_CHUNK, mix_w)

    gate = matmul(jax.nn.sigmoid(g_lo).reshape(m, RK_GATE_RANK), g_up).reshape(bsz, t, mix_w)
    k_s, ats, rts, bts, kts, vs, pcs = [], [], [], [], [], [], []
    for d in range(2):
        wl = jnp.tanh(w_lo[..., d * RK_DECAY_RANK:(d + 1) * RK_DECAY_RANK]).reshape(m, RK_DECAY_RANK)
        w_log = -jax.nn.softplus(-(w0[d] + matmul(wl, w_up[d]).reshape(bsz, t, mix_w))) - 0.5
        log_decay = -jnp.exp(w_log)
        al = a_lo[..., d * RK_ICLR_RANK:(d + 1) * RK_ICLR_RANK].reshape(m, RK_ICLR_RANK)
        iclr = jax.nn.sigmoid(a0[d] + matmul(al, a_up[d]).reshape(bsz, t, mix_w))
        kk = hshape(kk0 * k_k[d])
        kk = kk / jnp.maximum(jnp.sqrt(jnp.sum(kk * kk, axis=-1, keepdims=True)), 1e-12)
        kk = kk.reshape(bsz, t, mix_w)
        k_d = kk0 * (1.0 + (iclr - 1.0) * k_a[d])
        k_s.append(k_d)
        order = flipseg if d == 1 else (lambda x: x)
        lw = chunked(order(log_decay))
        cum = jnp.cumsum(lw, axis=2)
        p_inv = jnp.exp(-cum)
        ats.append((chunked(order(-kk)) * jnp.exp(cum - lw)).astype(BF16))
        rts.append((chunked(order(r)) * jnp.exp(cum)).astype(BF16))
        bts.append((chunked(order(kk * iclr)) * p_inv).astype(BF16))
        kts.append((chunked(order(k_d)) * p_inv).astype(BF16))
        vs.append(order(v).astype(BF16))
        pcs.append(jnp.exp(cum[:, :, -1:, :]))
    stack = lambda xs: jnp.stack(xs, axis=0).reshape(2, bsz, t, mix_w)
    ys = rwkv7_scan(stack(ats), stack(rts), stack(bts), stack(kts), stack(vs), jnp.stack(pcs, axis=0))
    ysum = hshape(ys[0] + flipseg(ys[1]))
    mean = jnp.mean(ysum, axis=-1, keepdims=True)
    var = jnp.mean(jnp.square(ysum - mean), axis=-1, keepdims=True)
    yn = ((ysum - mean) * lax.rsqrt(var + RK_GN_EPS)).reshape(bsz, t, mix_w)
    yn = yn * ln_w.astype(F32) + ln_b.astype(F32)
    rkf = r_k.astype(F32)
    bonus = 0.0
    for d in range(2):
        bonus = bonus + jnp.sum(hshape(r) * hshape(k_s[d]) * rkf, axis=-1, keepdims=True) * hshape(v)
    yc = (yn + bonus.reshape(bsz, t, mix_w)) * gate

    q = zqk[..., :mix_w].reshape(bsz, t, gq_heads, GQ_HEAD_DIM)
    k = zqk[..., mix_w:].reshape(bsz, t, kv_heads, GQ_HEAD_DIM)
    qn = _rms(q, q_norm)
    kn = _rms(k, k_norm)
    pos = jnp.arange(t - n_ctx)
    qn = jnp.concatenate([qn[:, :n_ctx], _rope_2d(qn[:, n_ctx:], pos // GRID_W, pos % GRID_W)], axis=1)
    kn = jnp.concatenate([kn[:, :n_ctx], _rope_2d(kn[:, n_ctx:], pos // GRID_W, pos % GRID_W)], axis=1)
    qb = qn.reshape(bsz, t, mix_w).astype(BF16)
    kb = kn.reshape(bsz, t, kv_w).astype(BF16)
    yd = gqa_attention(qb, kb, vv, n_ctx, kv_heads)

    cat = jnp.concatenate([yc.astype(BF16), yd], axis=-1)
    return matmul(cat.reshape(bsz * t, dm), w_out).reshape(bsz, t, dm)


def kernel(x, c, ctx, c_ctx, w_mod, b_mod, norm1, norm2, final_norm, ev_w_in, ev_w_out, s5_lam_re, s5_lam_im, s5_log_dt, s5_b_re, s5_b_im, s5_c_re, s5_c_im, s5_d, s5_glu_w, s5_glu_b, na_rpb, od_w_in, od_w_out, rk_mu, rk_g_up, rk_w0, rk_w_up, rk_a0, rk_a_up, rk_k_k, rk_k_a, rk_r_k, rk_ln_w, rk_ln_b, gq_q_norm, gq_k_norm, moe_w_router, moe_b_router, moe_w1, moe_b1, moe_w2, moe_b2):
    bsz, seq, dm = x.shape
    n_ctx = ctx.shape[1]
    depth = w_mod.shape[0]
    t = n_ctx + seq
    h = jnp.concatenate([ctx, x], axis=1).astype(F32)
    cond = jnp.concatenate([jax.nn.silu(c), jax.nn.silu(c_ctx)[None, :]], axis=0)
    for i in range(depth):
        last = i == depth - 1
        j = i // 2
        mod = matmul(cond, w_mod[i]) + b_mod[i].astype(F32)
        sh1, sc1, g1, sh2, sc2, g2 = jnp.split(mod, N_MOD, axis=-1)
        a = _modulated(h, n_ctx, norm1[i], sc1[bsz], sh1[bsz], sc1[:bsz], sh1[:bsz])
        if i % 2 == 0:
            out = _even_mixer(a, n_ctx, ev_w_in[j], ev_w_out[j], s5_lam_re[j], s5_lam_im[j], s5_log_dt[j],
                              s5_b_re[j], s5_b_im[j], s5_c_re[j], s5_c_im[j], s5_d[j], s5_glu_w[j],
                              s5_glu_b[j], na_rpb[j])
        else:
            out = _odd_mixer(a, n_ctx, od_w_in[j], od_w_out[j], rk_mu[j], rk_g_up[j], rk_w0[j], rk_w_up[j],
                             rk_a0[j], rk_a_up[j], rk_k_k[j], rk_k_a[j], rk_r_k[j], rk_ln_w[j], rk_ln_b[j],
                             gq_q_norm[j], gq_k_norm[j])
        h = _gated_add(h, n_ctx, out, g1[bsz], g1[:bsz])
        f = _modulated(h, n_ctx, norm2[i], sc2[bsz], sh2[bsz], sc2[:bsz], sh2[:bsz])
        moe_args = (moe_w_router[i], moe_b_router[i], moe_w1[i], moe_b1[i], moe_w2[i], moe_b2[i])
        if last:
            y = moe_ffn(f[:, n_ctx:].reshape(bsz * seq, dm), *moe_args).reshape(bsz, seq, dm)
            hx = h[:, n_ctx:] + g2[:bsz][:, None, :] * y
            return _rms(hx, final_norm).astype(x.dtype)
        y = moe_ffn(f.reshape(bsz * t, dm), *moe_args).reshape(bsz, t, dm)
        h = _gated_add(h, n_ctx, y, g2[bsz], g2[:bsz])
```

```python
import functools
import math

import numpy as np
import jax
import jax.numpy as jnp
from jax import lax
from jax.experimental import pallas as pl
from jax.experimental.pallas import tpu as pltpu

F32 = jnp.float32
BF16 = jnp.bfloat16

V7X_LANES = 128
V7X_SUBLANES = 8
V7X_VMEM_BYTES = 64 * 1024 * 1024
VMEM_LIMIT = 56 * 1024 * 1024

GRID_W = 64
N_MOD = 6
NORM_EPS = 1e-6
S5_GROUP = 16
S5_STATE = 64
S5_CHUNK = 16
NA_HEADS = 8
WIN_ROWS = 8
WIN_COLS = 16
NA_QROWS = 4
NA_KROWS = NA_QROWS + WIN_ROWS - 1
RK_HEAD = 64
RK_DECAY_RANK = 64
RK_ICLR_RANK = 64
RK_GATE_RANK = 128
RK_GN_EPS = 64e-5
RK_CHUNK = 64
GQ_HEAD_DIM = 128
ROPE_THETA = 10000.0
GQ_TQ = 128
GQ_TK = 2048
N_EXPERTS = 32
TOP_K = 4
SWIGLU_ALPHA = 1.702
SWIGLU_LIMIT = 7.0
MOE_TM = 256
NEG_BIG = -1e30
HI = lax.Precision.HIGHEST


def _params(*sem):
    return pltpu.CompilerParams(dimension_semantics=sem, vmem_limit_bytes=VMEM_LIMIT)


def _tile(n, target, align):
    best = None
    t = align
    while t <= min(n, target):
        if n % t == 0:
            best = t
        t += align
    return n if best is None else best


def _mm_kernel(x_ref, w_ref, o_ref):
    o_ref[...] = jnp.dot(x_ref[...], w_ref[...], preferred_element_type=F32).astype(o_ref.dtype)


def matmul(x, w, out_dtype=F32):
    m, k = x.shape
    n = w.shape[1]
    x = x.astype(BF16)
    w = w.astype(BF16)
    m_pad = -(-m // 16) * 16
    if m_pad != m:
        x = jnp.pad(x, ((0, m_pad - m), (0, 0)))
    tm = _tile(m_pad, 512, 16)
    tn = _tile(n, 2048, V7X_LANES)
    out = pl.pallas_call(
        _mm_kernel,
        grid=(n // tn, m_pad // tm),
        in_specs=[pl.BlockSpec((tm, k), lambda j, i: (i, 0)),
                  pl.BlockSpec((k, tn), lambda j, i: (0, j))],
        out_specs=pl.BlockSpec((tm, tn), lambda j, i: (i, j)),
        out_shape=jax.ShapeDtypeStruct((m_pad, n), out_dtype),
        compiler_params=_params("parallel", "parallel"),
        name="matmul",
    )(x, w)
    return out[:m] if m_pad != m else out


def _out_proj_kernel(xa_ref, xb_ref, wa_ref, wb_ref, h_ref, g_ref, nw_ref, sc_ref, sh_ref, hn_ref, f_ref):
    out = (jnp.dot(xa_ref[...], wa_ref[...], preferred_element_type=F32)
           + jnp.dot(xb_ref[...], wb_ref[...], preferred_element_type=F32))
    hn = h_ref[...] + g_ref[0] * out
    hn_ref[...] = hn
    normed = hn * lax.rsqrt(jnp.mean(hn * hn, axis=-1, keepdims=True) + NORM_EPS) * nw_ref[...]
    f_ref[...] = (normed * (1.0 + sc_ref[0]) + sh_ref[0]).astype(f_ref.dtype)


def out_proj_residual_norm(xa, xb, w_out, h, n_ctx, gate, norm_w, scale, shift):
    bsz, t, dm = h.shape
    wa = xa.shape[-1]
    tm = _tile(n_ctx, 256, 16)
    assert t % tm == 0
    nbt, ncb = t // tm, n_ctx // tm

    def mod_row(i):
        return jnp.where(i % nbt < ncb, bsz, i // nbt)

    rows = pl.BlockSpec((tm, dm), lambda i: (i, 0))
    mod = pl.BlockSpec((1, 1, dm), lambda i: (mod_row(i), 0, 0))
    m = bsz * t
    w_out = w_out.astype(BF16)
    hn, f = pl.pallas_call(
        _out_proj_kernel,
        grid=(m // tm,),
        in_specs=[pl.BlockSpec((tm, wa), lambda i: (i, 0)),
                  pl.BlockSpec((tm, dm - wa), lambda i: (i, 0)),
                  pl.BlockSpec((wa, dm), lambda i: (0, 0)),
                  pl.BlockSpec((dm - wa, dm), lambda i: (0, 0)),
                  rows, mod, pl.BlockSpec((1, dm), lambda i: (0, 0)), mod, mod],
        out_specs=[rows, rows],
        out_shape=[jax.ShapeDtypeStruct((m, dm), F32), jax.ShapeDtypeStruct((m, dm), BF16)],
        compiler_params=_params("parallel"),
        name="out_proj_residual_norm",
    )(xa.reshape(m, wa), xb.reshape(m, dm - wa), w_out[:wa], w_out[wa:], h.reshape(m, dm),
      gate[:, None, :], norm_w.astype(F32)[None, :], scale[:, None, :], shift[:, None, :])
    return hn.reshape(bsz, t, dm), f.reshape(bsz, t, dm)


def _flash_kernel(q_ref, k_ref, v_ref, o_ref, m_sc, l_sc, acc_sc, *, grp, hd, tq, tk, n_ctx,
                  n_ctx_qblocks, n_lat_chunks):
    qi = pl.program_id(2)
    q = jnp.concatenate([q_ref[0, :, g * hd:(g + 1) * hd] for g in range(grp)], axis=0)

    scale = hd ** -0.5
    s = lax.dot_general(q, k_ref[0, 0:n_ctx, :], (((1,), (1,)), ((), ())),
                        preferred_element_type=F32) * scale
    m0 = jnp.max(s, axis=-1, keepdims=True)
    p = jnp.exp(s - m0)
    m_sc[...] = m0
    l_sc[...] = jnp.sum(p, axis=-1, keepdims=True)
    acc_sc[...] = jnp.dot(p.astype(BF16), v_ref[0, 0:n_ctx, :], preferred_element_type=F32)

    def body(j, carry):
        off = pl.multiple_of(n_ctx + j * tk, V7X_LANES)
        k = k_ref[0, pl.ds(off, tk), :]
        v = v_ref[0, pl.ds(off, tk), :]
        s = lax.dot_general(q, k, (((1,), (1,)), ((), ())), preferred_element_type=F32) * scale
        m_prev = m_sc[...]
        m_new = jnp.maximum(m_prev, jnp.max(s, axis=-1, keepdims=True))
        alpha = jnp.exp(m_prev - m_new)
        p = jnp.exp(s - m_new)
        l_sc[...] = alpha * l_sc[...] + jnp.sum(p, axis=-1, keepdims=True)
        acc_sc[...] = alpha * acc_sc[...] + jnp.dot(p.astype(BF16), v, preferred_element_type=F32)
        m_sc[...] = m_new
        return carry

    lax.fori_loop(0, jnp.where(qi < n_ctx_qblocks, 0, n_lat_chunks), body, 0)
    o = acc_sc[...] / l_sc[...]
    for g in range(grp):
        o_ref[0, :, g * hd:(g + 1) * hd] = o[g * tq:(g + 1) * tq].astype(o_ref.dtype)


def gqa_attention(q, k, v, n_ctx, n_kv_heads):
    b, t, qw = q.shape
    hd = GQ_HEAD_DIM
    grp = qw // hd // n_kv_heads
    tq = GQ_TQ
    tk = _tile(t - n_ctx, GQ_TK, V7X_LANES)
    assert n_ctx % tq == 0 and t % tq == 0 and n_ctx % V7X_LANES == 0
    kern = functools.partial(_flash_kernel, grp=grp, hd=hd, tq=tq, tk=tk, n_ctx=n_ctx,
                             n_ctx_qblocks=n_ctx // tq, n_lat_chunks=(t - n_ctx) // tk)
    rows = grp * tq
    return pl.pallas_call(
        kern,
        grid=(b, n_kv_heads, t // tq),
        in_specs=[pl.BlockSpec((1, tq, grp * hd), lambda bi, h, i: (bi, i, h)),
                  pl.BlockSpec((1, t, hd), lambda bi, h, i: (bi, 0, h)),
                  pl.BlockSpec((1, t, hd), lambda bi, h, i: (bi, 0, h))],
        out_specs=pl.BlockSpec((1, tq, grp * hd), lambda bi, h, i: (bi, i, h)),
        out_shape=jax.ShapeDtypeStruct((b, t, qw), BF16),
        scratch_shapes=[pltpu.VMEM((rows, 1), F32), pltpu.VMEM((rows, 1), F32),
                        pltpu.VMEM((rows, hd), F32)],
        compiler_params=_params("parallel", "parallel", "parallel"),
        name="gqa_attention",
    )(q, k, v)


def _na_kernel(q_ref, k_ref, v_ref, bias_ref, o_ref, *, n_ctx, rows, n_blocks):
    j = pl.program_id(2)
    q = q_ref[0]
    kc = k_ref[0, 0:n_ctx, :]
    vc = v_ref[0, 0:n_ctx, :]
    dn = (((1,), (1,)), ((), ()))
    scale = q.shape[-1] ** -0.5
    s_ctx = lax.dot_general(q, kc, dn, preferred_element_type=F32) * scale

    @pl.when(j == 0)
    def _():
        m = jnp.max(s_ctx, axis=-1, keepdims=True)
        p = jnp.exp(s_ctx - m)
        l = jnp.sum(p, axis=-1, keepdims=True)
        o = jnp.dot(p.astype(BF16), vc, preferred_element_type=F32) / l
        o_ref[0] = o.astype(o_ref.dtype)

    @pl.when(j > 0)
    def _():
        jj = j - 1
        ks = jnp.clip(NA_QROWS * jj - WIN_ROWS // 2, 0, rows - NA_KROWS)
        off = pl.multiple_of(n_ctx + ks * GRID_W, GRID_W)
        kw = k_ref[0, pl.ds(off, NA_KROWS * GRID_W), :]
        vw = v_ref[0, pl.ds(off, NA_KROWS * GRID_W), :]
        pat = jnp.where(jj == 0, 0, jnp.where(jj == n_blocks - 1, 2, 1))
        s_win = lax.dot_general(q, kw, dn, preferred_element_type=F32) * scale + bias_ref[pat]
        m = jnp.maximum(jnp.max(s_win, axis=-1, keepdims=True), jnp.max(s_ctx, axis=-1, keepdims=True))
        p_win = jnp.exp(s_win - m)
        p_ctx = jnp.exp(s_ctx - m)
        l = jnp.sum(p_win, axis=-1, keepdims=True) + jnp.sum(p_ctx, axis=-1, keepdims=True)
        o = (jnp.dot(p_win.astype(BF16), vw, preferred_element_type=F32)
             + jnp.dot(p_ctx.astype(BF16), vc, preferred_element_type=F32)) / l
        o_ref[0] = o.astype(o_ref.dtype)


def _na_bias(rpb, rows):
    n_blocks = rows // NA_QROWS
    col = jnp.arange(GRID_W)
    col_start = jnp.clip(col - WIN_COLS // 2, 0, GRID_W - WIN_COLS)
    col_ok = (col[None, :] >= col_start[:, None]) & (col[None, :] < col_start[:, None] + WIN_COLS)
    dc = jnp.clip(col[None, :] - col[:, None] + (WIN_COLS - 1), 0, 2 * WIN_COLS - 2)
    pats = []
    for jj in (0, 1, n_blocks - 1):
        ks = min(max(NA_QROWS * jj - WIN_ROWS // 2, 0), rows - NA_KROWS)
        r = NA_QROWS * jj + jnp.arange(NA_QROWS)
        kr = ks + jnp.arange(NA_KROWS)
        r_start = jnp.clip(r - WIN_ROWS // 2, 0, rows - WIN_ROWS)
        row_ok = (kr[None, :] >= r_start[:, None]) & (kr[None, :] < r_start[:, None] + WIN_ROWS)
        dr = jnp.clip(kr[None, :] - r[:, None] + (WIN_ROWS - 1), 0, 2 * WIN_ROWS - 2)
        bias = rpb.astype(F32)[:, dr[:, None, :, None], dc[None, :, None, :]]
        ok = row_ok[:, None, :, None] & col_ok[None, :, None, :]
        bias = jnp.where(ok[None], bias, NEG_BIG)
        pats.append(bias.reshape(rpb.shape[0], NA_QROWS * GRID_W, NA_KROWS * GRID_W))
    return jnp.stack(pats, axis=0)


def neighborhood_attention(qkv, rpb, n_ctx):
    b, t, width3 = qkv.shape
    width = width3 // 3
    hd = width // NA_HEADS
    seq = t - n_ctx
    rows = seq // GRID_W
    qb = NA_QROWS * GRID_W
    assert n_ctx == qb and rows % NA_QROWS == 0 and rows >= NA_KROWS and WIN_ROWS <= rows
    n_blocks = rows // NA_QROWS
    bias = _na_bias(rpb, rows)
    kern = functools.partial(_na_kernel, n_ctx=n_ctx, rows=rows, n_blocks=n_blocks)
    kb = NA_KROWS * GRID_W
    return pl.pallas_call(
        kern,
        grid=(NA_HEADS, b, n_blocks + 1),
        in_specs=[pl.BlockSpec((1, qb, hd), lambda h, bi, j: (bi, j, h)),
                  pl.BlockSpec((1, t, hd), lambda h, bi, j: (bi, 0, NA_HEADS + h)),
                  pl.BlockSpec((1, t, hd), lambda h, bi, j: (bi, 0, 2 * NA_HEADS + h)),
                  pl.BlockSpec((3, None, qb, kb), lambda h, bi, j: (0, h, 0, 0))],
        out_specs=pl.BlockSpec((1, qb, hd), lambda h, bi, j: (bi, j, h)),
        out_shape=jax.ShapeDtypeStruct((b, t, width), BF16),
        compiler_params=_params("parallel", "parallel", "parallel"),
        name="neighborhood_attention",
    )(qkv, qkv, qkv, bias)


def _s5_kernel(u_ref, t_ref, n_ref, m_ref, d_ref, y_ref, hl_sc, hp_sc, *, nc_ctx, nc_all, bsz):
    u = u_ref[0]
    for part in range(4):
        hl_sc[part] = jnp.dot(u, n_ref[0, part], preferred_element_type=F32)
    zero = jnp.zeros((bsz, S5_STATE), F32)

    def scan(direction):
        dr = d_ref[0, 2 * direction:2 * direction + 1, :]
        di = d_ref[0, 2 * direction + 1:2 * direction + 2, :]

        def body(step, carry):
            hr, hi = carry
            if direction == 0:
                c = step
            else:
                c = jnp.where(step < nc_ctx, nc_ctx - 1 - step, nc_all - 1 - (step - nc_ctx))
            r0 = pl.multiple_of(c * bsz, bsz)
            hp_sc[2 * direction, pl.ds(r0, bsz), :] = hr
            hp_sc[2 * direction + 1, pl.ds(r0, bsz), :] = hi
            lr = hl_sc[2 * direction, pl.ds(r0, bsz), :]
            li = hl_sc[2 * direction + 1, pl.ds(r0, bsz), :]
            return dr * hr - di * hi + lr, dr * hi + di * hr + li

        lax.fori_loop(0, nc_all, body, (zero, zero))

    scan(0)
    scan(1)
    y = jnp.dot(u, t_ref[0], preferred_element_type=F32)
    for part in range(4):
        y = y + jnp.dot(hp_sc[part].astype(BF16), m_ref[0, part], preferred_element_type=F32)
    y_ref[0] = y


def _s5_tables(lam_re, lam_im, log_dt, b_re, b_im, c_re, c_im):
    c = S5_CHUNK
    g, p = lam_re.shape[1], lam_re.shape[2]
    h = b_re.shape[-1]
    toe = 0.0
    n_parts, m_parts, d_parts = [], [], []
    tt = jnp.arange(c)
    for d in range(2):
        lr, li = lam_re[d].astype(F32), lam_im[d].astype(F32)
        dt = jnp.exp(log_dt[d].astype(F32))[:, None]

        def lam_pow(e):
            e = jnp.asarray(e, F32)[..., None, None]
            mag = jnp.exp(e * lr * dt)
            return mag * jnp.cos(e * li * dt), mag * jnp.sin(e * li * dt)

        lbr, lbi = lam_pow(1)
        nr, ni = lbr - 1.0, lbi
        den = lr * lr + li * li
        fr, fi = (nr * lr + ni * li) / den, (ni * lr - nr * li) / den
        br, bi = b_re[d].astype(F32), b_im[d].astype(F32)
        bbr = fr[..., None] * br - fi[..., None] * bi
        bbi = fr[..., None] * bi + fi[..., None] * br
        cr, ci = c_re[d].astype(F32), c_im[d].astype(F32)
        pr, pi = lam_pow(tt)
        lbbr = pr[..., None] * bbr - pi[..., None] * bbi
        lbbi = pr[..., None] * bbi + pi[..., None] * bbr
        kern = (jnp.einsum('gop,tgpi->tgoi', cr, lbbr, precision=HI)
                - jnp.einsum('gop,tgpi->tgoi', ci, lbbi, precision=HI))
        lag = (tt[None, :] - tt[:, None]) if d == 0 else (tt[:, None] - tt[None, :])
        ok = lag >= 0
        kt = kern[jnp.clip(lag, 0, c - 1)]
        kt = jnp.where(ok[:, :, None, None, None], kt, 0.0)
        toe = toe + jnp.transpose(kt, (2, 0, 4, 1, 3)).reshape(g, c * h, c * h)
        e_in = (c - 1 - tt) if d == 0 else tt
        qr, qi = lam_pow(e_in)
        n_r = qr[..., None] * bbr - qi[..., None] * bbi
        n_i = qr[..., None] * bbi + qi[..., None] * bbr
        n_parts += [jnp.transpose(n_r, (1, 0, 3, 2)).reshape(g, c * h, p),
                    jnp.transpose(n_i, (1, 0, 3, 2)).reshape(g, c * h, p)]
        e_out = (tt + 1) if d == 0 else (c - tt)
        sr, si = lam_pow(e_out)
        clr = cr[None] * sr[:, :, None, :] - ci[None] * si[:, :, None, :]
        cli = cr[None] * si[:, :, None, :] + ci[None] * sr[:, :, None, :]
        m_parts += [jnp.transpose(clr, (1, 3, 0, 2)).reshape(g, p, c * h),
                    -jnp.transpose(cli, (1, 3, 0, 2)).reshape(g, p, c * h)]
        dcr, dci = lam_pow(c)
        d_parts += [dcr, dci]
    return (toe.astype(BF16), jnp.stack(n_parts, 1).astype(BF16), jnp.stack(m_parts, 1).astype(BF16),
            jnp.stack(d_parts, 1))


def s5_scan(u, n_ctx, lam_re, lam_im, log_dt, b_re, b_im, c_re, c_im):
    bsz, t, width = u.shape
    g = width // S5_GROUP
    c = S5_CHUNK
    assert bsz == V7X_SUBLANES and t % c == 0 and n_ctx % c == 0
    nc = t // c
    rows = nc * bsz
    toe, n_tab, m_tab, d_tab = _s5_tables(lam_re, lam_im, log_dt, b_re, b_im, c_re, c_im)
    uc = u.astype(BF16).reshape(bsz, nc, c, g, S5_GROUP)
    uc = jnp.transpose(uc, (3, 1, 0, 2, 4)).reshape(g, rows, c * S5_GROUP)
    kern = functools.partial(_s5_kernel, nc_ctx=n_ctx // c, nc_all=nc, bsz=bsz)
    cw = c * S5_GROUP
    y = pl.pallas_call(
        kern,
        grid=(g,),
        in_specs=[pl.BlockSpec((1, rows, cw), lambda i: (i, 0, 0)),
                  pl.BlockSpec((1, cw, cw), lambda i: (i, 0, 0)),
                  pl.BlockSpec((1, 4, cw, S5_STATE), lambda i: (i, 0, 0, 0)),
                  pl.BlockSpec((1, 4, S5_STATE, cw), lambda i: (i, 0, 0, 0)),
                  pl.BlockSpec((1, 4, S5_STATE), lambda i: (i, 0, 0))],
        out_specs=pl.BlockSpec((1, rows, cw), lambda i: (i, 0, 0)),
        out_shape=jax.ShapeDtypeStruct((g, rows, cw), F32),
        scratch_shapes=[pltpu.VMEM((4, rows, S5_STATE), F32), pltpu.VMEM((4, rows, S5_STATE), F32)],
        compiler_params=_params("parallel"),
        name="s5_scan",
    )(uc, toe, n_tab, m_tab, d_tab)
    y = y.reshape(g, nc, bsz, c, S5_GROUP)
    return jnp.transpose(y, (2, 1, 3, 0, 4)).reshape(bsz, t, width)


def _rwkv_kernel(at_ref, rt_ref, bt_ref, kt_ref, v_ref, pc_ref, y_ref, s_sc, *, heads, c):
    n = pl.program_id(2)

    @pl.when(n == 0)
    def _():
        s_sc[...] = jnp.zeros(s_sc.shape, F32)

    row = lax.broadcasted_iota(jnp.int32, (c, c), 0)
    col = lax.broadcasted_iota(jnp.int32, (c, c), 1)
    ahead = (row - col) * jnp.where(pl.program_id(0) == 0, 1, -1)
    strict = ahead > 0
    incl = ahead >= 0
    nt = (((1,), (1,)), ((), ()))
    tn = (((0,), (0,)), ((), ()))
    n_factors = int(math.log2(c))
    hs = range(heads)
    sls = [slice(j * RK_HEAD, (j + 1) * RK_HEAD) for j in hs]

    def dot(a, b, dims=None):
        a, b = a.astype(BF16), b.astype(BF16)
        if dims is None:
            return jnp.dot(a, b, preferred_element_type=F32)
        return lax.dot_general(a, b, dims, preferred_element_type=F32)

    vv = [v_ref[0, :, sl] for sl in sls]
    s0 = [s_sc[j] for j in hs]
    ar = [jnp.concatenate([at_ref[0, 0, :, sl], rt_ref[0, 0, :, sl]], axis=0) for sl in sls]
    bk = [jnp.concatenate([bt_ref[0, 0, :, sl], kt_ref[0, 0, :, sl]], axis=0) for sl in sls]
    g = [dot(ar[j], bk[j], nt) for j in hs]
    ah = [dot(ar[j], s0[j], nt) for j in hs]
    x = [jnp.where(strict, g[j][:c, :c], 0.0) for j in hs]
    a_ak = [jnp.where(strict, g[j][:c, c:], 0.0) for j in hs]
    m = [jnp.concatenate([jnp.where(incl, g[j][c:, :c], 0.0), jnp.where(incl, g[j][c:, c:], 0.0)], axis=1)
         for j in hs]
    u = [ah[j][:c] + dot(a_ak[j], vv[j]) for j in hs]
    u = [u[j] + dot(x[j], u[j]) for j in hs]
    for _ in range(n_factors - 1):
        x = [dot(x[j], x[j]) for j in hs]
        u = [u[j] + dot(x[j], u[j]) for j in hs]
    uv = [jnp.concatenate([u[j].astype(BF16), vv[j]], axis=0) for j in hs]
    for j in hs:
        y_ref[0, 0, :, sls[j]] = ah[j][c:] + dot(m[j], uv[j])
    for j in hs:
        s_new = s0[j] + dot(uv[j], bk[j], tn)
        s_sc[j] = s_new * pc_ref[0, 0, 0, :, sls[j]]


def rwkv7_scan(at, rt, bt, kt, v, pc, n_ctx):
    _, bsz, t, width = at.shape
    heads = width // RK_HEAD
    c = RK_CHUNK
    assert t % c == 0 and n_ctx % c == 0
    ncc, nc = n_ctx // c, t // c

    def chunk(d, n):
        back = jnp.where(n < ncc, ncc - 1 - n, ncc + (nc - 1 - n))
        return jnp.where(d == 0, n, back)

    stream = pl.BlockSpec((1, 1, c, width), lambda d, bi, n: (d, bi, chunk(d, n), 0))
    kern = functools.partial(_rwkv_kernel, heads=heads, c=c)
    return pl.pallas_call(
        kern,
        grid=(2, bsz, nc),
        in_specs=[stream, stream, stream, stream,
                  pl.BlockSpec((1, c, width), lambda d, bi, n: (bi, chunk(d, n), 0)),
                  pl.BlockSpec((1, 1, 1, 1, width), lambda d, bi, n: (d, bi, chunk(d, n), 0, 0))],
        out_specs=stream,
        out_shape=jax.ShapeDtypeStruct((2, bsz, t, width), F32),
        scratch_shapes=[pltpu.VMEM((heads, RK_HEAD, RK_HEAD), F32)],
        compiler_params=_params("parallel", "parallel", "arbitrary"),
        name="rwkv7_scan",
    )(at, rt, bt, kt, v, pc)


def _deinterleave_kernel(w_ref, p_ref, o_ref):
    o_ref[...] = jnp.dot(w_ref[...].astype(BF16), p_ref[...], preferred_element_type=F32).astype(o_ref.dtype)


def deinterleave_columns(w):
    r, n2 = w.shape
    perm = jnp.concatenate([jnp.arange(0, n2, 2), jnp.arange(1, n2, 2)])
    p = (jnp.arange(n2)[:, None] == perm[None, :]).astype(BF16)
    tm = _tile(r, 1024, 16)
    return pl.pallas_call(
        _deinterleave_kernel,
        grid=(r // tm,),
        in_specs=[pl.BlockSpec((tm, n2), lambda i: (i, 0)), pl.BlockSpec((n2, n2), lambda i: (0, 0))],
        out_specs=pl.BlockSpec((tm, n2), lambda i: (i, 0)),
        out_shape=jax.ShapeDtypeStruct((r, n2), BF16),
        compiler_params=_params("parallel"),
        name="deinterleave_columns",
    )(w, p)


def _moe_kernel(be_ref, nb_ref, x_ref, g_ref, w1_ref, b1g_ref, b1l_ref, w2_ref, b2_ref, o_ref):
    i = pl.program_id(0)
    de = w2_ref.shape[1]

    @pl.when(i < nb_ref[0])
    def _():
        x = x_ref[...]
        z = jnp.dot(x, w1_ref[0], preferred_element_type=F32)
        zg = z[:, :de] + b1g_ref[0]
        zl = z[:, de:] + b1l_ref[0]
        zg = jnp.minimum(zg, SWIGLU_LIMIT)
        zl = jnp.clip(zl, -SWIGLU_LIMIT, SWIGLU_LIMIT)
        act = zg * jax.nn.sigmoid(SWIGLU_ALPHA * zg) * (zl + 1.0)
        y = jnp.dot(act.astype(BF16), w2_ref[0], preferred_element_type=F32) + b2_ref[0]
        o_ref[...] = y * g_ref[...]

    @pl.when(i >= nb_ref[0])
    def _():
        o_ref[...] = jnp.zeros(o_ref.shape, F32)


def moe_ffn(h, w_router, b_router, w1, b1, w2, b2):
    n, dm = h.shape
    n_exp = w_router.shape[1]
    de = w2.shape[1]
    tm = MOE_TM
    logits = jnp.dot(h, w_router, precision=HI, preferred_element_type=F32) + b_router.astype(F32)
    top_val, top_idx = lax.top_k(logits, TOP_K)
    gates = jax.nn.softmax(top_val, axis=-1)
    flat_e = top_idx.reshape(-1).astype(jnp.int32)
    onehot = (flat_e[:, None] == jnp.arange(n_exp, dtype=jnp.int32)[None, :]).astype(jnp.int32)
    csum = jnp.cumsum(onehot, axis=0)
    rank = jnp.take_along_axis(csum, flat_e[:, None], axis=1)[:, 0] - 1
    counts = csum[-1]
    padded = (counts + tm - 1) // tm * tm
    pad_end = jnp.cumsum(padded)
    pad_start = pad_end - padded
    dest = pad_start[flat_e] + rank
    n_blocks = -(-(n * TOP_K) // tm) + n_exp
    cap = n_blocks * tm
    slot_src = jnp.full((cap,), -1, jnp.int32).at[dest].set(jnp.arange(n * TOP_K, dtype=jnp.int32))
    filled = slot_src >= 0
    slot_tok = jnp.where(filled, slot_src // TOP_K, n)
    slot_gate = jnp.where(filled, gates.reshape(-1)[jnp.maximum(slot_src, 0)], 0.0)
    block_exp = jnp.minimum(jnp.searchsorted(pad_end, jnp.arange(n_blocks) * tm, side='right'),
                            n_exp - 1).astype(jnp.int32)
    n_used = (pad_end[-1] // tm).astype(jnp.int32).reshape(1)
    h_pad = jnp.concatenate([h.astype(BF16), jnp.zeros((1, dm), BF16)], axis=0)
    x_sorted = h_pad[slot_tok]

    w1p = deinterleave_columns(w1.reshape(n_exp * dm, 2 * de)).reshape(n_exp, dm, 2 * de)
    b1g = b1[:, None, 0::2].astype(F32)
    b1l = b1[:, None, 1::2].astype(F32)
    w2b = w2.astype(BF16)
    b2r = b2[:, None, :].astype(F32)

    grid_spec = pltpu.PrefetchScalarGridSpec(
        num_scalar_prefetch=2,
        grid=(n_blocks,),
        in_specs=[pl.BlockSpec((tm, dm), lambda i, be, nb: (i, 0)),
                  pl.BlockSpec((tm, 1), lambda i, be, nb: (i, 0)),
                  pl.BlockSpec((1, dm, 2 * de), lambda i, be, nb: (be[i], 0, 0)),
                  pl.BlockSpec((1, 1, de), lambda i, be, nb: (be[i], 0, 0)),
                  pl.BlockSpec((1, 1, de), lambda i, be, nb: (be[i], 0, 0)),
                  pl.BlockSpec((1, de, dm), lambda i, be, nb: (be[i], 0, 0)),
                  pl.BlockSpec((1, 1, dm), lambda i, be, nb: (be[i], 0, 0))],
        out_specs=pl.BlockSpec((tm, dm), lambda i, be, nb: (i, 0)),
    )
    yb = pl.pallas_call(
        _moe_kernel,
        grid_spec=grid_spec,
        out_shape=jax.ShapeDtypeStruct((cap, dm), F32),
        compiler_params=_params("arbitrary"),
        name="moe_experts",
    )(block_exp, n_used, x_sorted, slot_gate[:, None], w1p, b1g, b1l, w2b, b2r)
    slots = dest.reshape(n, TOP_K)
    y = yb[slots[:, 0]]
    for kk in range(1, TOP_K):
        y = y + yb[slots[:, kk]]
    return y


def _rms(x, g, eps=NORM_EPS):
    xf = x.astype(F32)
    return xf * lax.rsqrt(jnp.mean(xf * xf, axis=-1, keepdims=True) + eps) * g.astype(F32)


def _modulated(h, n_ctx, norm_w, scale_c, shift_c, scale_x, shift_x):
    hn = _rms(h, norm_w)
    ac = hn[:, :n_ctx] * (1.0 + scale_c) + shift_c
    ax = hn[:, n_ctx:] * (1.0 + scale_x[:, None, :]) + shift_x[:, None, :]
    return jnp.concatenate([ac, ax], axis=1)


def _gated_add(h, n_ctx, out, gate_c, gate_x):
    return jnp.concatenate([h[:, :n_ctx] + gate_c * out[:, :n_ctx],
                            h[:, n_ctx:] + gate_x[:, None, :] * out[:, n_ctx:]], axis=1)


def _even_mixer(a, n_ctx, w_in, lam_re, lam_im, log_dt, b_re, b_im, c_re, c_im, d_skip,
                glu_w, glu_b, rpb):
    bsz, t, dm = a.shape
    mix_w = dm // 2
    a2 = a.reshape(bsz * t, dm).astype(BF16)
    u = matmul(a2, w_in[:, :mix_w]).reshape(bsz, t, mix_w)
    qkv = matmul(a2, w_in[:, mix_w:], out_dtype=BF16).reshape(bsz, t, 3 * mix_w)
    y = d_skip.astype(F32) * u + s5_scan(u, n_ctx, lam_re, lam_im, log_dt, b_re, b_im, c_re, c_im)
    gl = jax.nn.gelu(y)
    ya = gl * jax.nn.sigmoid(matmul(gl.reshape(bsz * t, mix_w), glu_w).reshape(bsz, t, mix_w)
                             + glu_b.astype(F32))
    yb = neighborhood_attention(qkv, rpb, n_ctx)
    return ya.astype(BF16), yb


def _token_shift(z, mu):
    zp = jnp.pad(z, ((0, 0), (1, 0), (0, 0)))[:, :-1]
    zn = jnp.pad(z, ((0, 0), (0, 1), (0, 0)))[:, 1:]
    return z + mu[0] * (zp - z) + mu[1] * (zn - z)


def _rope_2d(tq, row_pos, col_pos):
    e = tq.shape[-1]
    half = e // 2
    inv = ROPE_THETA ** (-jnp.arange(0, half, 2, dtype=F32) / half)

    def rot(u, pos):
        ang = pos.astype(F32)[:, None] * inv[None, :]
        cos = jnp.cos(ang)[None, :, None, :]
        sin = jnp.sin(ang)[None, :, None, :]
        u1, u2 = u[..., :half // 2], u[..., half // 2:]
        return jnp.concatenate([u1 * cos - u2 * sin, u2 * cos + u1 * sin], axis=-1)

    return jnp.concatenate([rot(tq[..., :half], row_pos), rot(tq[..., half:], col_pos)], axis=-1)


def _odd_mixer(a, n_ctx, w_in, mu, g_up, w0, w_up, a0, a_up, k_k, k_a, r_k, ln_w, ln_b,
               q_norm, k_norm):
    bsz, t, dm = a.shape
    mix_w = dm // 2
    heads = mix_w // RK_HEAD
    rk_in = 3 * mix_w + RK_GATE_RANK + 2 * RK_DECAY_RANK + 2 * RK_ICLR_RANK
    gq_heads = mix_w // GQ_HEAD_DIM
    kv_heads = gq_heads // 4
    kv_w = kv_heads * GQ_HEAD_DIM
    a2 = a.reshape(bsz * t, dm).astype(BF16)
    zr = matmul(a2, w_in[:, :rk_in]).reshape(bsz, t, rk_in)
    zqk = matmul(a2, w_in[:, rk_in:rk_in + mix_w + kv_w]).reshape(bsz, t, mix_w + kv_w)
    vv = matmul(a2, w_in[:, rk_in + mix_w + kv_w:], out_dtype=BF16).reshape(bsz, t, kv_w)

    muf = mu.astype(F32)
    zs = jnp.concatenate([_token_shift(zr[:, :n_ctx], muf), _token_shift(zr[:, n_ctx:], muf)], axis=1)
    r = zs[..., :mix_w]
    kk0 = zs[..., mix_w:2 * mix_w]
    v = zs[..., 2 * mix_w:3 * mix_w]
    o = 3 * mix_w
    g_lo = zs[..., o:o + RK_GATE_RANK]
    w_lo = zs[..., o + RK_GATE_RANK:o + RK_GATE_RANK + 2 * RK_DECAY_RANK]
    a_lo = zs[..., o + RK_GATE_RANK + 2 * RK_DECAY_RANK:]
    m = bsz * t

    def hshape(x):
        return x.reshape(bsz, t, heads, RK_HEAD)

    def chunked(x):
        return x.reshape(bsz, t // RK_CHUNK, RK_CHUNK, mix_w)

    gate = matmul(jax.nn.sigmoid(g_lo).reshape(m, RK_GATE_RANK), g_up).reshape(bsz, t, mix_w)
    k_s, ats, rts, bts, kts, pcs = [], [], [], [], [], []
    for d in range(2):
        wl = jnp.tanh(w_lo[..., d * RK_DECAY_RANK:(d + 1) * RK_DECAY_RANK]).reshape(m, RK_DECAY_RANK)
        w_log = -jax.nn.softplus(-(w0[d] + matmul(wl, w_up[d]).reshape(bsz, t, mix_w))) - 0.5
        log_decay = -jnp.exp(w_log)
        al = a_lo[..., d * RK_ICLR_RANK:(d + 1) * RK_ICLR_RANK].reshape(m, RK_ICLR_RANK)
        iclr = jax.nn.sigmoid(a0[d] + matmul(al, a_up[d]).reshape(bsz, t, mix_w))
        kk = hshape(kk0 * k_k[d])
        kk = kk / jnp.maximum(jnp.sqrt(jnp.sum(kk * kk, axis=-1, keepdims=True)), 1e-12)
        kk = kk.reshape(bsz, t, mix_w)
        k_d = kk0 * (1.0 + (iclr - 1.0) * k_a[d])
        k_s.append(k_d)
        lw = chunked(log_decay)
        cum = lax.cumsum(lw, axis=2, reverse=(d == 1))
        p_inv = jnp.exp(-cum)
        ats.append((chunked(-kk) * jnp.exp(cum - lw)).astype(BF16))
        rts.append((chunked(r) * jnp.exp(cum)).astype(BF16))
        bts.append((chunked(kk * iclr) * p_inv).astype(BF16))
        kts.append((chunked(k_d) * p_inv).astype(BF16))
        last = cum[:, :, -1:, :] if d == 0 else cum[:, :, :1, :]
        pcs.append(jnp.exp(last))

    def both(xs):
        return jnp.stack(xs, axis=0).reshape(2, bsz, t, mix_w)

    ys = rwkv7_scan(both(ats), both(rts), both(bts), both(kts), v.astype(BF16), jnp.stack(pcs, axis=0), n_ctx)
    ysum = hshape(ys[0] + ys[1])
    mean = jnp.mean(ysum, axis=-1, keepdims=True)
    var = jnp.mean(jnp.square(ysum - mean), axis=-1, keepdims=True)
    yn = ((ysum - mean) * lax.rsqrt(var + RK_GN_EPS)).reshape(bsz, t, mix_w)
    yn = yn * ln_w.astype(F32) + ln_b.astype(F32)
    rkf = r_k.astype(F32)
    bonus = 0.0
    for d in range(2):
        bonus = bonus + jnp.sum(hshape(r) * hshape(k_s[d]) * rkf, axis=-1, keepdims=True) * hshape(v)
    yc = (yn + bonus.reshape(bsz, t, mix_w)) * gate

    q = zqk[..., :mix_w].reshape(bsz, t, gq_heads, GQ_HEAD_DIM)
    k = zqk[..., mix_w:].reshape(bsz, t, kv_heads, GQ_HEAD_DIM)
    qn = _rms(q, q_norm)
    kn = _rms(k, k_norm)
    pos = jnp.arange(t - n_ctx)
    qn = jnp.concatenate([qn[:, :n_ctx], _rope_2d(qn[:, n_ctx:], pos // GRID_W, pos % GRID_W)], axis=1)
    kn = jnp.concatenate([kn[:, :n_ctx], _rope_2d(kn[:, n_ctx:], pos // GRID_W, pos % GRID_W)], axis=1)
    qb = qn.reshape(bsz, t, mix_w).astype(BF16)
    kb = kn.reshape(bsz, t, kv_w).astype(BF16)
    yd = gqa_attention(qb, kb, vv, n_ctx, kv_heads)
    return yc.astype(BF16), yd


def kernel(x, c, ctx, c_ctx, w_mod, b_mod, norm1, norm2, final_norm, ev_w_in, ev_w_out, s5_lam_re, s5_lam_im, s5_log_dt, s5_b_re, s5_b_im, s5_c_re, s5_c_im, s5_d, s5_glu_w, s5_glu_b, na_rpb, od_w_in, od_w_out, rk_mu, rk_g_up, rk_w0, rk_w_up, rk_a0, rk_a_up, rk_k_k, rk_k_a, rk_r_k, rk_ln_w, rk_ln_b, gq_q_norm, gq_k_norm, moe_w_router, moe_b_router, moe_w1, moe_b1, moe_w2, moe_b2):
    bsz, seq, dm = x.shape
    n_ctx = ctx.shape[1]
    depth = w_mod.shape[0]
    t = n_ctx + seq
    h = jnp.concatenate([ctx, x], axis=1).astype(F32)
    cond = jnp.concatenate([jax.nn.silu(c), jax.nn.silu(c_ctx)[None, :]], axis=0)
    for i in range(depth):
        last = i == depth - 1
        j = i // 2
        mod = matmul(cond, w_mod[i]) + b_mod[i].astype(F32)
        sh1, sc1, g1, sh2, sc2, g2 = jnp.split(mod, N_MOD, axis=-1)
        a = _modulated(h, n_ctx, norm1[i], sc1[bsz], sh1[bsz], sc1[:bsz], sh1[:bsz])
        if i % 2 == 0:
            ya, yb = _even_mixer(a, n_ctx, ev_w_in[j], s5_lam_re[j], s5_lam_im[j], s5_log_dt[j],
                                 s5_b_re[j], s5_b_im[j], s5_c_re[j], s5_c_im[j], s5_d[j], s5_glu_w[j],
                                 s5_glu_b[j], na_rpb[j])
            w_out = ev_w_out[j]
        else:
            ya, yb = _odd_mixer(a, n_ctx, od_w_in[j], rk_mu[j], rk_g_up[j], rk_w0[j], rk_w_up[j],
                                rk_a0[j], rk_a_up[j], rk_k_k[j], rk_k_a[j], rk_r_k[j], rk_ln_w[j], rk_ln_b[j],
                                gq_q_norm[j], gq_k_norm[j])
            w_out = od_w_out[j]
        h, f = out_proj_residual_norm(ya, yb, w_out, h, n_ctx, g1, norm2[i], sc2, sh2)
        moe_args = (moe_w_router[i], moe_b_router[i], moe_w1[i], moe_b1[i], moe_w2[i], moe_b2[i])
        if last:
            y = moe_ffn(f[:, n_ctx:].reshape(bsz * seq, dm), *moe_args).reshape(bsz, seq, dm)
            hx = h[:, n_ctx:] + g2[:bsz][:, None, :] * y
            return _rms(hx, final_norm).astype(x.dtype)
        y = moe_ffn(f.reshape(bsz * t, dm), *moe_args).reshape(bsz, t, dm)
        h = _gated_add(h, n_ctx, y, g2[bsz], g2[:bsz])
```

```python
import functools
import math

import numpy as np
import jax
import jax.numpy as jnp
from jax import lax
from jax.experimental import pallas as pl
from jax.experimental.pallas import tpu as pltpu

F32 = jnp.float32
BF16 = jnp.bfloat16

V7X_LANES = 128
V7X_SUBLANES = 8
V7X_MXU_DIM = 256
V7X_VMEM_BYTES = 64 * 1024 * 1024
VMEM_LIMIT = 56 * 1024 * 1024

GRID_W = 64
N_MOD = 6
NORM_EPS = 1e-6
S5_GROUP = 16
S5_STATE = 64
S5_CHUNK = 16
NA_HEADS = 8
WIN_ROWS = 8
WIN_COLS = 16
NA_QROWS = 4
NA_KROWS = NA_QROWS + WIN_ROWS - 1
RK_HEAD = 64
RK_DECAY_RANK = 64
RK_ICLR_RANK = 64
RK_GATE_RANK = 128
RK_GN_EPS = 64e-5
RK_CHUNK = 64
RK_TBLOCK = 256
GQ_HEAD_DIM = 128
ROPE_THETA = 10000.0
GQ_TQ = 128
GQ_TK = 2048
N_EXPERTS = 32
TOP_K = 4
SWIGLU_ALPHA = 1.702
SWIGLU_LIMIT = 7.0
MOE_TM = 256
NEG_BIG = -1e30
HI = lax.Precision.HIGHEST


def _params(*sem):
    return pltpu.CompilerParams(dimension_semantics=sem, vmem_limit_bytes=VMEM_LIMIT)


def _tile(n, target, align):
    best = None
    t = align
    while t <= min(n, target):
        if n % t == 0:
            best = t
        t += align
    return n if best is None else best


def _mm_kernel(x_ref, w_ref, o_ref):
    o_ref[...] = jnp.dot(x_ref[...], w_ref[...], preferred_element_type=F32).astype(o_ref.dtype)


def matmul(x, w, out_dtype=F32):
    m, k = x.shape
    n = w.shape[1]
    x = x.astype(BF16)
    w = w.astype(BF16)
    m_pad = -(-m // 16) * 16
    if m_pad != m:
        x = jnp.pad(x, ((0, m_pad - m), (0, 0)))
    tm = _tile(m_pad, 512, 16)
    tn = _tile(n, 2048, V7X_LANES)
    out = pl.pallas_call(
        _mm_kernel,
        grid=(n // tn, m_pad // tm),
        in_specs=[pl.BlockSpec((tm, k), lambda j, i: (i, 0)),
                  pl.BlockSpec((k, tn), lambda j, i: (0, j))],
        out_specs=pl.BlockSpec((tm, tn), lambda j, i: (i, j)),
        out_shape=jax.ShapeDtypeStruct((m_pad, n), out_dtype),
        compiler_params=_params("parallel", "parallel"),
        name="matmul",
    )(x, w)
    return out[:m] if m_pad != m else out


def _out_proj_kernel(xa_ref, xb_ref, wa_ref, wb_ref, h_ref, g_ref, nw_ref, sc_ref, sh_ref, hn_ref, f_ref):
    out = (jnp.dot(xa_ref[...], wa_ref[...], preferred_element_type=F32)
           + jnp.dot(xb_ref[...], wb_ref[...], preferred_element_type=F32))
    hn = h_ref[...] + g_ref[0] * out
    hn_ref[...] = hn
    normed = hn * lax.rsqrt(jnp.mean(hn * hn, axis=-1, keepdims=True) + NORM_EPS) * nw_ref[...]
    f_ref[...] = (normed * (1.0 + sc_ref[0]) + sh_ref[0]).astype(f_ref.dtype)


def out_proj_residual_norm(xa, xb, w_out, h, n_ctx, gate, norm_w, scale, shift):
    bsz, t, dm = h.shape
    wa = xa.shape[-1]
    tm = _tile(n_ctx, 256, 16)
    assert t % tm == 0
    nbt, ncb = t // tm, n_ctx // tm

    def mod_row(i):
        return jnp.where(i % nbt < ncb, bsz, i // nbt)

    rows = pl.BlockSpec((tm, dm), lambda i: (i, 0))
    mod = pl.BlockSpec((1, 1, dm), lambda i: (mod_row(i), 0, 0))
    m = bsz * t
    w_out = w_out.astype(BF16)
    hn, f = pl.pallas_call(
        _out_proj_kernel,
        grid=(m // tm,),
        in_specs=[pl.BlockSpec((tm, wa), lambda i: (i, 0)),
                  pl.BlockSpec((tm, dm - wa), lambda i: (i, 0)),
                  pl.BlockSpec((wa, dm), lambda i: (0, 0)),
                  pl.BlockSpec((dm - wa, dm), lambda i: (0, 0)),
                  rows, mod, pl.BlockSpec((1, dm), lambda i: (0, 0)), mod, mod],
        out_specs=[rows, rows],
        out_shape=[jax.ShapeDtypeStruct((m, dm), F32), jax.ShapeDtypeStruct((m, dm), BF16)],
        compiler_params=_params("parallel"),
        name="out_proj_residual_norm",
    )(xa.reshape(m, wa), xb.reshape(m, dm - wa), w_out[:wa], w_out[wa:], h.reshape(m, dm),
      gate[:, None, :], norm_w.astype(F32)[None, :], scale[:, None, :], shift[:, None, :])
    return hn.reshape(bsz, t, dm), f.reshape(bsz, t, dm)


def _flash_kernel(q_ref, k_ref, v_ref, o_ref, m_sc, l_sc, acc_sc, *, grp, hd, tq, tk, n_ctx,
                  n_ctx_qblocks, n_lat_chunks):
    qi = pl.program_id(2)
    q = jnp.concatenate([q_ref[0, :, g * hd:(g + 1) * hd] for g in range(grp)], axis=0)

    scale = hd ** -0.5
    s = lax.dot_general(q, k_ref[0, 0:n_ctx, :], (((1,), (1,)), ((), ())),
                        preferred_element_type=F32) * scale
    m0 = jnp.max(s, axis=-1, keepdims=True)
    p = jnp.exp(s - m0)
    m_sc[...] = m0
    l_sc[...] = jnp.sum(p, axis=-1, keepdims=True)
    acc_sc[...] = jnp.dot(p.astype(BF16), v_ref[0, 0:n_ctx, :], preferred_element_type=F32)

    def body(j, carry):
        off = pl.multiple_of(n_ctx + j * tk, V7X_LANES)
        k = k_ref[0, pl.ds(off, tk), :]
        v = v_ref[0, pl.ds(off, tk), :]
        s = lax.dot_general(q, k, (((1,), (1,)), ((), ())), preferred_element_type=F32) * scale
        m_prev = m_sc[...]
        m_new = jnp.maximum(m_prev, jnp.max(s, axis=-1, keepdims=True))
        alpha = jnp.exp(m_prev - m_new)
        p = jnp.exp(s - m_new)
        l_sc[...] = alpha * l_sc[...] + jnp.sum(p, axis=-1, keepdims=True)
        acc_sc[...] = alpha * acc_sc[...] + jnp.dot(p.astype(BF16), v, preferred_element_type=F32)
        m_sc[...] = m_new
        return carry

    lax.fori_loop(0, jnp.where(qi < n_ctx_qblocks, 0, n_lat_chunks), body, 0)
    o = acc_sc[...] / l_sc[...]
    for g in range(grp):
        o_ref[0, :, g * hd:(g + 1) * hd] = o[g * tq:(g + 1) * tq].astype(o_ref.dtype)


def gqa_attention(qk, v, n_ctx, n_q_heads, n_kv_heads):
    b, t, _ = qk.shape
    hd = GQ_HEAD_DIM
    qw = n_q_heads * hd
    grp = n_q_heads // n_kv_heads
    tq = GQ_TQ
    tk = _tile(t - n_ctx, GQ_TK, V7X_LANES)
    assert n_ctx % tq == 0 and t % tq == 0 and n_ctx % V7X_LANES == 0
    kern = functools.partial(_flash_kernel, grp=grp, hd=hd, tq=tq, tk=tk, n_ctx=n_ctx,
                             n_ctx_qblocks=n_ctx // tq, n_lat_chunks=(t - n_ctx) // tk)
    rows = grp * tq
    return pl.pallas_call(
        kern,
        grid=(b, n_kv_heads, t // tq),
        in_specs=[pl.BlockSpec((1, tq, grp * hd), lambda bi, h, i: (bi, i, h)),
                  pl.BlockSpec((1, t, hd), lambda bi, h, i: (bi, 0, n_q_heads + h)),
                  pl.BlockSpec((1, t, hd), lambda bi, h, i: (bi, 0, h))],
        out_specs=pl.BlockSpec((1, tq, grp * hd), lambda bi, h, i: (bi, i, h)),
        out_shape=jax.ShapeDtypeStruct((b, t, qw), BF16),
        scratch_shapes=[pltpu.VMEM((rows, 1), F32), pltpu.VMEM((rows, 1), F32),
                        pltpu.VMEM((rows, hd), F32)],
        compiler_params=_params("parallel", "parallel", "parallel"),
        name="gqa_attention",
    )(qk, qk, v)


def _na_kernel(q_ref, k_ref, v_ref, bias_ref, o_ref, *, n_ctx, rows, n_blocks):
    j = pl.program_id(2)
    q = q_ref[0]
    kc = k_ref[0, 0:n_ctx, :]
    vc = v_ref[0, 0:n_ctx, :]
    dn = (((1,), (1,)), ((), ()))
    scale = q.shape[-1] ** -0.5
    s_ctx = lax.dot_general(q, kc, dn, preferred_element_type=F32) * scale

    @pl.when(j == 0)
    def _():
        m = jnp.max(s_ctx, axis=-1, keepdims=True)
        p = jnp.exp(s_ctx - m)
        l = jnp.sum(p, axis=-1, keepdims=True)
        o = jnp.dot(p.astype(BF16), vc, preferred_element_type=F32) / l
        o_ref[0] = o.astype(o_ref.dtype)

    @pl.when(j > 0)
    def _():
        jj = j - 1
        ks = jnp.clip(NA_QROWS * jj - WIN_ROWS // 2, 0, rows - NA_KROWS)
        off = pl.multiple_of(n_ctx + ks * GRID_W, GRID_W)
        kw = k_ref[0, pl.ds(off, NA_KROWS * GRID_W), :]
        vw = v_ref[0, pl.ds(off, NA_KROWS * GRID_W), :]
        pat = jnp.where(jj == 0, 0, jnp.where(jj == n_blocks - 1, 2, 1))
        s_win = lax.dot_general(q, kw, dn, preferred_element_type=F32) * scale + bias_ref[pat]
        m = jnp.maximum(jnp.max(s_win, axis=-1, keepdims=True), jnp.max(s_ctx, axis=-1, keepdims=True))
        p_win = jnp.exp(s_win - m)
        p_ctx = jnp.exp(s_ctx - m)
        l = jnp.sum(p_win, axis=-1, keepdims=True) + jnp.sum(p_ctx, axis=-1, keepdims=True)
        o = (jnp.dot(p_win.astype(BF16), vw, preferred_element_type=F32)
             + jnp.dot(p_ctx.astype(BF16), vc, preferred_element_type=F32)) / l
        o_ref[0] = o.astype(o_ref.dtype)


def _na_bias(rpb, rows):
    n_blocks = rows // NA_QROWS
    col = np.arange(GRID_W)
    col_start = np.clip(col - WIN_COLS // 2, 0, GRID_W - WIN_COLS)
    col_ok = (col[None, :] >= col_start[:, None]) & (col[None, :] < col_start[:, None] + WIN_COLS)
    dc = np.clip(col[None, :] - col[:, None] + (WIN_COLS - 1), 0, 2 * WIN_COLS - 2)
    pick_c = (dc[:, :, None] == np.arange(2 * WIN_COLS - 1)).astype(np.float32)
    pats = []
    for jj in (0, 1, n_blocks - 1):
        ks = min(max(NA_QROWS * jj - WIN_ROWS // 2, 0), rows - NA_KROWS)
        r = NA_QROWS * jj + np.arange(NA_QROWS)
        kr = ks + np.arange(NA_KROWS)
        r_start = np.clip(r - WIN_ROWS // 2, 0, rows - WIN_ROWS)
        row_ok = (kr[None, :] >= r_start[:, None]) & (kr[None, :] < r_start[:, None] + WIN_ROWS)
        dr = np.clip(kr[None, :] - r[:, None] + (WIN_ROWS - 1), 0, 2 * WIN_ROWS - 2)
        pick_r = (dr[:, :, None] == np.arange(2 * WIN_ROWS - 1)).astype(np.float32)
        bias = jnp.einsum('hab,rka,qcb->hrqkc', rpb.astype(F32), pick_r, pick_c, precision=HI)
        ok = row_ok[:, None, :, None] & col_ok[None, :, None, :]
        bias = jnp.where(ok[None], bias, NEG_BIG)
        pats.append(bias.reshape(rpb.shape[0], NA_QROWS * GRID_W, NA_KROWS * GRID_W))
    return jnp.stack(pats, axis=0)


def neighborhood_attention(qkv, rpb, n_ctx):
    b, t, width3 = qkv.shape
    width = width3 // 3
    hd = width // NA_HEADS
    seq = t - n_ctx
    rows = seq // GRID_W
    qb = NA_QROWS * GRID_W
    assert n_ctx == qb and rows % NA_QROWS == 0 and rows >= NA_KROWS and WIN_ROWS <= rows
    n_blocks = rows // NA_QROWS
    bias = _na_bias(rpb, rows)
    kern = functools.partial(_na_kernel, n_ctx=n_ctx, rows=rows, n_blocks=n_blocks)
    kb = NA_KROWS * GRID_W
    return pl.pallas_call(
        kern,
        grid=(NA_HEADS, b, n_blocks + 1),
        in_specs=[pl.BlockSpec((1, qb, hd), lambda h, bi, j: (bi, j, h)),
                  pl.BlockSpec((1, t, hd), lambda h, bi, j: (bi, 0, NA_HEADS + h)),
                  pl.BlockSpec((1, t, hd), lambda h, bi, j: (bi, 0, 2 * NA_HEADS + h)),
                  pl.BlockSpec((3, None, qb, kb), lambda h, bi, j: (0, h, 0, 0))],
        out_specs=pl.BlockSpec((1, qb, hd), lambda h, bi, j: (bi, j, h)),
        out_shape=jax.ShapeDtypeStruct((b, t, width), BF16),
        compiler_params=_params("parallel", "parallel", "parallel"),
        name="neighborhood_attention",
    )(qkv, qkv, qkv, bias)


def _s5_kernel(u_ref, t_ref, n_ref, m_ref, d_ref, y_ref, hl_sc, hp_sc, *, nc_ctx, nc_all, bsz):
    u = u_ref[0]
    for part in range(4):
        hl_sc[part] = jnp.dot(u, n_ref[0, part], preferred_element_type=F32)
    zero = jnp.zeros((bsz, S5_STATE), F32)

    def scan(direction):
        dr = d_ref[0, 2 * direction:2 * direction + 1, :]
        di = d_ref[0, 2 * direction + 1:2 * direction + 2, :]

        def body(step, carry):
            hr, hi = carry
            if direction == 0:
                c = step
            else:
                c = jnp.where(step < nc_ctx, nc_ctx - 1 - step, nc_all - 1 - (step - nc_ctx))
            r0 = pl.multiple_of(c * bsz, bsz)
            hp_sc[2 * direction, pl.ds(r0, bsz), :] = hr
            hp_sc[2 * direction + 1, pl.ds(r0, bsz), :] = hi
            lr = hl_sc[2 * direction, pl.ds(r0, bsz), :]
            li = hl_sc[2 * direction + 1, pl.ds(r0, bsz), :]
            return dr * hr - di * hi + lr, dr * hi + di * hr + li

        lax.fori_loop(0, nc_all, body, (zero, zero))

    scan(0)
    scan(1)
    y = jnp.dot(u, t_ref[0], preferred_element_type=F32)
    for part in range(4):
        y = y + jnp.dot(hp_sc[part].astype(BF16), m_ref[0, part], preferred_element_type=F32)
    y_ref[0] = y


def _s5_tables(lam_re, lam_im, log_dt, b_re, b_im, c_re, c_im):
    c = S5_CHUNK
    g, p = lam_re.shape[1], lam_re.shape[2]
    h = b_re.shape[-1]
    toe = 0.0
    n_parts, m_parts, d_parts = [], [], []
    tt = jnp.arange(c)
    for d in range(2):
        lr, li = lam_re[d].astype(F32), lam_im[d].astype(F32)
        dt = jnp.exp(log_dt[d].astype(F32))[:, None]

        def lam_pow(e):
            e = jnp.asarray(e, F32)[..., None, None]
            mag = jnp.exp(e * lr * dt)
            return mag * jnp.cos(e * li * dt), mag * jnp.sin(e * li * dt)

        lbr, lbi = lam_pow(1)
        nr, ni = lbr - 1.0, lbi
        den = lr * lr + li * li
        fr, fi = (nr * lr + ni * li) / den, (ni * lr - nr * li) / den
        br, bi = b_re[d].astype(F32), b_im[d].astype(F32)
        bbr = fr[..., None] * br - fi[..., None] * bi
        bbi = fr[..., None] * bi + fi[..., None] * br
        cr, ci = c_re[d].astype(F32), c_im[d].astype(F32)
        pr, pi = lam_pow(tt)
        lbbr = pr[..., None] * bbr - pi[..., None] * bbi
        lbbi = pr[..., None] * bbi + pi[..., None] * bbr
        kern = (jnp.einsum('gop,tgpi->tgoi', cr, lbbr, precision=HI)
                - jnp.einsum('gop,tgpi->tgoi', ci, lbbi, precision=HI))
        steps = np.arange(c)
        lag = (steps[None, :] - steps[:, None]) if d == 0 else (steps[:, None] - steps[None, :])
        pick = (lag[:, :, None] == steps).astype(np.float32)
        kt = jnp.einsum('stx,xgoi->gsito', pick, kern, precision=HI)
        toe = toe + kt.reshape(g, c * h, c * h)
        e_in = (c - 1 - tt) if d == 0 else tt
        qr, qi = lam_pow(e_in)
        n_r = qr[..., None] * bbr - qi[..., None] * bbi
        n_i = qr[..., None] * bbi + qi[..., None] * bbr
        n_parts += [jnp.transpose(n_r, (1, 0, 3, 2)).reshape(g, c * h, p),
                    jnp.transpose(n_i, (1, 0, 3, 2)).reshape(g, c * h, p)]
        e_out = (tt + 1) if d == 0 else (c - tt)
        sr, si = lam_pow(e_out)
        clr = cr[None] * sr[:, :, None, :] - ci[None] * si[:, :, None, :]
        cli = cr[None] * si[:, :, None, :] + ci[None] * sr[:, :, None, :]
        m_parts += [jnp.transpose(clr, (1, 3, 0, 2)).reshape(g, p, c * h),
                    -jnp.transpose(cli, (1, 3, 0, 2)).reshape(g, p, c * h)]
        dcr, dci = lam_pow(c)
        d_parts += [dcr, dci]
    return (toe.astype(BF16), jnp.stack(n_parts, 1).astype(BF16), jnp.stack(m_parts, 1).astype(BF16),
            jnp.stack(d_parts, 1))


def s5_scan(u, n_ctx, lam_re, lam_im, log_dt, b_re, b_im, c_re, c_im):
    bsz, t, width = u.shape
    g = width // S5_GROUP
    c = S5_CHUNK
    assert bsz == V7X_SUBLANES and t % c == 0 and n_ctx % c == 0
    nc = t // c
    rows = nc * bsz
    toe, n_tab, m_tab, d_tab = _s5_tables(lam_re, lam_im, log_dt, b_re, b_im, c_re, c_im)
    uc = u.astype(BF16).reshape(bsz, nc, c, g, S5_GROUP)
    uc = jnp.transpose(uc, (3, 1, 0, 2, 4)).reshape(g, rows, c * S5_GROUP)
    kern = functools.partial(_s5_kernel, nc_ctx=n_ctx // c, nc_all=nc, bsz=bsz)
    cw = c * S5_GROUP
    y = pl.pallas_call(
        kern,
        grid=(g,),
        in_specs=[pl.BlockSpec((1, rows, cw), lambda i: (i, 0, 0)),
                  pl.BlockSpec((1, cw, cw), lambda i: (i, 0, 0)),
                  pl.BlockSpec((1, 4, cw, S5_STATE), lambda i: (i, 0, 0, 0)),
                  pl.BlockSpec((1, 4, S5_STATE, cw), lambda i: (i, 0, 0, 0)),
                  pl.BlockSpec((1, 4, S5_STATE), lambda i: (i, 0, 0))],
        out_specs=pl.BlockSpec((1, rows, cw), lambda i: (i, 0, 0)),
        out_shape=jax.ShapeDtypeStruct((g, rows, cw), F32),
        scratch_shapes=[pltpu.VMEM((4, rows, S5_STATE), F32), pltpu.VMEM((4, rows, S5_STATE), F32)],
        compiler_params=_params("parallel"),
        name="s5_scan",
    )(uc, toe, n_tab, m_tab, d_tab)
    y = y.reshape(g, nc, bsz, c, S5_GROUP)
    return jnp.transpose(y, (2, 1, 3, 0, 4)).reshape(bsz, t, width)


def _rwkv_kernel(at_ref, rt_ref, bt_ref, kt_ref, v_ref, pc_ref, y_ref, s_sc, *, heads, c):
    n = pl.program_id(2)

    @pl.when(n == 0)
    def _():
        s_sc[...] = jnp.zeros(s_sc.shape, F32)

    row = lax.broadcasted_iota(jnp.int32, (c, c), 0)
    col = lax.broadcasted_iota(jnp.int32, (c, c), 1)
    ahead = (row - col) * jnp.where(pl.program_id(0) == 0, 1, -1)
    strict = ahead > 0
    incl = ahead >= 0
    nt = (((1,), (1,)), ((), ()))
    tn = (((0,), (0,)), ((), ()))
    n_factors = int(math.log2(c))
    hs = range(heads)
    sls = [slice(j * RK_HEAD, (j + 1) * RK_HEAD) for j in hs]

    def dot(a, b, dims=None):
        a, b = a.astype(BF16), b.astype(BF16)
        if dims is None:
            return jnp.dot(a, b, preferred_element_type=F32)
        return lax.dot_general(a, b, dims, preferred_element_type=F32)

    vv = [v_ref[0, :, sl] for sl in sls]
    s0 = [s_sc[j] for j in hs]
    ar = [jnp.concatenate([at_ref[0, 0, :, sl], rt_ref[0, 0, :, sl]], axis=0) for sl in sls]
    bk = [jnp.concatenate([bt_ref[0, 0, :, sl], kt_ref[0, 0, :, sl]], axis=0) for sl in sls]
    g = [dot(ar[j], bk[j], nt) for j in hs]
    ah = [dot(ar[j], s0[j], nt) for j in hs]
    x = [jnp.where(strict, g[j][:c, :c], 0.0) for j in hs]
    a_ak = [jnp.where(strict, g[j][:c, c:], 0.0) for j in hs]
    m = [jnp.concatenate([jnp.where(incl, g[j][c:, :c], 0.0), jnp.where(incl, g[j][c:, c:], 0.0)], axis=1)
         for j in hs]
    u = [ah[j][:c] + dot(a_ak[j], vv[j]) for j in hs]
    u = [u[j] + dot(x[j], u[j]) for j in hs]
    for _ in range(n_factors - 1):
        x = [dot(x[j], x[j]) for j in hs]
        u = [u[j] + dot(x[j], u[j]) for j in hs]
    uv = [jnp.concatenate([u[j].astype(BF16), vv[j]], axis=0) for j in hs]
    for j in hs:
        y_ref[0, 0, :, sls[j]] = ah[j][c:] + dot(m[j], uv[j])
    for j in hs:
        s_new = s0[j] + dot(uv[j], bk[j], tn)
        s_sc[j] = s_new * pc_ref[0, 0, 0, :, sls[j]]


def rwkv7_scan(at, rt, bt, kt, v, pc, n_ctx):
    _, bsz, t, width = at.shape
    heads = width // RK_HEAD
    c = RK_CHUNK
    assert t % c == 0 and n_ctx % c == 0
    ncc, nc = n_ctx // c, t // c

    def chunk(d, n):
        back = jnp.where(n < ncc, ncc - 1 - n, ncc + (nc - 1 - n))
        return jnp.where(d == 0, n, back)

    stream = pl.BlockSpec((1, 1, c, width), lambda d, bi, n: (d, bi, chunk(d, n), 0))
    kern = functools.partial(_rwkv_kernel, heads=heads, c=c)
    return pl.pallas_call(
        kern,
        grid=(2, bsz, nc),
        in_specs=[stream, stream, stream, stream,
                  pl.BlockSpec((1, c, width), lambda d, bi, n: (bi, chunk(d, n), 0)),
                  pl.BlockSpec((1, 1, 1, 1, width), lambda d, bi, n: (d, bi, chunk(d, n), 0, 0))],
        out_specs=stream,
        out_shape=jax.ShapeDtypeStruct((2, bsz, t, width), F32),
        scratch_shapes=[pltpu.VMEM((heads, RK_HEAD, RK_HEAD), F32)],
        compiler_params=_params("parallel", "parallel", "arbitrary"),
        name="rwkv7_scan",
    )(at, rt, bt, kt, v, pc)


def _deinterleave_kernel(w_ref, p_ref, o_ref):
    o_ref[...] = jnp.dot(w_ref[...].astype(BF16), p_ref[...], preferred_element_type=F32).astype(o_ref.dtype)


def deinterleave_columns(w):
    r, n2 = w.shape
    perm = jnp.concatenate([jnp.arange(0, n2, 2), jnp.arange(1, n2, 2)])
    p = (jnp.arange(n2)[:, None] == perm[None, :]).astype(BF16)
    tm = _tile(r, 1024, 16)
    return pl.pallas_call(
        _deinterleave_kernel,
        grid=(r // tm,),
        in_specs=[pl.BlockSpec((tm, n2), lambda i: (i, 0)), pl.BlockSpec((n2, n2), lambda i: (0, 0))],
        out_specs=pl.BlockSpec((tm, n2), lambda i: (i, 0)),
        out_shape=jax.ShapeDtypeStruct((r, n2), BF16),
        compiler_params=_params("parallel"),
        name="deinterleave_columns",
    )(w, p)


def _moe_kernel(be_ref, nb_ref, x_ref, g_ref, w1_ref, b1g_ref, b1l_ref, w2_ref, b2_ref, o_ref):
    i = pl.program_id(0)
    de = w2_ref.shape[1]

    @pl.when(i < nb_ref[0])
    def _():
        x = x_ref[...]
        z = jnp.dot(x, w1_ref[0], preferred_element_type=F32)
        zg = z[:, :de] + b1g_ref[0]
        zl = z[:, de:] + b1l_ref[0]
        zg = jnp.minimum(zg, SWIGLU_LIMIT)
        zl = jnp.clip(zl, -SWIGLU_LIMIT, SWIGLU_LIMIT)
        act = zg * jax.nn.sigmoid(SWIGLU_ALPHA * zg) * (zl + 1.0)
        y = jnp.dot(act.astype(BF16), w2_ref[0], preferred_element_type=F32) + b2_ref[0]
        o_ref[...] = y * g_ref[...]

    @pl.when(i >= nb_ref[0])
    def _():
        o_ref[...] = jnp.zeros(o_ref.shape, F32)


def moe_ffn(h, w_router, b_router, w1, b1, w2, b2):
    n, dm = h.shape
    n_exp = w_router.shape[1]
    de = w2.shape[1]
    tm = MOE_TM
    logits = jnp.dot(h, w_router, precision=HI, preferred_element_type=F32) + b_router.astype(F32)
    top_val, top_idx = lax.top_k(logits, TOP_K)
    gates = jax.nn.softmax(top_val, axis=-1)
    flat_e = top_idx.reshape(-1).astype(jnp.int32)
    rb = V7X_LANES
    assert (n * TOP_K) % rb == 0
    onehot = (flat_e[:, None] == jnp.arange(n_exp, dtype=jnp.int32)[None, :]).astype(F32)
    blocks = onehot.reshape(-1, rb, n_exp)
    tri = np.tril(np.ones((rb, rb), np.float32))
    within = jnp.einsum('ij,bjk->bik', tri, blocks, precision=HI)
    totals = within[:, -1, :]
    before = jnp.cumsum(totals, axis=0) - totals
    csum = within + before[:, None, :]
    rank = (jnp.sum(csum * blocks, axis=-1).reshape(-1) - 1.0).astype(jnp.int32)
    counts = (before[-1] + totals[-1]).astype(jnp.int32)
    padded = (counts + tm - 1) // tm * tm
    pad_end = jnp.cumsum(padded)
    pad_start = pad_end - padded
    dest = pad_start[flat_e] + rank
    n_blocks = -(-(n * TOP_K) // tm) + n_exp
    cap = n_blocks * tm
    slot_src = jnp.full((cap,), -1, jnp.int32).at[dest].set(jnp.arange(n * TOP_K, dtype=jnp.int32))
    filled = slot_src >= 0
    slot_tok = jnp.where(filled, slot_src // TOP_K, n)
    slot_gate = jnp.where(filled, gates.reshape(-1)[jnp.maximum(slot_src, 0)], 0.0)
    block_exp = jnp.minimum(jnp.searchsorted(pad_end, jnp.arange(n_blocks) * tm, side='right'),
                            n_exp - 1).astype(jnp.int32)
    n_used = (pad_end[-1] // tm).astype(jnp.int32).reshape(1)
    h_pad = jnp.concatenate([h.astype(BF16), jnp.zeros((1, dm), BF16)], axis=0)
    x_sorted = h_pad[slot_tok]

    w1p = deinterleave_columns(w1.reshape(n_exp * dm, 2 * de)).reshape(n_exp, dm, 2 * de)
    b1g = b1[:, None, 0::2].astype(F32)
    b1l = b1[:, None, 1::2].astype(F32)
    w2b = w2.astype(BF16)
    b2r = b2[:, None, :].astype(F32)

    grid_spec = pltpu.PrefetchScalarGridSpec(
        num_scalar_prefetch=2,
        grid=(n_blocks,),
        in_specs=[pl.BlockSpec((tm, dm), lambda i, be, nb: (i, 0)),
                  pl.BlockSpec((tm, 1), lambda i, be, nb: (i, 0)),
                  pl.BlockSpec((1, dm, 2 * de), lambda i, be, nb: (be[i], 0, 0)),
                  pl.BlockSpec((1, 1, de), lambda i, be, nb: (be[i], 0, 0)),
                  pl.BlockSpec((1, 1, de), lambda i, be, nb: (be[i], 0, 0)),
                  pl.BlockSpec((1, de, dm), lambda i, be, nb: (be[i], 0, 0)),
                  pl.BlockSpec((1, 1, dm), lambda i, be, nb: (be[i], 0, 0))],
        out_specs=pl.BlockSpec((tm, dm), lambda i, be, nb: (i, 0)),
    )
    yb = pl.pallas_call(
        _moe_kernel,
        grid_spec=grid_spec,
        out_shape=jax.ShapeDtypeStruct((cap, dm), F32),
        compiler_params=_params("arbitrary"),
        name="moe_experts",
    )(block_exp, n_used, x_sorted, slot_gate[:, None], w1p, b1g, b1l, w2b, b2r)
    slots = dest.reshape(n, TOP_K)
    y = yb[slots[:, 0]]
    for kk in range(1, TOP_K):
        y = y + yb[slots[:, kk]]
    return y


def _rms(x, g, eps=NORM_EPS):
    xf = x.astype(F32)
    return xf * lax.rsqrt(jnp.mean(xf * xf, axis=-1, keepdims=True) + eps) * g.astype(F32)


def _modulated(h, n_ctx, norm_w, scale_c, shift_c, scale_x, shift_x):
    hn = _rms(h, norm_w)
    ac = hn[:, :n_ctx] * (1.0 + scale_c) + shift_c
    ax = hn[:, n_ctx:] * (1.0 + scale_x[:, None, :]) + shift_x[:, None, :]
    return jnp.concatenate([ac, ax], axis=1)


def _gated_add(h, n_ctx, out, gate_c, gate_x):
    return jnp.concatenate([h[:, :n_ctx] + gate_c * out[:, :n_ctx],
                            h[:, n_ctx:] + gate_x[:, None, :] * out[:, n_ctx:]], axis=1)


def _even_mixer(a, n_ctx, w_in, lam_re, lam_im, log_dt, b_re, b_im, c_re, c_im, d_skip,
                glu_w, glu_b, rpb):
    bsz, t, dm = a.shape
    mix_w = dm // 2
    a2 = a.reshape(bsz * t, dm).astype(BF16)
    u = matmul(a2, w_in[:, :mix_w]).reshape(bsz, t, mix_w)
    qkv = matmul(a2, w_in[:, mix_w:], out_dtype=BF16).reshape(bsz, t, 3 * mix_w)
    y = d_skip.astype(F32) * u + s5_scan(u, n_ctx, lam_re, lam_im, log_dt, b_re, b_im, c_re, c_im)
    gl = jax.nn.gelu(y)
    ya = gl * jax.nn.sigmoid(matmul(gl.reshape(bsz * t, mix_w), glu_w).reshape(bsz, t, mix_w)
                             + glu_b.astype(F32))
    yb = neighborhood_attention(qkv, rpb, n_ctx)
    return ya.astype(BF16), yb


def _rope_tables(n_ctx, seq, hd):
    half = hd // 2
    inv = ROPE_THETA ** (-jnp.arange(0, half, 2, dtype=F32) / half)
    pos = jnp.arange(seq)
    ang_r = (pos // GRID_W).astype(F32)[:, None] * inv[None, :]
    ang_c = (pos % GRID_W).astype(F32)[:, None] * inv[None, :]
    cos = jnp.concatenate([jnp.cos(ang_r), jnp.cos(ang_r), jnp.cos(ang_c), jnp.cos(ang_c)], axis=-1)
    sin = jnp.concatenate([-jnp.sin(ang_r), jnp.sin(ang_r), -jnp.sin(ang_c), jnp.sin(ang_c)], axis=-1)
    cos = jnp.concatenate([jnp.ones((n_ctx, hd), F32), cos], axis=0)
    sin = jnp.concatenate([jnp.zeros((n_ctx, hd), F32), sin], axis=0)
    return cos, sin


def _qk_prep_kernel(z_ref, cos_ref, sin_ref, gq_ref, gk_ref, o_ref, *, n_q_heads, hd):
    cosv = cos_ref[...]
    sinv = sin_ref[...]
    lane = lax.broadcasted_iota(jnp.int32, (1, hd), 1)
    first_quarter = (lane % (hd // 2)) < hd // 4
    for h in range(z_ref.shape[2] // hd):
        x = z_ref[0, :, h * hd:(h + 1) * hd]
        g = gq_ref[...] if h < n_q_heads else gk_ref[...]
        xn = x * lax.rsqrt(jnp.mean(x * x, axis=-1, keepdims=True) + NORM_EPS) * g
        partner = jnp.where(first_quarter, pltpu.roll(xn, hd - hd // 4, axis=1), pltpu.roll(xn, hd // 4, axis=1))
        o_ref[0, :, h * hd:(h + 1) * hd] = (xn * cosv + partner * sinv).astype(o_ref.dtype)


def qk_prepare(zqk, n_ctx, n_q_heads, q_norm, k_norm):
    bsz, t, width = zqk.shape
    hd = GQ_HEAD_DIM
    tm = _tile(t, 256, 16)
    cos, sin = _rope_tables(n_ctx, t - n_ctx, hd)
    rows = pl.BlockSpec((1, tm, width), lambda b, i: (b, i, 0))
    tab = pl.BlockSpec((tm, hd), lambda b, i: (i, 0))
    gain = pl.BlockSpec((1, hd), lambda b, i: (0, 0))
    return pl.pallas_call(
        functools.partial(_qk_prep_kernel, n_q_heads=n_q_heads, hd=hd),
        grid=(bsz, t // tm),
        in_specs=[rows, tab, tab, gain, gain],
        out_specs=rows,
        out_shape=jax.ShapeDtypeStruct((bsz, t, width), BF16),
        compiler_params=_params("parallel", "parallel"),
        name="qk_prepare",
    )(zqk, cos, sin, q_norm.astype(F32)[None, :], k_norm.astype(F32)[None, :])


def _split_bf16(x, terms):
    parts = []
    for _ in range(terms):
        p = x.astype(BF16)
        parts.append(p)
        x = x - p.astype(F32)
    return parts


def _head_sums(x, ones_ref):
    gw = ones_ref.shape[0]
    hi, lo = _split_bf16(x, 2)
    outs = []
    for g in range(x.shape[1] // gw):
        sl = slice(g * gw, (g + 1) * gw)
        outs.append(jnp.dot(hi[:, sl], ones_ref[...], preferred_element_type=F32)
                    + jnp.dot(lo[:, sl], ones_ref[...], preferred_element_type=F32))
    return jnp.concatenate(outs, axis=1)


def _rwkv_prep_kernel(z_ref, zp_ref, zn_ref, mu_ref, gup_ref, wup_ref, aup_ref, vec_ref, rk_ref, tri_ref,
                      ones_ref, at_ref, rt_ref, bt_ref, kt_ref, v_ref, pc_ref, gate_ref, bonus_ref,
                      *, width, tb, c, ncb, nb):
    i = pl.program_id(1)
    keep_prev = jnp.where((i == 0) | (i == ncb), 0.0, 1.0)
    keep_next = jnp.where((i == ncb - 1) | (i == nb - 1), 0.0, 1.0)
    z = z_ref[0]
    rid = lax.broadcasted_iota(jnp.int32, (tb, 1), 0)
    zp = jnp.where(rid == 0, zp_ref[0, 7:8, :] * keep_prev, pltpu.roll(z, 1, axis=0))
    zn = jnp.where(rid == tb - 1, zn_ref[0, 0:1, :] * keep_next, pltpu.roll(z, tb - 1, axis=0))
    zs = z + mu_ref[0:1, :] * (zp - z) + mu_ref[1:2, :] * (zn - z)
    w = width
    r, k, v = zs[:, :w], zs[:, w:2 * w], zs[:, 2 * w:3 * w]
    g_lo = zs[:, 3 * w:3 * w + RK_GATE_RANK]
    o = 3 * w + RK_GATE_RANK
    wl = jnp.tanh(zs[:, o:o + 2 * RK_DECAY_RANK]).astype(BF16)
    al = zs[:, o + 2 * RK_DECAY_RANK:o + 2 * RK_DECAY_RANK + 2 * RK_ICLR_RANK].astype(BF16)
    v_ref[0] = v.astype(v_ref.dtype)
    gate_ref[0] = jnp.dot(jax.nn.sigmoid(g_lo).astype(BF16), gup_ref[...], preferred_element_type=F32)
    ksum = jnp.zeros_like(k)
    for d in range(2):
        w0, a0 = vec_ref[d, 0:1, :], vec_ref[d, 1:2, :]
        k_k, k_a = vec_ref[d, 2:3, :], vec_ref[d, 3:4, :]
        neg = -(w0 + jnp.dot(wl, wup_ref[d], preferred_element_type=F32))
        softplus = jnp.maximum(neg, 0.0) + jnp.log(1.0 + jnp.exp(-jnp.abs(neg)))
        lw = -jnp.exp(-softplus - 0.5)
        iclr = jax.nn.sigmoid(a0 + jnp.dot(al, aup_ref[d], preferred_element_type=F32))
        kk = k * k_k
        kk = kk / jnp.maximum(jnp.sqrt(_head_sums(kk * kk, ones_ref)), 1e-12)
        k_d = k * (1.0 + (iclr - 1.0) * k_a)
        ksum = ksum + k_d
        cum = sum(jnp.dot(tri_ref[d], part, preferred_element_type=F32) for part in _split_bf16(lw, 3))
        p_inv = jnp.exp(-cum)
        at_ref[d, 0] = (-kk * jnp.exp(cum - lw)).astype(at_ref.dtype)
        rt_ref[d, 0] = (r * jnp.exp(cum)).astype(rt_ref.dtype)
        bt_ref[d, 0] = (kk * iclr * p_inv).astype(bt_ref.dtype)
        kt_ref[d, 0] = (k_d * p_inv).astype(kt_ref.dtype)
        for j in range(tb // c):
            row = j * c + (c - 1 if d == 0 else 0)
            pc_ref[d, 0, j] = jnp.exp(cum[row:row + 1, :])
    bonus_ref[0] = _head_sums(r * ksum * rk_ref[...], ones_ref) * v


def rwkv7_prepare(zr, n_ctx, mu, g_up, w0, w_up, a0, a_up, k_k, k_a, r_k):
    bsz, t, rk_in = zr.shape
    width = g_up.shape[1]
    c, tb = RK_CHUNK, RK_TBLOCK
    assert n_ctx % tb == 0 and t % tb == 0 and tb % c == 0 and tb % V7X_SUBLANES == 0
    ncb, nb = n_ctx // tb, t // tb
    rows_per_tile = V7X_SUBLANES

    def pad_rows(up, d, rank):
        return jnp.pad(up, ((d * rank, (1 - d) * rank), (0, 0)))

    wup = jnp.stack([pad_rows(w_up[d], d, RK_DECAY_RANK) for d in range(2)]).astype(BF16)
    aup = jnp.stack([pad_rows(a_up[d], d, RK_ICLR_RANK) for d in range(2)]).astype(BF16)
    vec = jnp.stack([w0, a0, k_k, k_a], axis=1).astype(F32)
    step = np.arange(tb)
    same = (step[:, None] // c) == (step[None, :] // c)
    tri = np.stack([same & (step[None, :] <= step[:, None]), same & (step[None, :] >= step[:, None])])
    seg = np.arange(V7X_MXU_DIM) // RK_HEAD
    ones = (seg[:, None] == seg[None, :])

    def full(shape):
        return pl.BlockSpec(shape, lambda b, i: (0,) * len(shape))

    per_dir = pl.BlockSpec((2, 1, tb, width), lambda b, i: (0, b, i, 0))
    rows = pl.BlockSpec((1, tb, width), lambda b, i: (b, i, 0))
    kern = functools.partial(_rwkv_prep_kernel, width=width, tb=tb, c=c, ncb=ncb, nb=nb)
    stream = jax.ShapeDtypeStruct((2, bsz, t, width), BF16)
    return pl.pallas_call(
        kern,
        grid=(bsz, nb),
        in_specs=[pl.BlockSpec((1, tb, rk_in), lambda b, i: (b, i, 0)),
                  pl.BlockSpec((1, rows_per_tile, rk_in),
                               lambda b, i: (b, jnp.maximum(i * (tb // rows_per_tile) - 1, 0), 0)),
                  pl.BlockSpec((1, rows_per_tile, rk_in),
                               lambda b, i: (b, jnp.minimum((i + 1) * (tb // rows_per_tile),
                                                            t // rows_per_tile - 1), 0)),
                  full((2, rk_in)), full((RK_GATE_RANK, width)), full((2, 2 * RK_DECAY_RANK, width)),
                  full((2, 2 * RK_ICLR_RANK, width)), full((2, 4, width)), full((1, width)),
                  full((2, tb, tb)), full((V7X_MXU_DIM, V7X_MXU_DIM))],
        out_specs=[per_dir, per_dir, per_dir, per_dir, rows,
                   pl.BlockSpec((2, 1, tb // c, 1, width), lambda b, i: (0, b, i, 0, 0)), rows, rows],
        out_shape=[stream, stream, stream, stream, jax.ShapeDtypeStruct((bsz, t, width), BF16),
                   jax.ShapeDtypeStruct((2, bsz, t // c, 1, width), F32),
                   jax.ShapeDtypeStruct((bsz, t, width), F32), jax.ShapeDtypeStruct((bsz, t, width), F32)],
        compiler_params=_params("parallel", "parallel"),
        name="rwkv7_prepare",
    )(zr, zr, zr, mu.astype(F32), g_up.astype(BF16), wup, aup, vec, r_k.astype(F32).reshape(1, width),
      jnp.asarray(tri, BF16), jnp.asarray(ones, BF16))


def _rwkv_finish_kernel(ys_ref, bonus_ref, gate_ref, lnw_ref, lnb_ref, ones_ref, o_ref):
    y = ys_ref[0, 0] + ys_ref[1, 0]
    inv_n = 1.0 / RK_HEAD
    mean = _head_sums(y, ones_ref) * inv_n
    dev = y - mean
    var = _head_sums(dev * dev, ones_ref) * inv_n
    yn = dev * lax.rsqrt(var + RK_GN_EPS) * lnw_ref[...] + lnb_ref[...]
    o_ref[0] = ((yn + bonus_ref[0]) * gate_ref[0]).astype(o_ref.dtype)


def rwkv7_finish(ys, bonus, gate, ln_w, ln_b):
    _, bsz, t, width = ys.shape
    tb = RK_TBLOCK
    seg = np.arange(V7X_MXU_DIM) // RK_HEAD
    ones = (seg[:, None] == seg[None, :])
    rows = pl.BlockSpec((1, tb, width), lambda b, i: (b, i, 0))
    vec = pl.BlockSpec((1, width), lambda b, i: (0, 0))
    return pl.pallas_call(
        _rwkv_finish_kernel,
        grid=(bsz, t // tb),
        in_specs=[pl.BlockSpec((2, 1, tb, width), lambda b, i: (0, b, i, 0)), rows, rows, vec, vec,
                  pl.BlockSpec((V7X_MXU_DIM, V7X_MXU_DIM), lambda b, i: (0, 0))],
        out_specs=rows,
        out_shape=jax.ShapeDtypeStruct((bsz, t, width), BF16),
        compiler_params=_params("parallel", "parallel"),
        name="rwkv7_finish",
    )(ys, bonus, gate, ln_w.astype(F32)[None, :], ln_b.astype(F32)[None, :], jnp.asarray(ones, BF16))


def _odd_mixer(a, n_ctx, w_in, mu, g_up, w0, w_up, a0, a_up, k_k, k_a, r_k, ln_w, ln_b,
               q_norm, k_norm):
    bsz, t, dm = a.shape
    mix_w = dm // 2
    heads = mix_w // RK_HEAD
    rk_in = 3 * mix_w + RK_GATE_RANK + 2 * RK_DECAY_RANK + 2 * RK_ICLR_RANK
    gq_heads = mix_w // GQ_HEAD_DIM
    kv_heads = gq_heads // 4
    kv_w = kv_heads * GQ_HEAD_DIM
    a2 = a.reshape(bsz * t, dm).astype(BF16)
    zr = matmul(a2, w_in[:, :rk_in]).reshape(bsz, t, rk_in)
    zqk = matmul(a2, w_in[:, rk_in:rk_in + mix_w + kv_w]).reshape(bsz, t, mix_w + kv_w)
    vv = matmul(a2, w_in[:, rk_in + mix_w + kv_w:], out_dtype=BF16).reshape(bsz, t, kv_w)

    at, rt, bt, kt, v, pc, gate, bonus = rwkv7_prepare(zr, n_ctx, mu, g_up, w0, w_up, a0, a_up, k_k, k_a,
                                                       r_k.reshape(-1))
    ys = rwkv7_scan(at, rt, bt, kt, v, pc, n_ctx)
    yc = rwkv7_finish(ys, bonus, gate, ln_w, ln_b)
    qk = qk_prepare(zqk, n_ctx, gq_heads, q_norm, k_norm)
    yd = gqa_attention(qk, vv, n_ctx, gq_heads, kv_heads)
    return yc, yd


def kernel(x, c, ctx, c_ctx, w_mod, b_mod, norm1, norm2, final_norm, ev_w_in, ev_w_out, s5_lam_re, s5_lam_im, s5_log_dt, s5_b_re, s5_b_im, s5_c_re, s5_c_im, s5_d, s5_glu_w, s5_glu_b, na_rpb, od_w_in, od_w_out, rk_mu, rk_g_up, rk_w0, rk_w_up, rk_a0, rk_a_up, rk_k_k, rk_k_a, rk_r_k, rk_ln_w, rk_ln_b, gq_q_norm, gq_k_norm, moe_w_router, moe_b_router, moe_w1, moe_b1, moe_w2, moe_b2):
    bsz, seq, dm = x.shape
    n_ctx = ctx.shape[1]
    depth = w_mod.shape[0]
    t = n_ctx + seq
    h = jnp.concatenate([ctx, x], axis=1).astype(F32)
    cond = jnp.concatenate([jax.nn.silu(c), jax.nn.silu(c_ctx)[None, :]], axis=0)
    for i in range(depth):
        last = i == depth - 1
        j = i // 2
        mod = matmul(cond, w_mod[i]) + b_mod[i].astype(F32)
        sh1, sc1, g1, sh2, sc2, g2 = jnp.split(mod, N_MOD, axis=-1)
        a = _modulated(h, n_ctx, norm1[i], sc1[bsz], sh1[bsz], sc1[:bsz], sh1[:bsz])
        if i % 2 == 0:
            ya, yb = _even_mixer(a, n_ctx, ev_w_in[j], s5_lam_re[j], s5_lam_im[j], s5_log_dt[j],
                                 s5_b_re[j], s5_b_im[j], s5_c_re[j], s5_c_im[j], s5_d[j], s5_glu_w[j],
                                 s5_glu_b[j], na_rpb[j])
            w_out = ev_w_out[j]
        else:
            ya, yb = _odd_mixer(a, n_ctx, od_w_in[j], rk_mu[j], rk_g_up[j], rk_w0[j], rk_w_up[j],
                                rk_a0[j], rk_a_up[j], rk_k_k[j], rk_k_a[j], rk_r_k[j], rk_ln_w[j], rk_ln_b[j],
                                gq_q_norm[j], gq_k_norm[j])
            w_out = od_w_out[j]
        h, f = out_proj_residual_norm(ya, yb, w_out, h, n_ctx, g1, norm2[i], sc2, sh2)
        moe_args = (moe_w_router[i], moe_b_router[i], moe_w1[i], moe_b1[i], moe_w2[i], moe_b2[i])
        if last:
            y = moe_ffn(f[:, n_ctx:].reshape(bsz * seq, dm), *moe_args).reshape(bsz, seq, dm)
            hx = h[:, n_ctx:] + g2[:bsz][:, None, :] * y
            return _rms(hx, final_norm).astype(x.dtype)
        y = moe_ffn(f.reshape(bsz * t, dm), *moe_args).reshape(bsz, t, dm)
        h = _gated_add(h, n_ctx, y, g2[bsz], g2[:bsz])
```

```python
import functools
import math

import numpy as np
import jax
import jax.numpy as jnp
from jax import lax
from jax.experimental import pallas as pl
from jax.experimental.pallas import tpu as pltpu

F32 = jnp.float32
BF16 = jnp.bfloat16

V7X_LANES = 128
V7X_SUBLANES = 8
V7X_MXU_DIM = 256
V7X_VMEM_BYTES = 64 * 1024 * 1024
VMEM_LIMIT = 56 * 1024 * 1024

GRID_W = 64
N_MOD = 6
NORM_EPS = 1e-6
S5_GROUP = 16
S5_STATE = 64
S5_CHUNK = 16
NA_HEADS = 8
WIN_ROWS = 8
WIN_COLS = 16
NA_QROWS = 4
NA_KROWS = NA_QROWS + WIN_ROWS - 1
RK_HEAD = 64
RK_DECAY_RANK = 64
RK_ICLR_RANK = 64
RK_GATE_RANK = 128
RK_GN_EPS = 64e-5
RK_CHUNK = 64
RK_TBLOCK = 256
GQ_HEAD_DIM = 128
ROPE_THETA = 10000.0
GQ_TQ = 128
GQ_TK = 2048
N_EXPERTS = 32
TOP_K = 4
SWIGLU_ALPHA = 1.702
SWIGLU_LIMIT = 7.0
MOE_TM = 256
NEG_BIG = -1e30
HI = lax.Precision.HIGHEST


def _params(*sem):
    return pltpu.CompilerParams(dimension_semantics=sem, vmem_limit_bytes=VMEM_LIMIT)


def _tile(n, target, align):
    best = None
    t = align
    while t <= min(n, target):
        if n % t == 0:
            best = t
        t += align
    return n if best is None else best


def _mm_kernel(x_ref, w_ref, o_ref):
    o_ref[...] = jnp.dot(x_ref[...], w_ref[...], preferred_element_type=F32).astype(o_ref.dtype)


def matmul(x, w, out_dtype=F32):
    m, k = x.shape
    n = w.shape[1]
    x = x.astype(BF16)
    w = w.astype(BF16)
    m_pad = -(-m // 16) * 16
    if m_pad != m:
        x = jnp.pad(x, ((0, m_pad - m), (0, 0)))
    tm = _tile(m_pad, 512, 16)
    tn = _tile(n, 2048, V7X_LANES)
    out = pl.pallas_call(
        _mm_kernel,
        grid=(n // tn, m_pad // tm),
        in_specs=[pl.BlockSpec((tm, k), lambda j, i: (i, 0)),
                  pl.BlockSpec((k, tn), lambda j, i: (0, j))],
        out_specs=pl.BlockSpec((tm, tn), lambda j, i: (i, j)),
        out_shape=jax.ShapeDtypeStruct((m_pad, n), out_dtype),
        compiler_params=_params("parallel", "parallel"),
        name="matmul",
    )(x, w)
    return out[:m] if m_pad != m else out


def _out_proj_kernel(xa_ref, xb_ref, wa_ref, wb_ref, h_ref, g_ref, nw_ref, sc_ref, sh_ref, hn_ref, f_ref):
    out = (jnp.dot(xa_ref[...], wa_ref[...], preferred_element_type=F32)
           + jnp.dot(xb_ref[...], wb_ref[...], preferred_element_type=F32))
    hn = h_ref[...] + g_ref[0] * out
    hn_ref[...] = hn
    normed = hn * lax.rsqrt(jnp.mean(hn * hn, axis=-1, keepdims=True) + NORM_EPS) * nw_ref[...]
    f_ref[...] = (normed * (1.0 + sc_ref[0]) + sh_ref[0]).astype(f_ref.dtype)


def out_proj_residual_norm(xa, xb, w_out, h, n_ctx, gate, norm_w, scale, shift):
    bsz, t, dm = h.shape
    wa = xa.shape[-1]
    tm = _tile(n_ctx, 256, 16)
    assert t % tm == 0
    nbt, ncb = t // tm, n_ctx // tm

    def mod_row(i):
        return jnp.where(i % nbt < ncb, bsz, i // nbt)

    rows = pl.BlockSpec((tm, dm), lambda i: (i, 0))
    mod = pl.BlockSpec((1, 1, dm), lambda i: (mod_row(i), 0, 0))
    m = bsz * t
    w_out = w_out.astype(BF16)
    hn, f = pl.pallas_call(
        _out_proj_kernel,
        grid=(m // tm,),
        in_specs=[pl.BlockSpec((tm, wa), lambda i: (i, 0)),
                  pl.BlockSpec((tm, dm - wa), lambda i: (i, 0)),
                  pl.BlockSpec((wa, dm), lambda i: (0, 0)),
                  pl.BlockSpec((dm - wa, dm), lambda i: (0, 0)),
                  rows, mod, pl.BlockSpec((1, dm), lambda i: (0, 0)), mod, mod],
        out_specs=[rows, rows],
        out_shape=[jax.ShapeDtypeStruct((m, dm), F32), jax.ShapeDtypeStruct((m, dm), BF16)],
        compiler_params=_params("parallel"),
        name="out_proj_residual_norm",
    )(xa.reshape(m, wa), xb.reshape(m, dm - wa), w_out[:wa], w_out[wa:], h.reshape(m, dm),
      gate[:, None, :], norm_w.astype(F32)[None, :], scale[:, None, :], shift[:, None, :])
    return hn.reshape(bsz, t, dm), f.reshape(bsz, t, dm)


def _flash_kernel(q_ref, k_ref, v_ref, o_ref, m_sc, l_sc, acc_sc, *, grp, hd, tq, tk, n_ctx,
                  n_ctx_qblocks, n_lat_chunks):
    qi = pl.program_id(2)
    q = jnp.concatenate([q_ref[0, :, g * hd:(g + 1) * hd] for g in range(grp)], axis=0)

    scale = hd ** -0.5
    s = lax.dot_general(q, k_ref[0, 0:n_ctx, :], (((1,), (1,)), ((), ())),
                        preferred_element_type=F32) * scale
    m0 = jnp.max(s, axis=-1, keepdims=True)
    p = jnp.exp(s - m0)
    m_sc[...] = m0
    l_sc[...] = jnp.sum(p, axis=-1, keepdims=True)
    acc_sc[...] = jnp.dot(p.astype(BF16), v_ref[0, 0:n_ctx, :], preferred_element_type=F32)

    def body(j, carry):
        off = pl.multiple_of(n_ctx + j * tk, V7X_LANES)
        k = k_ref[0, pl.ds(off, tk), :]
        v = v_ref[0, pl.ds(off, tk), :]
        s = lax.dot_general(q, k, (((1,), (1,)), ((), ())), preferred_element_type=F32) * scale
        m_prev = m_sc[...]
        m_new = jnp.maximum(m_prev, jnp.max(s, axis=-1, keepdims=True))
        alpha = jnp.exp(m_prev - m_new)
        p = jnp.exp(s - m_new)
        l_sc[...] = alpha * l_sc[...] + jnp.sum(p, axis=-1, keepdims=True)
        acc_sc[...] = alpha * acc_sc[...] + jnp.dot(p.astype(BF16), v, preferred_element_type=F32)
        m_sc[...] = m_new
        return carry

    lax.fori_loop(0, jnp.where(qi < n_ctx_qblocks, 0, n_lat_chunks), body, 0)
    o = acc_sc[...] / l_sc[...]
    for g in range(grp):
        o_ref[0, :, g * hd:(g + 1) * hd] = o[g * tq:(g + 1) * tq].astype(o_ref.dtype)


def gqa_attention(qk, v, n_ctx, n_q_heads, n_kv_heads):
    b, t, _ = qk.shape
    hd = GQ_HEAD_DIM
    qw = n_q_heads * hd
    grp = n_q_heads // n_kv_heads
    tq = GQ_TQ
    tk = _tile(t - n_ctx, GQ_TK, V7X_LANES)
    assert n_ctx % tq == 0 and t % tq == 0 and n_ctx % V7X_LANES == 0
    kern = functools.partial(_flash_kernel, grp=grp, hd=hd, tq=tq, tk=tk, n_ctx=n_ctx,
                             n_ctx_qblocks=n_ctx // tq, n_lat_chunks=(t - n_ctx) // tk)
    rows = grp * tq
    return pl.pallas_call(
        kern,
        grid=(b, n_kv_heads, t // tq),
        in_specs=[pl.BlockSpec((1, tq, grp * hd), lambda bi, h, i: (bi, i, h)),
                  pl.BlockSpec((1, t, hd), lambda bi, h, i: (bi, 0, n_q_heads + h)),
                  pl.BlockSpec((1, t, hd), lambda bi, h, i: (bi, 0, h))],
        out_specs=pl.BlockSpec((1, tq, grp * hd), lambda bi, h, i: (bi, i, h)),
        out_shape=jax.ShapeDtypeStruct((b, t, qw), BF16),
        scratch_shapes=[pltpu.VMEM((rows, 1), F32), pltpu.VMEM((rows, 1), F32),
                        pltpu.VMEM((rows, hd), F32)],
        compiler_params=_params("parallel", "parallel", "parallel"),
        name="gqa_attention",
    )(qk, qk, v)


def _na_kernel(q_ref, k_ref, v_ref, bias_ref, o_ref, *, n_ctx, rows, n_blocks):
    j = pl.program_id(2)
    q = q_ref[0]
    kc = k_ref[0, 0:n_ctx, :]
    vc = v_ref[0, 0:n_ctx, :]
    dn = (((1,), (1,)), ((), ()))
    scale = q.shape[-1] ** -0.5
    s_ctx = lax.dot_general(q, kc, dn, preferred_element_type=F32) * scale

    @pl.when(j == 0)
    def _():
        m = jnp.max(s_ctx, axis=-1, keepdims=True)
        p = jnp.exp(s_ctx - m)
        l = jnp.sum(p, axis=-1, keepdims=True)
        o = jnp.dot(p.astype(BF16), vc, preferred_element_type=F32) / l
        o_ref[0] = o.astype(o_ref.dtype)

    @pl.when(j > 0)
    def _():
        jj = j - 1
        ks = jnp.clip(NA_QROWS * jj - WIN_ROWS // 2, 0, rows - NA_KROWS)
        off = pl.multiple_of(n_ctx + ks * GRID_W, GRID_W)
        kw = k_ref[0, pl.ds(off, NA_KROWS * GRID_W), :]
        vw = v_ref[0, pl.ds(off, NA_KROWS * GRID_W), :]
        pat = jnp.where(jj == 0, 0, jnp.where(jj == n_blocks - 1, 2, 1))
        s_win = lax.dot_general(q, kw, dn, preferred_element_type=F32) * scale + bias_ref[pat]
        m = jnp.maximum(jnp.max(s_win, axis=-1, keepdims=True), jnp.max(s_ctx, axis=-1, keepdims=True))
        p_win = jnp.exp(s_win - m)
        p_ctx = jnp.exp(s_ctx - m)
        l = jnp.sum(p_win, axis=-1, keepdims=True) + jnp.sum(p_ctx, axis=-1, keepdims=True)
        o = (jnp.dot(p_win.astype(BF16), vw, preferred_element_type=F32)
             + jnp.dot(p_ctx.astype(BF16), vc, preferred_element_type=F32)) / l
        o_ref[0] = o.astype(o_ref.dtype)


def _na_bias(rpb, rows):
    n_blocks = rows // NA_QROWS
    col = np.arange(GRID_W)
    col_start = np.clip(col - WIN_COLS // 2, 0, GRID_W - WIN_COLS)
    col_ok = (col[None, :] >= col_start[:, None]) & (col[None, :] < col_start[:, None] + WIN_COLS)
    dc = np.clip(col[None, :] - col[:, None] + (WIN_COLS - 1), 0, 2 * WIN_COLS - 2)
    pick_c = (dc[:, :, None] == np.arange(2 * WIN_COLS - 1)).astype(np.float32)
    pats = []
    for jj in (0, 1, n_blocks - 1):
        ks = min(max(NA_QROWS * jj - WIN_ROWS // 2, 0), rows - NA_KROWS)
        r = NA_QROWS * jj + np.arange(NA_QROWS)
        kr = ks + np.arange(NA_KROWS)
        r_start = np.clip(r - WIN_ROWS // 2, 0, rows - WIN_ROWS)
        row_ok = (kr[None, :] >= r_start[:, None]) & (kr[None, :] < r_start[:, None] + WIN_ROWS)
        dr = np.clip(kr[None, :] - r[:, None] + (WIN_ROWS - 1), 0, 2 * WIN_ROWS - 2)
        pick_r = (dr[:, :, None] == np.arange(2 * WIN_ROWS - 1)).astype(np.float32)
        bias = jnp.einsum('hab,rka,qcb->hrqkc', rpb.astype(F32), pick_r, pick_c, precision=HI)
        ok = row_ok[:, None, :, None] & col_ok[None, :, None, :]
        bias = jnp.where(ok[None], bias, NEG_BIG)
        pats.append(bias.reshape(rpb.shape[0], NA_QROWS * GRID_W, NA_KROWS * GRID_W))
    return jnp.stack(pats, axis=0)


def neighborhood_attention(qkv, rpb, n_ctx):
    b, t, width3 = qkv.shape
    width = width3 // 3
    hd = width // NA_HEADS
    seq = t - n_ctx
    rows = seq // GRID_W
    qb = NA_QROWS * GRID_W
    assert n_ctx == qb and rows % NA_QROWS == 0 and rows >= NA_KROWS and WIN_ROWS <= rows
    n_blocks = rows // NA_QROWS
    bias = _na_bias(rpb, rows)
    kern = functools.partial(_na_kernel, n_ctx=n_ctx, rows=rows, n_blocks=n_blocks)
    kb = NA_KROWS * GRID_W
    return pl.pallas_call(
        kern,
        grid=(NA_HEADS, b, n_blocks + 1),
        in_specs=[pl.BlockSpec((1, qb, hd), lambda h, bi, j: (bi, j, h)),
                  pl.BlockSpec((1, t, hd), lambda h, bi, j: (bi, 0, NA_HEADS + h)),
                  pl.BlockSpec((1, t, hd), lambda h, bi, j: (bi, 0, 2 * NA_HEADS + h)),
                  pl.BlockSpec((3, None, qb, kb), lambda h, bi, j: (0, h, 0, 0))],
        out_specs=pl.BlockSpec((1, qb, hd), lambda h, bi, j: (bi, j, h)),
        out_shape=jax.ShapeDtypeStruct((b, t, width), BF16),
        compiler_params=_params("parallel", "parallel", "parallel"),
        name="neighborhood_attention",
    )(qkv, qkv, qkv, bias)


def _s5_kernel(u_ref, t_ref, n_ref, m_ref, d_ref, y_ref, hl_sc, hp_sc, *, nc_ctx, nc_all, bsz):
    u = u_ref[0]
    for part in range(4):
        hl_sc[part] = jnp.dot(u, n_ref[0, part], preferred_element_type=F32)
    zero = jnp.zeros((bsz, S5_STATE), F32)

    def scan(direction):
        dr = d_ref[0, 2 * direction:2 * direction + 1, :]
        di = d_ref[0, 2 * direction + 1:2 * direction + 2, :]

        def body(step, carry):
            hr, hi = carry
            if direction == 0:
                c = step
            else:
                c = jnp.where(step < nc_ctx, nc_ctx - 1 - step, nc_all - 1 - (step - nc_ctx))
            r0 = pl.multiple_of(c * bsz, bsz)
            hp_sc[2 * direction, pl.ds(r0, bsz), :] = hr
            hp_sc[2 * direction + 1, pl.ds(r0, bsz), :] = hi
            lr = hl_sc[2 * direction, pl.ds(r0, bsz), :]
            li = hl_sc[2 * direction + 1, pl.ds(r0, bsz), :]
            return dr * hr - di * hi + lr, dr * hi + di * hr + li

        lax.fori_loop(0, nc_all, body, (zero, zero))

    scan(0)
    scan(1)
    y = jnp.dot(u, t_ref[0], preferred_element_type=F32)
    for part in range(4):
        y = y + jnp.dot(hp_sc[part].astype(BF16), m_ref[0, part], preferred_element_type=F32)
    y_ref[0] = y


def _s5_tables(lam_re, lam_im, log_dt, b_re, b_im, c_re, c_im):
    c = S5_CHUNK
    g, p = lam_re.shape[1], lam_re.shape[2]
    h = b_re.shape[-1]
    toe = 0.0
    n_parts, m_parts, d_parts = [], [], []
    tt = jnp.arange(c)
    for d in range(2):
        lr, li = lam_re[d].astype(F32), lam_im[d].astype(F32)
        dt = jnp.exp(log_dt[d].astype(F32))[:, None]

        def lam_pow(e):
            e = jnp.asarray(e, F32)[..., None, None]
            mag = jnp.exp(e * lr * dt)
            return mag * jnp.cos(e * li * dt), mag * jnp.sin(e * li * dt)

        lbr, lbi = lam_pow(1)
        nr, ni = lbr - 1.0, lbi
        den = lr * lr + li * li
        fr, fi = (nr * lr + ni * li) / den, (ni * lr - nr * li) / den
        br, bi = b_re[d].astype(F32), b_im[d].astype(F32)
        bbr = fr[..., None] * br - fi[..., None] * bi
        bbi = fr[..., None] * bi + fi[..., None] * br
        cr, ci = c_re[d].astype(F32), c_im[d].astype(F32)
        pr, pi = lam_pow(tt)
        lbbr = pr[..., None] * bbr - pi[..., None] * bbi
        lbbi = pr[..., None] * bbi + pi[..., None] * bbr
        kern = (jnp.einsum('gop,tgpi->tgoi', cr, lbbr, precision=HI)
                - jnp.einsum('gop,tgpi->tgoi', ci, lbbi, precision=HI))
        steps = np.arange(c)
        lag = (steps[None, :] - steps[:, None]) if d == 0 else (steps[:, None] - steps[None, :])
        pick = (lag[:, :, None] == steps).astype(np.float32)
        kt = jnp.einsum('stx,xgoi->gsito', pick, kern, precision=HI)
        toe = toe + kt.reshape(g, c * h, c * h)
        e_in = (c - 1 - tt) if d == 0 else tt
        qr, qi = lam_pow(e_in)
        n_r = qr[..., None] * bbr - qi[..., None] * bbi
        n_i = qr[..., None] * bbi + qi[..., None] * bbr
        n_parts += [jnp.transpose(n_r, (1, 0, 3, 2)).reshape(g, c * h, p),
                    jnp.transpose(n_i, (1, 0, 3, 2)).reshape(g, c * h, p)]
        e_out = (tt + 1) if d == 0 else (c - tt)
        sr, si = lam_pow(e_out)
        clr = cr[None] * sr[:, :, None, :] - ci[None] * si[:, :, None, :]
        cli = cr[None] * si[:, :, None, :] + ci[None] * sr[:, :, None, :]
        m_parts += [jnp.transpose(clr, (1, 3, 0, 2)).reshape(g, p, c * h),
                    -jnp.transpose(cli, (1, 3, 0, 2)).reshape(g, p, c * h)]
        dcr, dci = lam_pow(c)
        d_parts += [dcr, dci]
    return (toe.astype(BF16), jnp.stack(n_parts, 1).astype(BF16), jnp.stack(m_parts, 1).astype(BF16),
            jnp.stack(d_parts, 1))


def s5_scan(u, n_ctx, lam_re, lam_im, log_dt, b_re, b_im, c_re, c_im):
    bsz, t, width = u.shape
    g = width // S5_GROUP
    c = S5_CHUNK
    assert bsz == V7X_SUBLANES and t % c == 0 and n_ctx % c == 0
    nc = t // c
    rows = nc * bsz
    toe, n_tab, m_tab, d_tab = _s5_tables(lam_re, lam_im, log_dt, b_re, b_im, c_re, c_im)
    uc = u.astype(BF16).reshape(bsz, nc, c, g, S5_GROUP)
    uc = jnp.transpose(uc, (3, 1, 0, 2, 4)).reshape(g, rows, c * S5_GROUP)
    kern = functools.partial(_s5_kernel, nc_ctx=n_ctx // c, nc_all=nc, bsz=bsz)
    cw = c * S5_GROUP
    y = pl.pallas_call(
        kern,
        grid=(g,),
        in_specs=[pl.BlockSpec((1, rows, cw), lambda i: (i, 0, 0)),
                  pl.BlockSpec((1, cw, cw), lambda i: (i, 0, 0)),
                  pl.BlockSpec((1, 4, cw, S5_STATE), lambda i: (i, 0, 0, 0)),
                  pl.BlockSpec((1, 4, S5_STATE, cw), lambda i: (i, 0, 0, 0)),
                  pl.BlockSpec((1, 4, S5_STATE), lambda i: (i, 0, 0))],
        out_specs=pl.BlockSpec((1, rows, cw), lambda i: (i, 0, 0)),
        out_shape=jax.ShapeDtypeStruct((g, rows, cw), F32),
        scratch_shapes=[pltpu.VMEM((4, rows, S5_STATE), F32), pltpu.VMEM((4, rows, S5_STATE), F32)],
        compiler_params=_params("parallel"),
        name="s5_scan",
    )(uc, toe, n_tab, m_tab, d_tab)
    y = y.reshape(g, nc, bsz, c, S5_GROUP)
    return jnp.transpose(y, (2, 1, 3, 0, 4)).reshape(bsz, t, width)


def _rwkv_kernel(at_ref, rt_ref, bt_ref, kt_ref, v_ref, pc_ref, y_ref, s_sc, *, heads, c):
    n = pl.program_id(2)

    @pl.when(n == 0)
    def _():
        s_sc[...] = jnp.zeros(s_sc.shape, F32)

    row = lax.broadcasted_iota(jnp.int32, (c, c), 0)
    col = lax.broadcasted_iota(jnp.int32, (c, c), 1)
    ahead = (row - col) * jnp.where(pl.program_id(0) == 0, 1, -1)
    strict = ahead > 0
    incl = ahead >= 0
    nt = (((1,), (1,)), ((), ()))
    tn = (((0,), (0,)), ((), ()))
    n_factors = int(math.log2(c))
    hs = range(heads)
    sls = [slice(j * RK_HEAD, (j + 1) * RK_HEAD) for j in hs]

    def dot(a, b, dims=None):
        a, b = a.astype(BF16), b.astype(BF16)
        if dims is None:
            return jnp.dot(a, b, preferred_element_type=F32)
        return lax.dot_general(a, b, dims, preferred_element_type=F32)

    vv = [v_ref[0, :, sl] for sl in sls]
    s0 = [s_sc[j] for j in hs]
    ar = [jnp.concatenate([at_ref[0, 0, :, sl], rt_ref[0, 0, :, sl]], axis=0) for sl in sls]
    bk = [jnp.concatenate([bt_ref[0, 0, :, sl], kt_ref[0, 0, :, sl]], axis=0) for sl in sls]
    g = [dot(ar[j], bk[j], nt) for j in hs]
    ah = [dot(ar[j], s0[j], nt) for j in hs]
    x = [jnp.where(strict, g[j][:c, :c], 0.0) for j in hs]
    a_ak = [jnp.where(strict, g[j][:c, c:], 0.0) for j in hs]
    m = [jnp.concatenate([jnp.where(incl, g[j][c:, :c], 0.0), jnp.where(incl, g[j][c:, c:], 0.0)], axis=1)
         for j in hs]
    u = [ah[j][:c] + dot(a_ak[j], vv[j]) for j in hs]
    u = [u[j] + dot(x[j], u[j]) for j in hs]
    for _ in range(n_factors - 1):
        x = [dot(x[j], x[j]) for j in hs]
        u = [u[j] + dot(x[j], u[j]) for j in hs]
    uv = [jnp.concatenate([u[j].astype(BF16), vv[j]], axis=0) for j in hs]
    for j in hs:
        y_ref[0, 0, :, sls[j]] = ah[j][c:] + dot(m[j], uv[j])
    for j in hs:
        s_new = s0[j] + dot(uv[j], bk[j], tn)
        s_sc[j] = s_new * pc_ref[0, 0, 0, :, sls[j]]


def rwkv7_scan(at, rt, bt, kt, v, pc, n_ctx):
    _, bsz, t, width = at.shape
    heads = width // RK_HEAD
    c = RK_CHUNK
    assert t % c == 0 and n_ctx % c == 0
    ncc, nc = n_ctx // c, t // c

    def chunk(d, n):
        back = jnp.where(n < ncc, ncc - 1 - n, ncc + (nc - 1 - n))
        return jnp.where(d == 0, n, back)

    stream = pl.BlockSpec((1, 1, c, width), lambda d, bi, n: (d, bi, chunk(d, n), 0))
    kern = functools.partial(_rwkv_kernel, heads=heads, c=c)
    return pl.pallas_call(
        kern,
        grid=(2, bsz, nc),
        in_specs=[stream, stream, stream, stream,
                  pl.BlockSpec((1, c, width), lambda d, bi, n: (bi, chunk(d, n), 0)),
                  pl.BlockSpec((1, 1, 1, 1, width), lambda d, bi, n: (d, bi, chunk(d, n), 0, 0))],
        out_specs=stream,
        out_shape=jax.ShapeDtypeStruct((2, bsz, t, width), F32),
        scratch_shapes=[pltpu.VMEM((heads, RK_HEAD, RK_HEAD), F32)],
        compiler_params=_params("parallel", "parallel", "arbitrary"),
        name="rwkv7_scan",
    )(at, rt, bt, kt, v, pc)


def _deinterleave_kernel(w_ref, p_ref, o_ref):
    o_ref[...] = jnp.dot(w_ref[...].astype(BF16), p_ref[...], preferred_element_type=F32).astype(o_ref.dtype)


def deinterleave_columns(w_all, layer):
    _, r, n2 = w_all.shape
    perm = np.concatenate([np.arange(0, n2, 2), np.arange(1, n2, 2)])
    p = jnp.asarray(np.arange(n2)[:, None] == perm[None, :], BF16)
    tm = _tile(r, 1024, 16)
    w = w_all.reshape(-1, n2)
    first = layer * (r // tm)
    return pl.pallas_call(
        _deinterleave_kernel,
        grid=(r // tm,),
        in_specs=[pl.BlockSpec((tm, n2), lambda i: (first + i, 0)), pl.BlockSpec((n2, n2), lambda i: (0, 0))],
        out_specs=pl.BlockSpec((tm, n2), lambda i: (i, 0)),
        out_shape=jax.ShapeDtypeStruct((r, n2), BF16),
        compiler_params=_params("parallel"),
        name="deinterleave_columns",
    )(w, p)


def _moe_kernel(be_ref, nb_ref, x_ref, g_ref, w1_ref, b1g_ref, b1l_ref, w2_ref, b2_ref, o_ref):
    i = pl.program_id(0)
    de = w2_ref.shape[1]

    @pl.when(i < nb_ref[0])
    def _():
        x = x_ref[...]
        z = jnp.dot(x, w1_ref[0], preferred_element_type=F32)
        zg = z[:, :de] + b1g_ref[0]
        zl = z[:, de:] + b1l_ref[0]
        zg = jnp.minimum(zg, SWIGLU_LIMIT)
        zl = jnp.clip(zl, -SWIGLU_LIMIT, SWIGLU_LIMIT)
        act = zg * jax.nn.sigmoid(SWIGLU_ALPHA * zg) * (zl + 1.0)
        y = jnp.dot(act.astype(BF16), w2_ref[0], preferred_element_type=F32) + b2_ref[0]
        o_ref[...] = y * g_ref[...]

    @pl.when(i >= nb_ref[0])
    def _():
        o_ref[...] = jnp.zeros(o_ref.shape, F32)


def moe_ffn(h, w_router, b_router, w1_all, layer, b1, w2, b2):
    n, dm = h.shape
    n_exp = w_router.shape[1]
    de = w2.shape[1]
    tm = MOE_TM
    logits = jnp.dot(h, w_router, precision=HI, preferred_element_type=F32) + b_router.astype(F32)
    top_val, top_idx = lax.top_k(logits, TOP_K)
    gates = jax.nn.softmax(top_val, axis=-1)
    flat_e = top_idx.reshape(-1).astype(jnp.int32)
    rb = V7X_LANES
    assert (n * TOP_K) % rb == 0
    onehot = (flat_e[:, None] == jnp.arange(n_exp, dtype=jnp.int32)[None, :]).astype(F32)
    blocks = onehot.reshape(-1, rb, n_exp)
    tri = np.tril(np.ones((rb, rb), np.float32))
    within = jnp.einsum('ij,bjk->bik', tri, blocks, precision=HI)
    totals = within[:, -1, :]
    before = jnp.cumsum(totals, axis=0) - totals
    csum = within + before[:, None, :]
    rank = (jnp.sum(csum * blocks, axis=-1).reshape(-1) - 1.0).astype(jnp.int32)
    counts = (before[-1] + totals[-1]).astype(jnp.int32)
    padded = (counts + tm - 1) // tm * tm
    pad_end = jnp.cumsum(padded)
    pad_start = pad_end - padded
    dest = pad_start[flat_e] + rank
    n_blocks = -(-(n * TOP_K) // tm) + n_exp
    cap = n_blocks * tm
    slot_src = jnp.full((cap,), -1, jnp.int32).at[dest].set(jnp.arange(n * TOP_K, dtype=jnp.int32))
    filled = slot_src >= 0
    slot_tok = jnp.where(filled, slot_src // TOP_K, n)
    slot_gate = jnp.where(filled, gates.reshape(-1)[jnp.maximum(slot_src, 0)], 0.0)
    block_start = jnp.arange(n_blocks, dtype=jnp.int32) * tm
    block_exp = jnp.minimum(jnp.sum(pad_end[None, :] <= block_start[:, None], axis=1), n_exp - 1).astype(jnp.int32)
    n_used = (pad_end[-1] // tm).astype(jnp.int32).reshape(1)
    h_pad = jnp.concatenate([h.astype(BF16), jnp.zeros((1, dm), BF16)], axis=0)
    x_sorted = h_pad[slot_tok]

    w1p = deinterleave_columns(w1_all.reshape(-1, n_exp * dm, 2 * de), layer).reshape(n_exp, dm, 2 * de)
    b1g = b1[:, None, 0::2].astype(F32)
    b1l = b1[:, None, 1::2].astype(F32)
    w2b = w2.astype(BF16)
    b2r = b2[:, None, :].astype(F32)

    grid_spec = pltpu.PrefetchScalarGridSpec(
        num_scalar_prefetch=2,
        grid=(n_blocks,),
        in_specs=[pl.BlockSpec((tm, dm), lambda i, be, nb: (i, 0)),
                  pl.BlockSpec((tm, 1), lambda i, be, nb: (i, 0)),
                  pl.BlockSpec((1, dm, 2 * de), lambda i, be, nb: (be[i], 0, 0)),
                  pl.BlockSpec((1, 1, de), lambda i, be, nb: (be[i], 0, 0)),
                  pl.BlockSpec((1, 1, de), lambda i, be, nb: (be[i], 0, 0)),
                  pl.BlockSpec((1, de, dm), lambda i, be, nb: (be[i], 0, 0)),
                  pl.BlockSpec((1, 1, dm), lambda i, be, nb: (be[i], 0, 0))],
        out_specs=pl.BlockSpec((tm, dm), lambda i, be, nb: (i, 0)),
    )
    yb = pl.pallas_call(
        _moe_kernel,
        grid_spec=grid_spec,
        out_shape=jax.ShapeDtypeStruct((cap, dm), F32),
        compiler_params=_params("arbitrary"),
        name="moe_experts",
    )(block_exp, n_used, x_sorted, slot_gate[:, None], w1p, b1g, b1l, w2b, b2r)
    slots = dest.reshape(n, TOP_K)
    return [yb[slots[:, kk]] for kk in range(TOP_K)]


def _combine_kernel(*refs, n_parts):
    parts, (h_ref, g_ref, nw_ref, sc_ref, sh_ref, hn_ref, a_ref) = refs[:n_parts], refs[n_parts:]
    y = parts[0][...]
    for p in parts[1:]:
        y = y + p[...]
    hn = h_ref[...] + g_ref[0] * y
    hn_ref[...] = hn
    normed = hn * lax.rsqrt(jnp.mean(hn * hn, axis=-1, keepdims=True) + NORM_EPS) * nw_ref[...]
    a_ref[...] = (normed * (1.0 + sc_ref[0]) + sh_ref[0]).astype(a_ref.dtype)


def combine_residual_norm(parts, h, n_ctx, gate, norm_w, scale, shift):
    bsz, t, dm = h.shape
    tm = _tile(n_ctx, 256, 16)
    assert t % tm == 0
    nbt, ncb = t // tm, n_ctx // tm
    m = bsz * t
    rows = pl.BlockSpec((tm, dm), lambda i: (i, 0))
    mod = pl.BlockSpec((1, 1, dm), lambda i: (jnp.where(i % nbt < ncb, bsz, i // nbt), 0, 0))
    hn, a = pl.pallas_call(
        functools.partial(_combine_kernel, n_parts=len(parts)),
        grid=(m // tm,),
        in_specs=[rows] * len(parts) + [rows, mod, pl.BlockSpec((1, dm), lambda i: (0, 0)), mod, mod],
        out_specs=[rows, rows],
        out_shape=[jax.ShapeDtypeStruct((m, dm), F32), jax.ShapeDtypeStruct((m, dm), BF16)],
        compiler_params=_params("parallel"),
        name="combine_residual_norm",
    )(*parts, h.reshape(m, dm), gate[:, None, :], norm_w.astype(F32)[None, :], scale[:, None, :],
      shift[:, None, :])
    return hn.reshape(bsz, t, dm), a.reshape(bsz, t, dm)


def _rms(x, g, eps=NORM_EPS):
    xf = x.astype(F32)
    return xf * lax.rsqrt(jnp.mean(xf * xf, axis=-1, keepdims=True) + eps) * g.astype(F32)


def _modulated(h, n_ctx, norm_w, scale_c, shift_c, scale_x, shift_x):
    hn = _rms(h, norm_w)
    ac = hn[:, :n_ctx] * (1.0 + scale_c) + shift_c
    ax = hn[:, n_ctx:] * (1.0 + scale_x[:, None, :]) + shift_x[:, None, :]
    return jnp.concatenate([ac, ax], axis=1)


def _even_mixer(a, n_ctx, w_in, lam_re, lam_im, log_dt, b_re, b_im, c_re, c_im, d_skip,
                glu_w, glu_b, rpb):
    bsz, t, dm = a.shape
    mix_w = dm // 2
    a2 = a.reshape(bsz * t, dm).astype(BF16)
    u = matmul(a2, w_in[:, :mix_w]).reshape(bsz, t, mix_w)
    qkv = matmul(a2, w_in[:, mix_w:], out_dtype=BF16).reshape(bsz, t, 3 * mix_w)
    y = d_skip.astype(F32) * u + s5_scan(u, n_ctx, lam_re, lam_im, log_dt, b_re, b_im, c_re, c_im)
    gl = jax.nn.gelu(y)
    ya = gl * jax.nn.sigmoid(matmul(gl.reshape(bsz * t, mix_w), glu_w).reshape(bsz, t, mix_w)
                             + glu_b.astype(F32))
    yb = neighborhood_attention(qkv, rpb, n_ctx)
    return ya.astype(BF16), yb


def _rope_tables(n_ctx, seq, hd):
    half = hd // 2
    inv = ROPE_THETA ** (-jnp.arange(0, half, 2, dtype=F32) / half)
    pos = jnp.arange(seq)
    ang_r = (pos // GRID_W).astype(F32)[:, None] * inv[None, :]
    ang_c = (pos % GRID_W).astype(F32)[:, None] * inv[None, :]
    cos = jnp.concatenate([jnp.cos(ang_r), jnp.cos(ang_r), jnp.cos(ang_c), jnp.cos(ang_c)], axis=-1)
    sin = jnp.concatenate([-jnp.sin(ang_r), jnp.sin(ang_r), -jnp.sin(ang_c), jnp.sin(ang_c)], axis=-1)
    cos = jnp.concatenate([jnp.ones((n_ctx, hd), F32), cos], axis=0)
    sin = jnp.concatenate([jnp.zeros((n_ctx, hd), F32), sin], axis=0)
    return cos, sin


def _qk_prep_kernel(z_ref, cos_ref, sin_ref, gq_ref, gk_ref, o_ref, *, n_q_heads, hd):
    cosv = cos_ref[...]
    sinv = sin_ref[...]
    lane = lax.broadcasted_iota(jnp.int32, (1, hd), 1)
    first_quarter = (lane % (hd // 2)) < hd // 4
    for h in range(z_ref.shape[2] // hd):
        x = z_ref[0, :, h * hd:(h + 1) * hd]
        g = gq_ref[...] if h < n_q_heads else gk_ref[...]
        xn = x * lax.rsqrt(jnp.mean(x * x, axis=-1, keepdims=True) + NORM_EPS) * g
        partner = jnp.where(first_quarter, pltpu.roll(xn, hd - hd // 4, axis=1), pltpu.roll(xn, hd // 4, axis=1))
        o_ref[0, :, h * hd:(h + 1) * hd] = (xn * cosv + partner * sinv).astype(o_ref.dtype)


def qk_prepare(zqk, n_ctx, n_q_heads, q_norm, k_norm):
    bsz, t, width = zqk.shape
    hd = GQ_HEAD_DIM
    tm = _tile(t, 256, 16)
    cos, sin = _rope_tables(n_ctx, t - n_ctx, hd)
    rows = pl.BlockSpec((1, tm, width), lambda b, i: (b, i, 0))
    tab = pl.BlockSpec((tm, hd), lambda b, i: (i, 0))
    gain = pl.BlockSpec((1, hd), lambda b, i: (0, 0))
    return pl.pallas_call(
        functools.partial(_qk_prep_kernel, n_q_heads=n_q_heads, hd=hd),
        grid=(bsz, t // tm),
        in_specs=[rows, tab, tab, gain, gain],
        out_specs=rows,
        out_shape=jax.ShapeDtypeStruct((bsz, t, width), BF16),
        compiler_params=_params("parallel", "parallel"),
        name="qk_prepare",
    )(zqk, cos, sin, q_norm.astype(F32)[None, :], k_norm.astype(F32)[None, :])


def _split_bf16(x, terms):
    parts = []
    for _ in range(terms):
        p = x.astype(BF16)
        parts.append(p)
        x = x - p.astype(F32)
    return parts


def _head_sums(x, ones_ref):
    gw = ones_ref.shape[0]
    hi, lo = _split_bf16(x, 2)
    outs = []
    for g in range(x.shape[1] // gw):
        sl = slice(g * gw, (g + 1) * gw)
        outs.append(jnp.dot(hi[:, sl], ones_ref[...], preferred_element_type=F32)
                    + jnp.dot(lo[:, sl], ones_ref[...], preferred_element_type=F32))
    return jnp.concatenate(outs, axis=1)


def _rwkv_prep_kernel(z_ref, zp_ref, zn_ref, mu_ref, gup_ref, wup_ref, aup_ref, vec_ref, rk_ref, tri_ref,
                      ones_ref, at_ref, rt_ref, bt_ref, kt_ref, v_ref, pc_ref, gate_ref, bonus_ref,
                      *, width, tb, c, ncb, nb):
    i = pl.program_id(1)
    keep_prev = jnp.where((i == 0) | (i == ncb), 0.0, 1.0)
    keep_next = jnp.where((i == ncb - 1) | (i == nb - 1), 0.0, 1.0)
    z = z_ref[0]
    rid = lax.broadcasted_iota(jnp.int32, (tb, 1), 0)
    zp = jnp.where(rid == 0, zp_ref[0, 7:8, :] * keep_prev, pltpu.roll(z, 1, axis=0))
    zn = jnp.where(rid == tb - 1, zn_ref[0, 0:1, :] * keep_next, pltpu.roll(z, tb - 1, axis=0))
    zs = z + mu_ref[0:1, :] * (zp - z) + mu_ref[1:2, :] * (zn - z)
    w = width
    r, k, v = zs[:, :w], zs[:, w:2 * w], zs[:, 2 * w:3 * w]
    g_lo = zs[:, 3 * w:3 * w + RK_GATE_RANK]
    o = 3 * w + RK_GATE_RANK
    wl = jnp.tanh(zs[:, o:o + 2 * RK_DECAY_RANK]).astype(BF16)
    al = zs[:, o + 2 * RK_DECAY_RANK:o + 2 * RK_DECAY_RANK + 2 * RK_ICLR_RANK].astype(BF16)
    v_ref[0] = v.astype(v_ref.dtype)
    gate_ref[0] = jnp.dot(jax.nn.sigmoid(g_lo).astype(BF16), gup_ref[...], preferred_element_type=F32)
    ksum = jnp.zeros_like(k)
    for d in range(2):
        w0, a0 = vec_ref[d, 0:1, :], vec_ref[d, 1:2, :]
        k_k, k_a = vec_ref[d, 2:3, :], vec_ref[d, 3:4, :]
        neg = -(w0 + jnp.dot(wl, wup_ref[d], preferred_element_type=F32))
        softplus = jnp.maximum(neg, 0.0) + jnp.log(1.0 + jnp.exp(-jnp.abs(neg)))
        lw = -jnp.exp(-softplus - 0.5)
        iclr = jax.nn.sigmoid(a0 + jnp.dot(al, aup_ref[d], preferred_element_type=F32))
        kk = k * k_k
        kk = kk / jnp.maximum(jnp.sqrt(_head_sums(kk * kk, ones_ref)), 1e-12)
        k_d = k * (1.0 + (iclr - 1.0) * k_a)
        ksum = ksum + k_d
        cum = sum(jnp.dot(tri_ref[d], part, preferred_element_type=F32) for part in _split_bf16(lw, 3))
        p_inv = jnp.exp(-cum)
        at_ref[d, 0] = (-kk * jnp.exp(cum - lw)).astype(at_ref.dtype)
        rt_ref[d, 0] = (r * jnp.exp(cum)).astype(rt_ref.dtype)
        bt_ref[d, 0] = (kk * iclr * p_inv).astype(bt_ref.dtype)
        kt_ref[d, 0] = (k_d * p_inv).astype(kt_ref.dtype)
        for j in range(tb // c):
            row = j * c + (c - 1 if d == 0 else 0)
            pc_ref[d, 0, j] = jnp.exp(cum[row:row + 1, :])
    bonus_ref[0] = _head_sums(r * ksum * rk_ref[...], ones_ref) * v


def rwkv7_prepare(zr, n_ctx, mu, g_up, w0, w_up, a0, a_up, k_k, k_a, r_k):
    bsz, t, rk_in = zr.shape
    width = g_up.shape[1]
    c, tb = RK_CHUNK, RK_TBLOCK
    assert n_ctx % tb == 0 and t % tb == 0 and tb % c == 0 and tb % V7X_SUBLANES == 0
    ncb, nb = n_ctx // tb, t // tb
    rows_per_tile = V7X_SUBLANES

    def pad_rows(up, d, rank):
        return jnp.pad(up, ((d * rank, (1 - d) * rank), (0, 0)))

    wup = jnp.stack([pad_rows(w_up[d], d, RK_DECAY_RANK) for d in range(2)]).astype(BF16)
    aup = jnp.stack([pad_rows(a_up[d], d, RK_ICLR_RANK) for d in range(2)]).astype(BF16)
    vec = jnp.stack([w0, a0, k_k, k_a], axis=1).astype(F32)
    step = np.arange(tb)
    same = (step[:, None] // c) == (step[None, :] // c)
    tri = np.stack([same & (step[None, :] <= step[:, None]), same & (step[None, :] >= step[:, None])])
    seg = np.arange(V7X_MXU_DIM) // RK_HEAD
    ones = (seg[:, None] == seg[None, :])

    def full(shape):
        return pl.BlockSpec(shape, lambda b, i: (0,) * len(shape))

    per_dir = pl.BlockSpec((2, 1, tb, width), lambda b, i: (0, b, i, 0))
    rows = pl.BlockSpec((1, tb, width), lambda b, i: (b, i, 0))
    kern = functools.partial(_rwkv_prep_kernel, width=width, tb=tb, c=c, ncb=ncb, nb=nb)
    stream = jax.ShapeDtypeStruct((2, bsz, t, width), BF16)
    return pl.pallas_call(
        kern,
        grid=(bsz, nb),
        in_specs=[pl.BlockSpec((1, tb, rk_in), lambda b, i: (b, i, 0)),
                  pl.BlockSpec((1, rows_per_tile, rk_in),
                               lambda b, i: (b, jnp.maximum(i * (tb // rows_per_tile) - 1, 0), 0)),
                  pl.BlockSpec((1, rows_per_tile, rk_in),
                               lambda b, i: (b, jnp.minimum((i + 1) * (tb // rows_per_tile),
                                                            t // rows_per_tile - 1), 0)),
                  full((2, rk_in)), full((RK_GATE_RANK, width)), full((2, 2 * RK_DECAY_RANK, width)),
                  full((2, 2 * RK_ICLR_RANK, width)), full((2, 4, width)), full((1, width)),
                  full((2, tb, tb)), full((V7X_MXU_DIM, V7X_MXU_DIM))],
        out_specs=[per_dir, per_dir, per_dir, per_dir, rows,
                   pl.BlockSpec((2, 1, tb // c, 1, width), lambda b, i: (0, b, i, 0, 0)), rows, rows],
        out_shape=[stream, stream, stream, stream, jax.ShapeDtypeStruct((bsz, t, width), BF16),
                   jax.ShapeDtypeStruct((2, bsz, t // c, 1, width), F32),
                   jax.ShapeDtypeStruct((bsz, t, width), F32), jax.ShapeDtypeStruct((bsz, t, width), F32)],
        compiler_params=_params("parallel", "parallel"),
        name="rwkv7_prepare",
    )(zr, zr, zr, mu.astype(F32), g_up.astype(BF16), wup, aup, vec, r_k.astype(F32).reshape(1, width),
      jnp.asarray(tri, BF16), jnp.asarray(ones, BF16))


def _rwkv_finish_kernel(ys_ref, bonus_ref, gate_ref, lnw_ref, lnb_ref, ones_ref, o_ref):
    y = ys_ref[0, 0] + ys_ref[1, 0]
    inv_n = 1.0 / RK_HEAD
    mean = _head_sums(y, ones_ref) * inv_n
    dev = y - mean
    var = _head_sums(dev * dev, ones_ref) * inv_n
    yn = dev * lax.rsqrt(var + RK_GN_EPS) * lnw_ref[...] + lnb_ref[...]
    o_ref[0] = ((yn + bonus_ref[0]) * gate_ref[0]).astype(o_ref.dtype)


def rwkv7_finish(ys, bonus, gate, ln_w, ln_b):
    _, bsz, t, width = ys.shape
    tb = RK_TBLOCK
    seg = np.arange(V7X_MXU_DIM) // RK_HEAD
    ones = (seg[:, None] == seg[None, :])
    rows = pl.BlockSpec((1, tb, width), lambda b, i: (b, i, 0))
    vec = pl.BlockSpec((1, width), lambda b, i: (0, 0))
    return pl.pallas_call(
        _rwkv_finish_kernel,
        grid=(bsz, t // tb),
        in_specs=[pl.BlockSpec((2, 1, tb, width), lambda b, i: (0, b, i, 0)), rows, rows, vec, vec,
                  pl.BlockSpec((V7X_MXU_DIM, V7X_MXU_DIM), lambda b, i: (0, 0))],
        out_specs=rows,
        out_shape=jax.ShapeDtypeStruct((bsz, t, width), BF16),
        compiler_params=_params("parallel", "parallel"),
        name="rwkv7_finish",
    )(ys, bonus, gate, ln_w.astype(F32)[None, :], ln_b.astype(F32)[None, :], jnp.asarray(ones, BF16))


def _odd_mixer(a, n_ctx, w_in, mu, g_up, w0, w_up, a0, a_up, k_k, k_a, r_k, ln_w, ln_b,
               q_norm, k_norm):
    bsz, t, dm = a.shape
    mix_w = dm // 2
    heads = mix_w // RK_HEAD
    rk_in = 3 * mix_w + RK_GATE_RANK + 2 * RK_DECAY_RANK + 2 * RK_ICLR_RANK
    gq_heads = mix_w // GQ_HEAD_DIM
    kv_heads = gq_heads // 4
    kv_w = kv_heads * GQ_HEAD_DIM
    a2 = a.reshape(bsz * t, dm).astype(BF16)
    zr = matmul(a2, w_in[:, :rk_in]).reshape(bsz, t, rk_in)
    zqk = matmul(a2, w_in[:, rk_in:rk_in + mix_w + kv_w]).reshape(bsz, t, mix_w + kv_w)
    vv = matmul(a2, w_in[:, rk_in + mix_w + kv_w:], out_dtype=BF16).reshape(bsz, t, kv_w)

    at, rt, bt, kt, v, pc, gate, bonus = rwkv7_prepare(zr, n_ctx, mu, g_up, w0, w_up, a0, a_up, k_k, k_a,
                                                       r_k.reshape(-1))
    ys = rwkv7_scan(at, rt, bt, kt, v, pc, n_ctx)
    yc = rwkv7_finish(ys, bonus, gate, ln_w, ln_b)
    qk = qk_prepare(zqk, n_ctx, gq_heads, q_norm, k_norm)
    yd = gqa_attention(qk, vv, n_ctx, gq_heads, kv_heads)
    return yc, yd


def kernel(x, c, ctx, c_ctx, w_mod, b_mod, norm1, norm2, final_norm, ev_w_in, ev_w_out, s5_lam_re, s5_lam_im, s5_log_dt, s5_b_re, s5_b_im, s5_c_re, s5_c_im, s5_d, s5_glu_w, s5_glu_b, na_rpb, od_w_in, od_w_out, rk_mu, rk_g_up, rk_w0, rk_w_up, rk_a0, rk_a_up, rk_k_k, rk_k_a, rk_r_k, rk_ln_w, rk_ln_b, gq_q_norm, gq_k_norm, moe_w_router, moe_b_router, moe_w1, moe_b1, moe_w2, moe_b2):
    bsz, seq, dm = x.shape
    n_ctx = ctx.shape[1]
    depth = w_mod.shape[0]
    t = n_ctx + seq
    h = jnp.concatenate([ctx, x], axis=1).astype(F32)
    cond = jnp.concatenate([jax.nn.silu(c), jax.nn.silu(c_ctx)[None, :]], axis=0)
    mods = [jnp.split(matmul(cond, w_mod[i]) + b_mod[i].astype(F32), N_MOD, axis=-1) for i in range(depth)]
    sh1, sc1 = mods[0][0], mods[0][1]
    a = _modulated(h, n_ctx, norm1[0], sc1[bsz], sh1[bsz], sc1[:bsz], sh1[:bsz])
    for i in range(depth):
        last = i == depth - 1
        j = i // 2
        _, _, g1, sh2, sc2, g2 = mods[i]
        if i % 2 == 0:
            ya, yb = _even_mixer(a, n_ctx, ev_w_in[j], s5_lam_re[j], s5_lam_im[j], s5_log_dt[j],
                                 s5_b_re[j], s5_b_im[j], s5_c_re[j], s5_c_im[j], s5_d[j], s5_glu_w[j],
                                 s5_glu_b[j], na_rpb[j])
            w_out = ev_w_out[j]
        else:
            ya, yb = _odd_mixer(a, n_ctx, od_w_in[j], rk_mu[j], rk_g_up[j], rk_w0[j], rk_w_up[j],
                                rk_a0[j], rk_a_up[j], rk_k_k[j], rk_k_a[j], rk_r_k[j], rk_ln_w[j], rk_ln_b[j],
                                gq_q_norm[j], gq_k_norm[j])
            w_out = od_w_out[j]
        h, f = out_proj_residual_norm(ya, yb, w_out, h, n_ctx, g1, norm2[i], sc2, sh2)
        moe_args = (moe_w_router[i], moe_b_router[i], moe_w1, i, moe_b1[i], moe_w2[i], moe_b2[i])
        if last:
            parts = moe_ffn(f[:, n_ctx:].reshape(bsz * seq, dm), *moe_args)
            y = sum(parts[1:], parts[0]).reshape(bsz, seq, dm)
            hx = h[:, n_ctx:] + g2[:bsz][:, None, :] * y
            return _rms(hx, final_norm).astype(x.dtype)
        parts = moe_ffn(f.reshape(bsz * t, dm), *moe_args)
        h, a = combine_residual_norm(parts, h, n_ctx, g2, norm1[i + 1], mods[i + 1][1], mods[i + 1][0])
```

```python
import functools
import math

import numpy as np
import jax
import jax.numpy as jnp
from jax import lax
from jax.experimental import pallas as pl
from jax.experimental.pallas import tpu as pltpu

F32 = jnp.float32
BF16 = jnp.bfloat16

V7X_LANES = 128
V7X_SUBLANES = 8
V7X_MXU_DIM = 256
V7X_VMEM_BYTES = 64 * 1024 * 1024
VMEM_LIMIT = 56 * 1024 * 1024

GRID_W = 64
N_MOD = 6
NORM_EPS = 1e-6
S5_GROUP = 16
S5_STATE = 64
S5_CHUNK = 16
NA_HEADS = 8
WIN_ROWS = 8
WIN_COLS = 16
NA_QROWS = 4
NA_KROWS = NA_QROWS + WIN_ROWS - 1
NA_BATCH = 2
RK_HEAD = 64
RK_DECAY_RANK = 64
RK_ICLR_RANK = 64
RK_GATE_RANK = 128
RK_GN_EPS = 64e-5
RK_CHUNK = 64
RK_TBLOCK = 256
GQ_HEAD_DIM = 128
ROPE_THETA = 10000.0
GQ_TQ = 128
GQ_TK = 2048
N_EXPERTS = 32
TOP_K = 4
SWIGLU_ALPHA = 1.702
SWIGLU_LIMIT = 7.0
MOE_TM = 256
NEG_BIG = -1e30
HI = lax.Precision.HIGHEST


def _params(*sem):
    return pltpu.CompilerParams(dimension_semantics=sem, vmem_limit_bytes=VMEM_LIMIT)


def _tile(n, target, align):
    best = None
    t = align
    while t <= min(n, target):
        if n % t == 0:
            best = t
        t += align
    return n if best is None else best


def _mm_kernel(x_ref, w_ref, o_ref):
    o_ref[...] = jnp.dot(x_ref[...], w_ref[...], preferred_element_type=F32).astype(o_ref.dtype)


def matmul(x, w, out_dtype=F32):
    m, k = x.shape
    n = w.shape[1]
    x = x.astype(BF16)
    w = w.astype(BF16)
    m_pad = -(-m // 16) * 16
    if m_pad != m:
        x = jnp.pad(x, ((0, m_pad - m), (0, 0)))
    tm = _tile(m_pad, 512, 16)
    tn = _tile(n, 2048, V7X_LANES)
    out = pl.pallas_call(
        _mm_kernel,
        grid=(n // tn, m_pad // tm),
        in_specs=[pl.BlockSpec((tm, k), lambda j, i: (i, 0)),
                  pl.BlockSpec((k, tn), lambda j, i: (0, j))],
        out_specs=pl.BlockSpec((tm, tn), lambda j, i: (i, j)),
        out_shape=jax.ShapeDtypeStruct((m_pad, n), out_dtype),
        compiler_params=_params("parallel", "parallel"),
        name="matmul",
    )(x, w)
    return out[:m] if m_pad != m else out


def _out_proj_kernel(xa_ref, xb_ref, wa_ref, wb_ref, h_ref, g_ref, nw_ref, sc_ref, sh_ref, hn_ref, f_ref):
    out = (jnp.dot(xa_ref[...], wa_ref[...], preferred_element_type=F32)
           + jnp.dot(xb_ref[...], wb_ref[...], preferred_element_type=F32))
    hn = h_ref[...] + g_ref[0] * out
    hn_ref[...] = hn
    normed = hn * lax.rsqrt(jnp.mean(hn * hn, axis=-1, keepdims=True) + NORM_EPS) * nw_ref[...]
    f_ref[...] = (normed * (1.0 + sc_ref[0]) + sh_ref[0]).astype(f_ref.dtype)


def out_proj_residual_norm(xa, xb, w_out, h, n_ctx, gate, norm_w, scale, shift):
    bsz, t, dm = h.shape
    wa = xa.shape[-1]
    tm = _tile(n_ctx, 256, 16)
    assert t % tm == 0
    nbt, ncb = t // tm, n_ctx // tm

    def mod_row(i):
        return jnp.where(i % nbt < ncb, bsz, i // nbt)

    rows = pl.BlockSpec((tm, dm), lambda i: (i, 0))
    mod = pl.BlockSpec((1, 1, dm), lambda i: (mod_row(i), 0, 0))
    m = bsz * t
    w_out = w_out.astype(BF16)
    hn, f = pl.pallas_call(
        _out_proj_kernel,
        grid=(m // tm,),
        in_specs=[pl.BlockSpec((tm, wa), lambda i: (i, 0)),
                  pl.BlockSpec((tm, dm - wa), lambda i: (i, 0)),
                  pl.BlockSpec((wa, dm), lambda i: (0, 0)),
                  pl.BlockSpec((dm - wa, dm), lambda i: (0, 0)),
                  rows, mod, pl.BlockSpec((1, dm), lambda i: (0, 0)), mod, mod],
        out_specs=[rows, rows],
        out_shape=[jax.ShapeDtypeStruct((m, dm), F32), jax.ShapeDtypeStruct((m, dm), BF16)],
        compiler_params=_params("parallel"),
        name="out_proj_residual_norm",
    )(xa.reshape(m, wa), xb.reshape(m, dm - wa), w_out[:wa], w_out[wa:], h.reshape(m, dm),
      gate[:, None, :], norm_w.astype(F32)[None, :], scale[:, None, :], shift[:, None, :])
    return hn.reshape(bsz, t, dm), f.reshape(bsz, t, dm)


def _flash_kernel(q_ref, k_ref, v_ref, o_ref, m_sc, l_sc, acc_sc, *, grp, hd, tq, tk, n_ctx,
                  n_ctx_qblocks, n_lat_chunks):
    qi = pl.program_id(2)
    q = jnp.concatenate([q_ref[0, :, g * hd:(g + 1) * hd] for g in range(grp)], axis=0)

    scale = hd ** -0.5
    s = lax.dot_general(q, k_ref[0, 0:n_ctx, :], (((1,), (1,)), ((), ())),
                        preferred_element_type=F32) * scale
    m0 = jnp.max(s, axis=-1, keepdims=True)
    p = jnp.exp(s - m0)
    m_sc[...] = m0
    l_sc[...] = jnp.sum(p, axis=-1, keepdims=True)
    acc_sc[...] = jnp.dot(p.astype(BF16), v_ref[0, 0:n_ctx, :], preferred_element_type=F32)

    def body(j, carry):
        off = pl.multiple_of(n_ctx + j * tk, V7X_LANES)
        k = k_ref[0, pl.ds(off, tk), :]
        v = v_ref[0, pl.ds(off, tk), :]
        s = lax.dot_general(q, k, (((1,), (1,)), ((), ())), preferred_element_type=F32) * scale
        m_prev = m_sc[...]
        m_new = jnp.maximum(m_prev, jnp.max(s, axis=-1, keepdims=True))
        alpha = jnp.exp(m_prev - m_new)
        p = jnp.exp(s - m_new)
        l_sc[...] = alpha * l_sc[...] + jnp.sum(p, axis=-1, keepdims=True)
        acc_sc[...] = alpha * acc_sc[...] + jnp.dot(p.astype(BF16), v, preferred_element_type=F32)
        m_sc[...] = m_new
        return carry

    lax.fori_loop(0, jnp.where(qi < n_ctx_qblocks, 0, n_lat_chunks), body, 0)
    o = acc_sc[...] / l_sc[...]
    for g in range(grp):
        o_ref[0, :, g * hd:(g + 1) * hd] = o[g * tq:(g + 1) * tq].astype(o_ref.dtype)


def gqa_attention(qk, v, n_ctx, n_q_heads, n_kv_heads):
    b, t, _ = qk.shape
    hd = GQ_HEAD_DIM
    qw = n_q_heads * hd
    grp = n_q_heads // n_kv_heads
    tq = GQ_TQ
    tk = _tile(t - n_ctx, GQ_TK, V7X_LANES)
    assert n_ctx % tq == 0 and t % tq == 0 and n_ctx % V7X_LANES == 0
    kern = functools.partial(_flash_kernel, grp=grp, hd=hd, tq=tq, tk=tk, n_ctx=n_ctx,
                             n_ctx_qblocks=n_ctx // tq, n_lat_chunks=(t - n_ctx) // tk)
    rows = grp * tq
    return pl.pallas_call(
        kern,
        grid=(b, n_kv_heads, t // tq),
        in_specs=[pl.BlockSpec((1, tq, grp * hd), lambda bi, h, i: (bi, i, h)),
                  pl.BlockSpec((1, t, hd), lambda bi, h, i: (bi, 0, n_q_heads + h)),
                  pl.BlockSpec((1, t, hd), lambda bi, h, i: (bi, 0, h))],
        out_specs=pl.BlockSpec((1, tq, grp * hd), lambda bi, h, i: (bi, i, h)),
        out_shape=jax.ShapeDtypeStruct((b, t, qw), BF16),
        scratch_shapes=[pltpu.VMEM((rows, 1), F32), pltpu.VMEM((rows, 1), F32),
                        pltpu.VMEM((rows, hd), F32)],
        compiler_params=_params("parallel", "parallel", "parallel"),
        name="gqa_attention",
    )(qk, qk, v)


def _na_kernel(q_ref, k_ref, v_ref, bias_ref, o_ref, *, n_ctx, rows, n_blocks):
    j = pl.program_id(2)
    bs = range(q_ref.shape[0])
    dn = (((1,), (1,)), ((), ()))
    scale = q_ref.shape[-1] ** -0.5
    q = [q_ref[b] for b in bs]
    kc = [k_ref[b, 0:n_ctx, :] for b in bs]
    vc = [v_ref[b, 0:n_ctx, :] for b in bs]
    s_ctx = [lax.dot_general(q[b], kc[b], dn, preferred_element_type=F32) * scale for b in bs]

    def row_max(x):
        return jnp.max(x, axis=-1, keepdims=True)

    def row_sum(x):
        return jnp.sum(x, axis=-1, keepdims=True)

    def pv(p, v):
        return jnp.dot(p.astype(BF16), v, preferred_element_type=F32)

    @pl.when(j == 0)
    def _():
        p = [jnp.exp(s_ctx[b] - row_max(s_ctx[b])) for b in bs]
        for b in bs:
            o_ref[b] = (pv(p[b], vc[b]) / row_sum(p[b])).astype(o_ref.dtype)

    @pl.when(j > 0)
    def _():
        jj = j - 1
        ks = jnp.clip(NA_QROWS * jj - WIN_ROWS // 2, 0, rows - NA_KROWS)
        off = pl.multiple_of(n_ctx + ks * GRID_W, GRID_W)
        pat = jnp.where(jj == 0, 0, jnp.where(jj == n_blocks - 1, 2, 1))
        bias = bias_ref[pat]
        kw = [k_ref[b, pl.ds(off, NA_KROWS * GRID_W), :] for b in bs]
        vw = [v_ref[b, pl.ds(off, NA_KROWS * GRID_W), :] for b in bs]
        s_win = [lax.dot_general(q[b], kw[b], dn, preferred_element_type=F32) * scale + bias for b in bs]
        m = [jnp.maximum(row_max(s_win[b]), row_max(s_ctx[b])) for b in bs]
        p_win = [jnp.exp(s_win[b] - m[b]) for b in bs]
        p_ctx = [jnp.exp(s_ctx[b] - m[b]) for b in bs]
        for b in bs:
            l = row_sum(p_win[b]) + row_sum(p_ctx[b])
            o_ref[b] = ((pv(p_win[b], vw[b]) + pv(p_ctx[b], vc[b])) / l).astype(o_ref.dtype)


def _na_bias(rpb, rows):
    n_blocks = rows // NA_QROWS
    col = np.arange(GRID_W)
    col_start = np.clip(col - WIN_COLS // 2, 0, GRID_W - WIN_COLS)
    col_ok = (col[None, :] >= col_start[:, None]) & (col[None, :] < col_start[:, None] + WIN_COLS)
    dc = np.clip(col[None, :] - col[:, None] + (WIN_COLS - 1), 0, 2 * WIN_COLS - 2)
    pick_c = (dc[:, :, None] == np.arange(2 * WIN_COLS - 1)).astype(np.float32)
    pats = []
    for jj in (0, 1, n_blocks - 1):
        ks = min(max(NA_QROWS * jj - WIN_ROWS // 2, 0), rows - NA_KROWS)
        r = NA_QROWS * jj + np.arange(NA_QROWS)
        kr = ks + np.arange(NA_KROWS)
        r_start = np.clip(r - WIN_ROWS // 2, 0, rows - WIN_ROWS)
        row_ok = (kr[None, :] >= r_start[:, None]) & (kr[None, :] < r_start[:, None] + WIN_ROWS)
        dr = np.clip(kr[None, :] - r[:, None] + (WIN_ROWS - 1), 0, 2 * WIN_ROWS - 2)
        pick_r = (dr[:, :, None] == np.arange(2 * WIN_ROWS - 1)).astype(np.float32)
        bias = jnp.einsum('hab,rka,qcb->hrqkc', rpb.astype(F32), pick_r, pick_c, precision=HI)
        ok = row_ok[:, None, :, None] & col_ok[None, :, None, :]
        bias = jnp.where(ok[None], bias, NEG_BIG)
        pats.append(bias.reshape(rpb.shape[0], NA_QROWS * GRID_W, NA_KROWS * GRID_W))
    return jnp.stack(pats, axis=0)


def neighborhood_attention(qkv, rpb, n_ctx):
    b, t, width3 = qkv.shape
    width = width3 // 3
    hd = width // NA_HEADS
    seq = t - n_ctx
    rows = seq // GRID_W
    qb = NA_QROWS * GRID_W
    assert n_ctx == qb and rows % NA_QROWS == 0 and rows >= NA_KROWS and WIN_ROWS <= rows
    n_blocks = rows // NA_QROWS
    bias = _na_bias(rpb, rows)
    kern = functools.partial(_na_kernel, n_ctx=n_ctx, rows=rows, n_blocks=n_blocks)
    kb = NA_KROWS * GRID_W
    nbatch = NA_BATCH if b % NA_BATCH == 0 else 1
    return pl.pallas_call(
        kern,
        grid=(NA_HEADS, b // nbatch, n_blocks + 1),
        in_specs=[pl.BlockSpec((nbatch, qb, hd), lambda h, bi, j: (bi, j, h)),
                  pl.BlockSpec((nbatch, t, hd), lambda h, bi, j: (bi, 0, NA_HEADS + h)),
                  pl.BlockSpec((nbatch, t, hd), lambda h, bi, j: (bi, 0, 2 * NA_HEADS + h)),
                  pl.BlockSpec((3, None, qb, kb), lambda h, bi, j: (0, h, 0, 0))],
        out_specs=pl.BlockSpec((nbatch, qb, hd), lambda h, bi, j: (bi, j, h)),
        out_shape=jax.ShapeDtypeStruct((b, t, width), BF16),
        compiler_params=_params("parallel", "parallel", "parallel"),
        name="neighborhood_attention",
    )(qkv, qkv, qkv, bias)


def _s5_kernel(u_ref, t_ref, n_ref, m_ref, d_ref, y_ref, hl_sc, hp_sc, *, nc_ctx, nc_all, bsz):
    u = u_ref[0]
    for part in range(4):
        hl_sc[part] = jnp.dot(u, n_ref[0, part], preferred_element_type=F32)
    zero = jnp.zeros((bsz, S5_STATE), F32)

    def scan(direction):
        dr = d_ref[0, 2 * direction:2 * direction + 1, :]
        di = d_ref[0, 2 * direction + 1:2 * direction + 2, :]

        def body(step, carry):
            hr, hi = carry
            if direction == 0:
                c = step
            else:
                c = jnp.where(step < nc_ctx, nc_ctx - 1 - step, nc_all - 1 - (step - nc_ctx))
            r0 = pl.multiple_of(c * bsz, bsz)
            hp_sc[2 * direction, pl.ds(r0, bsz), :] = hr
            hp_sc[2 * direction + 1, pl.ds(r0, bsz), :] = hi
            lr = hl_sc[2 * direction, pl.ds(r0, bsz), :]
            li = hl_sc[2 * direction + 1, pl.ds(r0, bsz), :]
            return dr * hr - di * hi + lr, dr * hi + di * hr + li

        lax.fori_loop(0, nc_all, body, (zero, zero))

    scan(0)
    scan(1)
    y = jnp.dot(u, t_ref[0], preferred_element_type=F32)
    for part in range(4):
        y = y + jnp.dot(hp_sc[part].astype(BF16), m_ref[0, part], preferred_element_type=F32)
    y_ref[0] = y.astype(y_ref.dtype)


def _s5_tables(lam_re, lam_im, log_dt, b_re, b_im, c_re, c_im):
    c = S5_CHUNK
    g, p = lam_re.shape[1], lam_re.shape[2]
    h = b_re.shape[-1]
    toe = 0.0
    n_parts, m_parts, d_parts = [], [], []
    tt = jnp.arange(c)
    for d in range(2):
        lr, li = lam_re[d].astype(F32), lam_im[d].astype(F32)
        dt = jnp.exp(log_dt[d].astype(F32))[:, None]

        def lam_pow(e):
            e = jnp.asarray(e, F32)[..., None, None]
            mag = jnp.exp(e * lr * dt)
            return mag * jnp.cos(e * li * dt), mag * jnp.sin(e * li * dt)

        lbr, lbi = lam_pow(1)
        nr, ni = lbr - 1.0, lbi
        den = lr * lr + li * li
        fr, fi = (nr * lr + ni * li) / den, (ni * lr - nr * li) / den
        br, bi = b_re[d].astype(F32), b_im[d].astype(F32)
        bbr = fr[..., None] * br - fi[..., None] * bi
        bbi = fr[..., None] * bi + fi[..., None] * br
        cr, ci = c_re[d].astype(F32), c_im[d].astype(F32)
        pr, pi = lam_pow(tt)
        lbbr = pr[..., None] * bbr - pi[..., None] * bbi
        lbbi = pr[..., None] * bbi + pi[..., None] * bbr
        kern = (jnp.einsum('gop,tgpi->tgoi', cr, lbbr, precision=HI)
                - jnp.einsum('gop,tgpi->tgoi', ci, lbbi, precision=HI))
        steps = np.arange(c)
        lag = (steps[None, :] - steps[:, None]) if d == 0 else (steps[:, None] - steps[None, :])
        pick = (lag[:, :, None] == steps).astype(np.float32)
        kt = jnp.einsum('stx,xgoi->gsito', pick, kern, precision=HI)
        toe = toe + kt.reshape(g, c * h, c * h)
        e_in = (c - 1 - tt) if d == 0 else tt
        qr, qi = lam_pow(e_in)
        n_r = qr[..., None] * bbr - qi[..., None] * bbi
        n_i = qr[..., None] * bbi + qi[..., None] * bbr
        n_parts += [jnp.transpose(n_r, (1, 0, 3, 2)).reshape(g, c * h, p),
                    jnp.transpose(n_i, (1, 0, 3, 2)).reshape(g, c * h, p)]
        e_out = (tt + 1) if d == 0 else (c - tt)
        sr, si = lam_pow(e_out)
        clr = cr[None] * sr[:, :, None, :] - ci[None] * si[:, :, None, :]
        cli = cr[None] * si[:, :, None, :] + ci[None] * sr[:, :, None, :]
        m_parts += [jnp.transpose(clr, (1, 3, 0, 2)).reshape(g, p, c * h),
                    -jnp.transpose(cli, (1, 3, 0, 2)).reshape(g, p, c * h)]
        dcr, dci = lam_pow(c)
        d_parts += [dcr, dci]
    return (toe.astype(BF16), jnp.stack(n_parts, 1).astype(BF16), jnp.stack(m_parts, 1).astype(BF16),
            jnp.stack(d_parts, 1))


def s5_scan(u, n_ctx, lam_re, lam_im, log_dt, b_re, b_im, c_re, c_im):
    bsz, t, width = u.shape
    g = width // S5_GROUP
    c = S5_CHUNK
    assert bsz == V7X_SUBLANES and t % c == 0 and n_ctx % c == 0
    nc = t // c
    rows = nc * bsz
    toe, n_tab, m_tab, d_tab = _s5_tables(lam_re, lam_im, log_dt, b_re, b_im, c_re, c_im)
    uc = u.astype(BF16).reshape(bsz, nc, c, g, S5_GROUP)
    uc = jnp.transpose(uc, (3, 1, 0, 2, 4)).reshape(g, rows, c * S5_GROUP)
    kern = functools.partial(_s5_kernel, nc_ctx=n_ctx // c, nc_all=nc, bsz=bsz)
    cw = c * S5_GROUP
    y = pl.pallas_call(
        kern,
        grid=(g,),
        in_specs=[pl.BlockSpec((1, rows, cw), lambda i: (i, 0, 0)),
                  pl.BlockSpec((1, cw, cw), lambda i: (i, 0, 0)),
                  pl.BlockSpec((1, 4, cw, S5_STATE), lambda i: (i, 0, 0, 0)),
                  pl.BlockSpec((1, 4, S5_STATE, cw), lambda i: (i, 0, 0, 0)),
                  pl.BlockSpec((1, 4, S5_STATE), lambda i: (i, 0, 0))],
        out_specs=pl.BlockSpec((1, rows, cw), lambda i: (i, 0, 0)),
        out_shape=jax.ShapeDtypeStruct((g, rows, cw), BF16),
        scratch_shapes=[pltpu.VMEM((4, rows, S5_STATE), F32), pltpu.VMEM((4, rows, S5_STATE), F32)],
        compiler_params=_params("parallel"),
        name="s5_scan",
    )(uc, toe, n_tab, m_tab, d_tab)
    y = y.reshape(g, nc, bsz, c, S5_GROUP)
    return jnp.transpose(y, (2, 1, 3, 0, 4)).reshape(bsz, t, width)


def _rwkv_kernel(at_ref, rt_ref, bt_ref, kt_ref, v_ref, pc_ref, y_ref, s_sc, *, heads, c):
    n = pl.program_id(2)

    @pl.when(n == 0)
    def _():
        s_sc[...] = jnp.zeros(s_sc.shape, F32)

    row = lax.broadcasted_iota(jnp.int32, (c, c), 0)
    col = lax.broadcasted_iota(jnp.int32, (c, c), 1)
    ahead = (row - col) * jnp.where(pl.program_id(0) == 0, 1, -1)
    strict = ahead > 0
    incl = ahead >= 0
    nt = (((1,), (1,)), ((), ()))
    tn = (((0,), (0,)), ((), ()))
    n_factors = int(math.log2(c))
    hs = range(heads)
    sls = [slice(j * RK_HEAD, (j + 1) * RK_HEAD) for j in hs]

    def dot(a, b, dims=None):
        a, b = a.astype(BF16), b.astype(BF16)
        if dims is None:
            return jnp.dot(a, b, preferred_element_type=F32)
        return lax.dot_general(a, b, dims, preferred_element_type=F32)

    vv = [v_ref[0, :, sl] for sl in sls]
    s0 = [s_sc[j] for j in hs]
    ar = [jnp.concatenate([at_ref[0, 0, :, sl], rt_ref[0, 0, :, sl]], axis=0) for sl in sls]
    bk = [jnp.concatenate([bt_ref[0, 0, :, sl], kt_ref[0, 0, :, sl]], axis=0) for sl in sls]
    g = [dot(ar[j], bk[j], nt) for j in hs]
    ah = [dot(ar[j], s0[j], nt) for j in hs]
    x = [jnp.where(strict, g[j][:c, :c], 0.0) for j in hs]
    a_ak = [jnp.where(strict, g[j][:c, c:], 0.0) for j in hs]
    m = [jnp.concatenate([jnp.where(incl, g[j][c:, :c], 0.0), jnp.where(incl, g[j][c:, c:], 0.0)], axis=1)
         for j in hs]
    u = [ah[j][:c] + dot(a_ak[j], vv[j]) for j in hs]
    u = [u[j] + dot(x[j], u[j]) for j in hs]
    for _ in range(n_factors - 1):
        x = [dot(x[j], x[j]) for j in hs]
        u = [u[j] + dot(x[j], u[j]) for j in hs]
    uv = [jnp.concatenate([u[j].astype(BF16), vv[j]], axis=0) for j in hs]
    for j in hs:
        y_ref[0, 0, :, sls[j]] = ah[j][c:] + dot(m[j], uv[j])
    for j in hs:
        s_new = s0[j] + dot(uv[j], bk[j], tn)
        s_sc[j] = s_new * pc_ref[0, 0, 0, :, sls[j]]


def rwkv7_scan(at, rt, bt, kt, v, pc, n_ctx):
    _, bsz, t, width = at.shape
    heads = width // RK_HEAD
    c = RK_CHUNK
    assert t % c == 0 and n_ctx % c == 0
    ncc, nc = n_ctx // c, t // c

    def chunk(d, n):
        back = jnp.where(n < ncc, ncc - 1 - n, ncc + (nc - 1 - n))
        return jnp.where(d == 0, n, back)

    stream = pl.BlockSpec((1, 1, c, width), lambda d, bi, n: (d, bi, chunk(d, n), 0))
    kern = functools.partial(_rwkv_kernel, heads=heads, c=c)
    return pl.pallas_call(
        kern,
        grid=(2, bsz, nc),
        in_specs=[stream, stream, stream, stream,
                  pl.BlockSpec((1, c, width), lambda d, bi, n: (bi, chunk(d, n), 0)),
                  pl.BlockSpec((1, 1, 1, 1, width), lambda d, bi, n: (d, bi, chunk(d, n), 0, 0))],
        out_specs=stream,
        out_shape=jax.ShapeDtypeStruct((2, bsz, t, width), F32),
        scratch_shapes=[pltpu.VMEM((heads, RK_HEAD, RK_HEAD), F32)],
        compiler_params=_params("parallel", "parallel", "arbitrary"),
        name="rwkv7_scan",
    )(at, rt, bt, kt, v, pc)


def _deinterleave_kernel(w_ref, p_ref, o_ref):
    g = p_ref.shape[0]
    half = o_ref.shape[1] // 2
    for c in range(w_ref.shape[1] // g):
        t = jnp.dot(w_ref[:, c * g:(c + 1) * g].astype(BF16), p_ref[...], preferred_element_type=F32)
        o_ref[:, c * g // 2:(c + 1) * g // 2] = t[:, :g // 2].astype(o_ref.dtype)
        o_ref[:, half + c * g // 2:half + (c + 1) * g // 2] = t[:, g // 2:].astype(o_ref.dtype)


def deinterleave_columns(w_all, layer):
    _, r, n2 = w_all.shape
    g = V7X_MXU_DIM
    assert n2 % g == 0
    perm = np.concatenate([np.arange(0, g, 2), np.arange(1, g, 2)])
    p = jnp.asarray(np.arange(g)[:, None] == perm[None, :], BF16)
    tm = _tile(r, 1024, 16)
    w = w_all.reshape(-1, n2)
    first = layer * (r // tm)
    return pl.pallas_call(
        _deinterleave_kernel,
        grid=(r // tm,),
        in_specs=[pl.BlockSpec((tm, n2), lambda i: (first + i, 0)), pl.BlockSpec((g, g), lambda i: (0, 0))],
        out_specs=pl.BlockSpec((tm, n2), lambda i: (i, 0)),
        out_shape=jax.ShapeDtypeStruct((r, n2), BF16),
        compiler_params=_params("parallel"),
        name="deinterleave_columns",
    )(w, p)


def _moe_kernel(be_ref, nb_ref, x_ref, g_ref, w1_ref, b1g_ref, b1l_ref, w2_ref, b2_ref, o_ref):
    i = pl.program_id(0)
    de = w2_ref.shape[1]

    @pl.when(i < nb_ref[0])
    def _():
        x = x_ref[...]
        z = jnp.dot(x, w1_ref[0], preferred_element_type=F32)
        zg = z[:, :de] + b1g_ref[0]
        zl = z[:, de:] + b1l_ref[0]
        zg = jnp.minimum(zg, SWIGLU_LIMIT)
        zl = jnp.clip(zl, -SWIGLU_LIMIT, SWIGLU_LIMIT)
        act = zg * jax.nn.sigmoid(SWIGLU_ALPHA * zg) * (zl + 1.0)
        y = jnp.dot(act.astype(BF16), w2_ref[0], preferred_element_type=F32) + b2_ref[0]
        o_ref[...] = (y * g_ref[...]).astype(o_ref.dtype)

    @pl.when(i >= nb_ref[0])
    def _():
        o_ref[...] = jnp.zeros(o_ref.shape, o_ref.dtype)


def moe_ffn(h, w_router, b_router, w1_all, layer, b1, w2, b2):
    n, dm = h.shape
    n_exp = w_router.shape[1]
    de = w2.shape[1]
    tm = MOE_TM
    logits = jnp.dot(h, w_router, precision=HI, preferred_element_type=F32) + b_router.astype(F32)
    top_val, top_idx = lax.top_k(logits, TOP_K)
    gates = jax.nn.softmax(top_val, axis=-1)
    flat_e = top_idx.reshape(-1).astype(jnp.int32)
    rb = V7X_LANES
    assert (n * TOP_K) % rb == 0
    onehot = (flat_e[:, None] == jnp.arange(n_exp, dtype=jnp.int32)[None, :]).astype(F32)
    blocks = onehot.reshape(-1, rb, n_exp)
    tri = np.tril(np.ones((rb, rb), np.float32))
    within = jnp.einsum('ij,bjk->bik', tri, blocks, precision=HI)
    totals = within[:, -1, :]
    before = jnp.cumsum(totals, axis=0) - totals
    csum = within + before[:, None, :]
    rank = (jnp.sum(csum * blocks, axis=-1).reshape(-1) - 1.0).astype(jnp.int32)
    counts = (before[-1] + totals[-1]).astype(jnp.int32)
    padded = (counts + tm - 1) // tm * tm
    pad_end = jnp.cumsum(padded)
    pad_start = pad_end - padded
    dest = pad_start[flat_e] + rank
    n_blocks = -(-(n * TOP_K) // tm) + n_exp
    cap = n_blocks * tm
    slot_src = jnp.full((cap,), -1, jnp.int32).at[dest].set(jnp.arange(n * TOP_K, dtype=jnp.int32))
    filled = slot_src >= 0
    slot_tok = jnp.where(filled, slot_src // TOP_K, n)
    slot_gate = jnp.where(filled, gates.reshape(-1)[jnp.maximum(slot_src, 0)], 0.0)
    block_start = jnp.arange(n_blocks, dtype=jnp.int32) * tm
    block_exp = jnp.minimum(jnp.sum(pad_end[None, :] <= block_start[:, None], axis=1), n_exp - 1).astype(jnp.int32)
    n_used = (pad_end[-1] // tm).astype(jnp.int32).reshape(1)
    h_pad = jnp.concatenate([h.astype(BF16), jnp.zeros((1, dm), BF16)], axis=0)
    x_sorted = h_pad[slot_tok]

    w1p = deinterleave_columns(w1_all.reshape(-1, n_exp * dm, 2 * de), layer).reshape(n_exp, dm, 2 * de)
    b1g = b1[:, None, 0::2].astype(F32)
    b1l = b1[:, None, 1::2].astype(F32)
    w2b = w2.astype(BF16)
    b2r = b2[:, None, :].astype(F32)

    grid_spec = pltpu.PrefetchScalarGridSpec(
        num_scalar_prefetch=2,
        grid=(n_blocks,),
        in_specs=[pl.BlockSpec((tm, dm), lambda i, be, nb: (i, 0)),
                  pl.BlockSpec((tm, 1), lambda i, be, nb: (i, 0)),
                  pl.BlockSpec((1, dm, 2 * de), lambda i, be, nb: (be[i], 0, 0)),
                  pl.BlockSpec((1, 1, de), lambda i, be, nb: (be[i], 0, 0)),
                  pl.BlockSpec((1, 1, de), lambda i, be, nb: (be[i], 0, 0)),
                  pl.BlockSpec((1, de, dm), lambda i, be, nb: (be[i], 0, 0)),
                  pl.BlockSpec((1, 1, dm), lambda i, be, nb: (be[i], 0, 0))],
        out_specs=pl.BlockSpec((tm, dm), lambda i, be, nb: (i, 0)),
    )
    yb = pl.pallas_call(
        _moe_kernel,
        grid_spec=grid_spec,
        out_shape=jax.ShapeDtypeStruct((cap, dm), BF16),
        compiler_params=_params("arbitrary"),
        name="moe_experts",
    )(block_exp, n_used, x_sorted, slot_gate[:, None], w1p, b1g, b1l, w2b, b2r)
    slots = dest.reshape(n, TOP_K)
    return [yb[slots[:, kk]] for kk in range(TOP_K)]


def _combine_kernel(*refs, n_parts):
    parts, (h_ref, g_ref, nw_ref, sc_ref, sh_ref, hn_ref, a_ref) = refs[:n_parts], refs[n_parts:]
    y = parts[0][...].astype(F32)
    for p in parts[1:]:
        y = y + p[...].astype(F32)
    hn = h_ref[...] + g_ref[0] * y
    hn_ref[...] = hn
    normed = hn * lax.rsqrt(jnp.mean(hn * hn, axis=-1, keepdims=True) + NORM_EPS) * nw_ref[...]
    a_ref[...] = (normed * (1.0 + sc_ref[0]) + sh_ref[0]).astype(a_ref.dtype)


def combine_residual_norm(parts, h, n_ctx, gate, norm_w, scale, shift):
    bsz, t, dm = h.shape
    tm = _tile(n_ctx, 256, 16)
    assert t % tm == 0
    nbt, ncb = t // tm, n_ctx // tm
    m = bsz * t
    rows = pl.BlockSpec((tm, dm), lambda i: (i, 0))
    mod = pl.BlockSpec((1, 1, dm), lambda i: (jnp.where(i % nbt < ncb, bsz, i // nbt), 0, 0))
    hn, a = pl.pallas_call(
        functools.partial(_combine_kernel, n_parts=len(parts)),
        grid=(m // tm,),
        in_specs=[rows] * len(parts) + [rows, mod, pl.BlockSpec((1, dm), lambda i: (0, 0)), mod, mod],
        out_specs=[rows, rows],
        out_shape=[jax.ShapeDtypeStruct((m, dm), F32), jax.ShapeDtypeStruct((m, dm), BF16)],
        compiler_params=_params("parallel"),
        name="combine_residual_norm",
    )(*parts, h.reshape(m, dm), gate[:, None, :], norm_w.astype(F32)[None, :], scale[:, None, :],
      shift[:, None, :])
    return hn.reshape(bsz, t, dm), a.reshape(bsz, t, dm)


def _rms(x, g, eps=NORM_EPS):
    xf = x.astype(F32)
    return xf * lax.rsqrt(jnp.mean(xf * xf, axis=-1, keepdims=True) + eps) * g.astype(F32)


def _modulated(h, n_ctx, norm_w, scale_c, shift_c, scale_x, shift_x):
    hn = _rms(h, norm_w)
    ac = hn[:, :n_ctx] * (1.0 + scale_c) + shift_c
    ax = hn[:, n_ctx:] * (1.0 + scale_x[:, None, :]) + shift_x[:, None, :]
    return jnp.concatenate([ac, ax], axis=1)


def _even_mixer(a, n_ctx, w_in, lam_re, lam_im, log_dt, b_re, b_im, c_re, c_im, d_skip,
                glu_w, glu_b, rpb):
    bsz, t, dm = a.shape
    mix_w = dm // 2
    a2 = a.reshape(bsz * t, dm).astype(BF16)
    u = matmul(a2, w_in[:, :mix_w]).reshape(bsz, t, mix_w)
    qkv = matmul(a2, w_in[:, mix_w:], out_dtype=BF16).reshape(bsz, t, 3 * mix_w)
    y = d_skip.astype(F32) * u + s5_scan(u, n_ctx, lam_re, lam_im, log_dt, b_re, b_im, c_re, c_im)
    gl = jax.nn.gelu(y)
    ya = gl * jax.nn.sigmoid(matmul(gl.reshape(bsz * t, mix_w), glu_w).reshape(bsz, t, mix_w)
                             + glu_b.astype(F32))
    yb = neighborhood_attention(qkv, rpb, n_ctx)
    return ya.astype(BF16), yb


def _rope_tables(n_ctx, seq, hd):
    half = hd // 2
    inv = ROPE_THETA ** (-jnp.arange(0, half, 2, dtype=F32) / half)
    pos = jnp.arange(seq)
    ang_r = (pos // GRID_W).astype(F32)[:, None] * inv[None, :]
    ang_c = (pos % GRID_W).astype(F32)[:, None] * inv[None, :]
    cos = jnp.concatenate([jnp.cos(ang_r), jnp.cos(ang_r), jnp.cos(ang_c), jnp.cos(ang_c)], axis=-1)
    sin = jnp.concatenate([-jnp.sin(ang_r), jnp.sin(ang_r), -jnp.sin(ang_c), jnp.sin(ang_c)], axis=-1)
    cos = jnp.concatenate([jnp.ones((n_ctx, hd), F32), cos], axis=0)
    sin = jnp.concatenate([jnp.zeros((n_ctx, hd), F32), sin], axis=0)
    return cos, sin


def _qk_prep_kernel(z_ref, cos_ref, sin_ref, gq_ref, gk_ref, o_ref, *, n_q_heads, hd):
    cosv = cos_ref[...]
    sinv = sin_ref[...]
    lane = lax.broadcasted_iota(jnp.int32, (1, hd), 1)
    first_quarter = (lane % (hd // 2)) < hd // 4
    for h in range(z_ref.shape[2] // hd):
        x = z_ref[0, :, h * hd:(h + 1) * hd]
        g = gq_ref[...] if h < n_q_heads else gk_ref[...]
        xn = x * lax.rsqrt(jnp.mean(x * x, axis=-1, keepdims=True) + NORM_EPS) * g
        partner = jnp.where(first_quarter, pltpu.roll(xn, hd - hd // 4, axis=1), pltpu.roll(xn, hd // 4, axis=1))
        o_ref[0, :, h * hd:(h + 1) * hd] = (xn * cosv + partner * sinv).astype(o_ref.dtype)


def qk_prepare(zqk, n_ctx, n_q_heads, q_norm, k_norm):
    bsz, t, width = zqk.shape
    hd = GQ_HEAD_DIM
    tm = _tile(t, 256, 16)
    cos, sin = _rope_tables(n_ctx, t - n_ctx, hd)
    rows = pl.BlockSpec((1, tm, width), lambda b, i: (b, i, 0))
    tab = pl.BlockSpec((tm, hd), lambda b, i: (i, 0))
    gain = pl.BlockSpec((1, hd), lambda b, i: (0, 0))
    return pl.pallas_call(
        functools.partial(_qk_prep_kernel, n_q_heads=n_q_heads, hd=hd),
        grid=(bsz, t // tm),
        in_specs=[rows, tab, tab, gain, gain],
        out_specs=rows,
        out_shape=jax.ShapeDtypeStruct((bsz, t, width), BF16),
        compiler_params=_params("parallel", "parallel"),
        name="qk_prepare",
    )(zqk, cos, sin, q_norm.astype(F32)[None, :], k_norm.astype(F32)[None, :])


def _split_bf16(x, terms):
    parts = []
    for _ in range(terms):
        p = x.astype(BF16)
        parts.append(p)
        x = x - p.astype(F32)
    return parts


def _head_sums(x, ones_ref):
    gw = ones_ref.shape[0]
    hi, lo = _split_bf16(x, 2)
    outs = []
    for g in range(x.shape[1] // gw):
        sl = slice(g * gw, (g + 1) * gw)
        outs.append(jnp.dot(hi[:, sl], ones_ref[...], preferred_element_type=F32)
                    + jnp.dot(lo[:, sl], ones_ref[...], preferred_element_type=F32))
    return jnp.concatenate(outs, axis=1)


def _rwkv_prep_kernel(z_ref, zp_ref, zn_ref, mu_ref, gup_ref, wup_ref, aup_ref, vec_ref, rk_ref, tri_ref,
                      ones_ref, at_ref, rt_ref, bt_ref, kt_ref, v_ref, pc_ref, gate_ref, bonus_ref,
                      *, width, tb, c, ncb, nb):
    i = pl.program_id(1)
    keep_prev = jnp.where((i == 0) | (i == ncb), 0.0, 1.0)
    keep_next = jnp.where((i == ncb - 1) | (i == nb - 1), 0.0, 1.0)
    z = z_ref[0]
    rid = lax.broadcasted_iota(jnp.int32, (tb, 1), 0)
    zp = jnp.where(rid == 0, zp_ref[0, 7:8, :] * keep_prev, pltpu.roll(z, 1, axis=0))
    zn = jnp.where(rid == tb - 1, zn_ref[0, 0:1, :] * keep_next, pltpu.roll(z, tb - 1, axis=0))
    zs = z + mu_ref[0:1, :] * (zp - z) + mu_ref[1:2, :] * (zn - z)
    w = width
    r, k, v = zs[:, :w], zs[:, w:2 * w], zs[:, 2 * w:3 * w]
    g_lo = zs[:, 3 * w:3 * w + RK_GATE_RANK]
    o = 3 * w + RK_GATE_RANK
    wl = jnp.tanh(zs[:, o:o + 2 * RK_DECAY_RANK]).astype(BF16)
    al = zs[:, o + 2 * RK_DECAY_RANK:o + 2 * RK_DECAY_RANK + 2 * RK_ICLR_RANK].astype(BF16)
    v_ref[0] = v.astype(v_ref.dtype)
    gate_ref[0] = jnp.dot(jax.nn.sigmoid(g_lo).astype(BF16), gup_ref[...], preferred_element_type=F32)
    ksum = jnp.zeros_like(k)
    for d in range(2):
        w0, a0 = vec_ref[d, 0:1, :], vec_ref[d, 1:2, :]
        k_k, k_a = vec_ref[d, 2:3, :], vec_ref[d, 3:4, :]
        neg = -(w0 + jnp.dot(wl, wup_ref[d], preferred_element_type=F32))
        softplus = jnp.maximum(neg, 0.0) + jnp.log(1.0 + jnp.exp(-jnp.abs(neg)))
        lw = -jnp.exp(-softplus - 0.5)
        iclr = jax.nn.sigmoid(a0 + jnp.dot(al, aup_ref[d], preferred_element_type=F32))
        kk = k * k_k
        kk = kk / jnp.maximum(jnp.sqrt(_head_sums(kk * kk, ones_ref)), 1e-12)
        k_d = k * (1.0 + (iclr - 1.0) * k_a)
        ksum = ksum + k_d
        cum = sum(jnp.dot(tri_ref[d], part, preferred_element_type=F32) for part in _split_bf16(lw, 3))
        p_inv = jnp.exp(-cum)
        at_ref[d, 0] = (-kk * jnp.exp(cum - lw)).astype(at_ref.dtype)
        rt_ref[d, 0] = (r * jnp.exp(cum)).astype(rt_ref.dtype)
        bt_ref[d, 0] = (kk * iclr * p_inv).astype(bt_ref.dtype)
        kt_ref[d, 0] = (k_d * p_inv).astype(kt_ref.dtype)
        for j in range(tb // c):
            row = j * c + (c - 1 if d == 0 else 0)
            pc_ref[d, 0, j] = jnp.exp(cum[row:row + 1, :])
    bonus_ref[0] = _head_sums(r * ksum * rk_ref[...], ones_ref) * v


def rwkv7_prepare(zr, n_ctx, mu, g_up, w0, w_up, a0, a_up, k_k, k_a, r_k):
    bsz, t, rk_in = zr.shape
    width = g_up.shape[1]
    c, tb = RK_CHUNK, RK_TBLOCK
    assert n_ctx % tb == 0 and t % tb == 0 and tb % c == 0 and tb % V7X_SUBLANES == 0
    ncb, nb = n_ctx // tb, t // tb
    rows_per_tile = V7X_SUBLANES

    def pad_rows(up, d, rank):
        return jnp.pad(up, ((d * rank, (1 - d) * rank), (0, 0)))

    wup = jnp.stack([pad_rows(w_up[d], d, RK_DECAY_RANK) for d in range(2)]).astype(BF16)
    aup = jnp.stack([pad_rows(a_up[d], d, RK_ICLR_RANK) for d in range(2)]).astype(BF16)
    vec = jnp.stack([w0, a0, k_k, k_a], axis=1).astype(F32)
    step = np.arange(tb)
    same = (step[:, None] // c) == (step[None, :] // c)
    tri = np.stack([same & (step[None, :] <= step[:, None]), same & (step[None, :] >= step[:, None])])
    seg = np.arange(V7X_MXU_DIM) // RK_HEAD
    ones = (seg[:, None] == seg[None, :])

    def full(shape):
        return pl.BlockSpec(shape, lambda b, i: (0,) * len(shape))

    per_dir = pl.BlockSpec((2, 1, tb, width), lambda b, i: (0, b, i, 0))
    rows = pl.BlockSpec((1, tb, width), lambda b, i: (b, i, 0))
    kern = functools.partial(_rwkv_prep_kernel, width=width, tb=tb, c=c, ncb=ncb, nb=nb)
    stream = jax.ShapeDtypeStruct((2, bsz, t, width), BF16)
    return pl.pallas_call(
        kern,
        grid=(bsz, nb),
        in_specs=[pl.BlockSpec((1, tb, rk_in), lambda b, i: (b, i, 0)),
                  pl.BlockSpec((1, rows_per_tile, rk_in),
                               lambda b, i: (b, jnp.maximum(i * (tb // rows_per_tile) - 1, 0), 0)),
                  pl.BlockSpec((1, rows_per_tile, rk_in),
                               lambda b, i: (b, jnp.minimum((i + 1) * (tb // rows_per_tile),
                                                            t // rows_per_tile - 1), 0)),
                  full((2, rk_in)), full((RK_GATE_RANK, width)), full((2, 2 * RK_DECAY_RANK, width)),
                  full((2, 2 * RK_ICLR_RANK, width)), full((2, 4, width)), full((1, width)),
                  full((2, tb, tb)), full((V7X_MXU_DIM, V7X_MXU_DIM))],
        out_specs=[per_dir, per_dir, per_dir, per_dir, rows,
                   pl.BlockSpec((2, 1, tb // c, 1, width), lambda b, i: (0, b, i, 0, 0)), rows, rows],
        out_shape=[stream, stream, stream, stream, jax.ShapeDtypeStruct((bsz, t, width), BF16),
                   jax.ShapeDtypeStruct((2, bsz, t // c, 1, width), F32),
                   jax.ShapeDtypeStruct((bsz, t, width), F32), jax.ShapeDtypeStruct((bsz, t, width), F32)],
        compiler_params=_params("parallel", "parallel"),
        name="rwkv7_prepare",
    )(zr, zr, zr, mu.astype(F32), g_up.astype(BF16), wup, aup, vec, r_k.astype(F32).reshape(1, width),
      jnp.asarray(tri, BF16), jnp.asarray(ones, BF16))


def _rwkv_finish_kernel(ys_ref, bonus_ref, gate_ref, lnw_ref, lnb_ref, ones_ref, o_ref):
    y = ys_ref[0, 0] + ys_ref[1, 0]
    inv_n = 1.0 / RK_HEAD
    mean = _head_sums(y, ones_ref) * inv_n
    dev = y - mean
    var = _head_sums(dev * dev, ones_ref) * inv_n
    yn = dev * lax.rsqrt(var + RK_GN_EPS) * lnw_ref[...] + lnb_ref[...]
    o_ref[0] = ((yn + bonus_ref[0]) * gate_ref[0]).astype(o_ref.dtype)


def rwkv7_finish(ys, bonus, gate, ln_w, ln_b):
    _, bsz, t, width = ys.shape
    tb = RK_TBLOCK
    seg = np.arange(V7X_MXU_DIM) // RK_HEAD
    ones = (seg[:, None] == seg[None, :])
    rows = pl.BlockSpec((1, tb, width), lambda b, i: (b, i, 0))
    vec = pl.BlockSpec((1, width), lambda b, i: (0, 0))
    return pl.pallas_call(
        _rwkv_finish_kernel,
        grid=(bsz, t // tb),
        in_specs=[pl.BlockSpec((2, 1, tb, width), lambda b, i: (0, b, i, 0)), rows, rows, vec, vec,
                  pl.BlockSpec((V7X_MXU_DIM, V7X_MXU_DIM), lambda b, i: (0, 0))],
        out_specs=rows,
        out_shape=jax.ShapeDtypeStruct((bsz, t, width), BF16),
        compiler_params=_params("parallel", "parallel"),
        name="rwkv7_finish",
    )(ys, bonus, gate, ln_w.astype(F32)[None, :], ln_b.astype(F32)[None, :], jnp.asarray(ones, BF16))


def _odd_mixer(a, n_ctx, w_in, mu, g_up, w0, w_up, a0, a_up, k_k, k_a, r_k, ln_w, ln_b,
               q_norm, k_norm):
    bsz, t, dm = a.shape
    mix_w = dm // 2
    heads = mix_w // RK_HEAD
    rk_in = 3 * mix_w + RK_GATE_RANK + 2 * RK_DECAY_RANK + 2 * RK_ICLR_RANK
    gq_heads = mix_w // GQ_HEAD_DIM
    kv_heads = gq_heads // 4
    kv_w = kv_heads * GQ_HEAD_DIM
    a2 = a.reshape(bsz * t, dm).astype(BF16)
    zr = matmul(a2, w_in[:, :rk_in]).reshape(bsz, t, rk_in)
    zqk = matmul(a2, w_in[:, rk_in:rk_in + mix_w + kv_w]).reshape(bsz, t, mix_w + kv_w)
    vv = matmul(a2, w_in[:, rk_in + mix_w + kv_w:], out_dtype=BF16).reshape(bsz, t, kv_w)

    at, rt, bt, kt, v, pc, gate, bonus = rwkv7_prepare(zr, n_ctx, mu, g_up, w0, w_up, a0, a_up, k_k, k_a,
                                                       r_k.reshape(-1))
    ys = rwkv7_scan(at, rt, bt, kt, v, pc, n_ctx)
    yc = rwkv7_finish(ys, bonus, gate, ln_w, ln_b)
    qk = qk_prepare(zqk, n_ctx, gq_heads, q_norm, k_norm)
    yd = gqa_attention(qk, vv, n_ctx, gq_heads, kv_heads)
    return yc, yd


def kernel(x, c, ctx, c_ctx, w_mod, b_mod, norm1, norm2, final_norm, ev_w_in, ev_w_out, s5_lam_re, s5_lam_im, s5_log_dt, s5_b_re, s5_b_im, s5_c_re, s5_c_im, s5_d, s5_glu_w, s5_glu_b, na_rpb, od_w_in, od_w_out, rk_mu, rk_g_up, rk_w0, rk_w_up, rk_a0, rk_a_up, rk_k_k, rk_k_a, rk_r_k, rk_ln_w, rk_ln_b, gq_q_norm, gq_k_norm, moe_w_router, moe_b_router, moe_w1, moe_b1, moe_w2, moe_b2):
    bsz, seq, dm = x.shape
    n_ctx = ctx.shape[1]
    depth = w_mod.shape[0]
    t = n_ctx + seq
    h = jnp.concatenate([ctx, x], axis=1).astype(F32)
    cond = jnp.concatenate([jax.nn.silu(c), jax.nn.silu(c_ctx)[None, :]], axis=0)
    mods = [jnp.split(matmul(cond, w_mod[i]) + b_mod[i].astype(F32), N_MOD, axis=-1) for i in range(depth)]
    sh1, sc1 = mods[0][0], mods[0][1]
    a = _modulated(h, n_ctx, norm1[0], sc1[bsz], sh1[bsz], sc1[:bsz], sh1[:bsz])
    for i in range(depth):
        last = i == depth - 1
        j = i // 2
        _, _, g1, sh2, sc2, g2 = mods[i]
        if i % 2 == 0:
            ya, yb = _even_mixer(a, n_ctx, ev_w_in[j], s5_lam_re[j], s5_lam_im[j], s5_log_dt[j],
                                 s5_b_re[j], s5_b_im[j], s5_c_re[j], s5_c_im[j], s5_d[j], s5_glu_w[j],
                                 s5_glu_b[j], na_rpb[j])
            w_out = ev_w_out[j]
        else:
            ya, yb = _odd_mixer(a, n_ctx, od_w_in[j], rk_mu[j], rk_g_up[j], rk_w0[j], rk_w_up[j],
                                rk_a0[j], rk_a_up[j], rk_k_k[j], rk_k_a[j], rk_r_k[j], rk_ln_w[j], rk_ln_b[j],
                                gq_q_norm[j], gq_k_norm[j])
            w_out = od_w_out[j]
        h, f = out_proj_residual_norm(ya, yb, w_out, h, n_ctx, g1, norm2[i], sc2, sh2)
        moe_args = (moe_w_router[i], moe_b_router[i], moe_w1, i, moe_b1[i], moe_w2[i], moe_b2[i])
        if last:
            parts = moe_ffn(f[:, n_ctx:].reshape(bsz * seq, dm), *moe_args)
            y = sum(p.astype(F32) for p in parts).reshape(bsz, seq, dm)
            hx = h[:, n_ctx:] + g2[:bsz][:, None, :] * y
            return _rms(hx, final_norm).astype(x.dtype)
        parts = moe_ffn(f.reshape(bsz * t, dm), *moe_args)
        h, a = combine_residual_norm(parts, h, n_ctx, g2, norm1[i + 1], mods[i + 1][1], mods[i + 1][0])
```

```python
import functools
import math

import numpy as np
import jax
import jax.numpy as jnp
from jax import lax
from jax.experimental import pallas as pl
from jax.experimental.pallas import tpu as pltpu

F32 = jnp.float32
BF16 = jnp.bfloat16

V7X_LANES = 128
V7X_SUBLANES = 8
V7X_MXU_DIM = 256
V7X_VMEM_BYTES = 64 * 1024 * 1024
VMEM_LIMIT = 56 * 1024 * 1024

GRID_W = 64
N_MOD = 6
NORM_EPS = 1e-6
S5_GROUP = 16
S5_STATE = 64
S5_CHUNK = 16
NA_HEADS = 8
WIN_ROWS = 8
WIN_COLS = 16
NA_QROWS = 4
NA_KROWS = NA_QROWS + WIN_ROWS - 1
NA_BATCH = 2
RK_HEAD = 64
RK_DECAY_RANK = 64
RK_ICLR_RANK = 64
RK_GATE_RANK = 128
RK_GN_EPS = 64e-5
RK_CHUNK = 64
RK_TBLOCK = 256
GQ_HEAD_DIM = 128
ROPE_THETA = 10000.0
GQ_TQ = 128
GQ_TK = 2048
N_EXPERTS = 32
TOP_K = 4
SWIGLU_ALPHA = 1.702
SWIGLU_LIMIT = 7.0
MOE_TM = 256
NEG_BIG = -1e30
HI = lax.Precision.HIGHEST


def _params(*sem):
    return pltpu.CompilerParams(dimension_semantics=sem, vmem_limit_bytes=VMEM_LIMIT)


def _tile(n, target, align):
    best = None
    t = align
    while t <= min(n, target):
        if n % t == 0:
            best = t
        t += align
    return n if best is None else best


def _mm_kernel(x_ref, w_ref, o_ref):
    o_ref[...] = jnp.dot(x_ref[...], w_ref[...], preferred_element_type=F32).astype(o_ref.dtype)


def matmul(x, w, out_dtype=F32):
    m, k = x.shape
    n = w.shape[1]
    x = x.astype(BF16)
    w = w.astype(BF16)
    m_pad = -(-m // 16) * 16
    if m_pad != m:
        x = jnp.pad(x, ((0, m_pad - m), (0, 0)))
    tm = _tile(m_pad, 512, 16)
    tn = _tile(n, 2048, V7X_LANES)
    out = pl.pallas_call(
        _mm_kernel,
        grid=(n // tn, m_pad // tm),
        in_specs=[pl.BlockSpec((tm, k), lambda j, i: (i, 0)),
                  pl.BlockSpec((k, tn), lambda j, i: (0, j))],
        out_specs=pl.BlockSpec((tm, tn), lambda j, i: (i, j)),
        out_shape=jax.ShapeDtypeStruct((m_pad, n), out_dtype),
        compiler_params=_params("parallel", "parallel"),
        name="matmul",
    )(x, w)
    return out[:m] if m_pad != m else out


def _out_proj_kernel(xa_ref, xb_ref, wa_ref, wb_ref, h_ref, g_ref, nw_ref, sc_ref, sh_ref, hn_ref, f_ref):
    out = (jnp.dot(xa_ref[...], wa_ref[...], preferred_element_type=F32)
           + jnp.dot(xb_ref[...], wb_ref[...], preferred_element_type=F32))
    hn = h_ref[...] + g_ref[0] * out
    hn_ref[...] = hn
    normed = hn * lax.rsqrt(jnp.mean(hn * hn, axis=-1, keepdims=True) + NORM_EPS) * nw_ref[...]
    f_ref[...] = (normed * (1.0 + sc_ref[0]) + sh_ref[0]).astype(f_ref.dtype)


def out_proj_residual_norm(xa, xb, w_out, h, n_ctx, gate, norm_w, scale, shift):
    bsz, t, dm = h.shape
    wa = xa.shape[-1]
    tm = _tile(n_ctx, 256, 16)
    assert t % tm == 0
    nbt, ncb = t // tm, n_ctx // tm

    def mod_row(i):
        return jnp.where(i % nbt < ncb, bsz, i // nbt)

    rows = pl.BlockSpec((tm, dm), lambda i: (i, 0))
    mod = pl.BlockSpec((1, 1, dm), lambda i: (mod_row(i), 0, 0))
    m = bsz * t
    w_out = w_out.astype(BF16)
    hn, f = pl.pallas_call(
        _out_proj_kernel,
        grid=(m // tm,),
        in_specs=[pl.BlockSpec((tm, wa), lambda i: (i, 0)),
                  pl.BlockSpec((tm, dm - wa), lambda i: (i, 0)),
                  pl.BlockSpec((wa, dm), lambda i: (0, 0)),
                  pl.BlockSpec((dm - wa, dm), lambda i: (0, 0)),
                  rows, mod, pl.BlockSpec((1, dm), lambda i: (0, 0)), mod, mod],
        out_specs=[rows, rows],
        out_shape=[jax.ShapeDtypeStruct((m, dm), F32), jax.ShapeDtypeStruct((m, dm), BF16)],
        compiler_params=_params("parallel"),
        name="out_proj_residual_norm",
    )(xa.reshape(m, wa), xb.reshape(m, dm - wa), w_out[:wa], w_out[wa:], h.reshape(m, dm),
      gate[:, None, :], norm_w.astype(F32)[None, :], scale[:, None, :], shift[:, None, :])
    return hn.reshape(bsz, t, dm), f.reshape(bsz, t, dm)


def _flash_kernel(q_ref, k_ref, v_ref, o_ref, m_sc, l_sc, acc_sc, *, grp, hd, tq, tk, n_ctx,
                  n_ctx_qblocks, n_lat_chunks):
    qi = pl.program_id(2)
    q = jnp.concatenate([q_ref[0, :, g * hd:(g + 1) * hd] for g in range(grp)], axis=0)

    scale = hd ** -0.5
    s = lax.dot_general(q, k_ref[0, 0:n_ctx, :], (((1,), (1,)), ((), ())),
                        preferred_element_type=F32) * scale
    m0 = jnp.max(s, axis=-1, keepdims=True)
    p = jnp.exp(s - m0)
    m_sc[...] = m0
    l_sc[...] = jnp.sum(p, axis=-1, keepdims=True)
    acc_sc[...] = jnp.dot(p.astype(BF16), v_ref[0, 0:n_ctx, :], preferred_element_type=F32)

    def body(j, carry):
        off = pl.multiple_of(n_ctx + j * tk, V7X_LANES)
        k = k_ref[0, pl.ds(off, tk), :]
        v = v_ref[0, pl.ds(off, tk), :]
        s = lax.dot_general(q, k, (((1,), (1,)), ((), ())), preferred_element_type=F32) * scale
        m_prev = m_sc[...]
        m_new = jnp.maximum(m_prev, jnp.max(s, axis=-1, keepdims=True))
        alpha = jnp.exp(m_prev - m_new)
        p = jnp.exp(s - m_new)
        l_sc[...] = alpha * l_sc[...] + jnp.sum(p, axis=-1, keepdims=True)
        acc_sc[...] = alpha * acc_sc[...] + jnp.dot(p.astype(BF16), v, preferred_element_type=F32)
        m_sc[...] = m_new
        return carry

    lax.fori_loop(0, jnp.where(qi < n_ctx_qblocks, 0, n_lat_chunks), body, 0)
    o = acc_sc[...] / l_sc[...]
    for g in range(grp):
        o_ref[0, :, g * hd:(g + 1) * hd] = o[g * tq:(g + 1) * tq].astype(o_ref.dtype)


def gqa_attention(qk, v, n_ctx, n_q_heads, n_kv_heads):
    b, t, _ = qk.shape
    hd = GQ_HEAD_DIM
    qw = n_q_heads * hd
    grp = n_q_heads // n_kv_heads
    tq = GQ_TQ
    tk = _tile(t - n_ctx, GQ_TK, V7X_LANES)
    assert n_ctx % tq == 0 and t % tq == 0 and n_ctx % V7X_LANES == 0
    kern = functools.partial(_flash_kernel, grp=grp, hd=hd, tq=tq, tk=tk, n_ctx=n_ctx,
                             n_ctx_qblocks=n_ctx // tq, n_lat_chunks=(t - n_ctx) // tk)
    rows = grp * tq
    return pl.pallas_call(
        kern,
        grid=(b, n_kv_heads, t // tq),
        in_specs=[pl.BlockSpec((1, tq, grp * hd), lambda bi, h, i: (bi, i, h)),
                  pl.BlockSpec((1, t, hd), lambda bi, h, i: (bi, 0, n_q_heads + h)),
                  pl.BlockSpec((1, t, hd), lambda bi, h, i: (bi, 0, h))],
        out_specs=pl.BlockSpec((1, tq, grp * hd), lambda bi, h, i: (bi, i, h)),
        out_shape=jax.ShapeDtypeStruct((b, t, qw), BF16),
        scratch_shapes=[pltpu.VMEM((rows, 1), F32), pltpu.VMEM((rows, 1), F32),
                        pltpu.VMEM((rows, hd), F32)],
        compiler_params=_params("parallel", "parallel", "parallel"),
        name="gqa_attention",
    )(qk, qk, v)


def _na_kernel(q_ref, k_ref, v_ref, bias_ref, o_ref, *, n_ctx, rows, n_blocks):
    j = pl.program_id(2)
    bs = range(q_ref.shape[0])
    dn = (((1,), (1,)), ((), ()))
    scale = q_ref.shape[-1] ** -0.5
    q = [q_ref[b] for b in bs]
    kc = [k_ref[b, 0:n_ctx, :] for b in bs]
    vc = [v_ref[b, 0:n_ctx, :] for b in bs]
    s_ctx = [lax.dot_general(q[b], kc[b], dn, preferred_element_type=F32) * scale for b in bs]

    def row_max(x):
        return jnp.max(x, axis=-1, keepdims=True)

    def row_sum(x):
        return jnp.sum(x, axis=-1, keepdims=True)

    def pv(p, v):
        return jnp.dot(p.astype(BF16), v, preferred_element_type=F32)

    @pl.when(j == 0)
    def _():
        p = [jnp.exp(s_ctx[b] - row_max(s_ctx[b])) for b in bs]
        for b in bs:
            o_ref[b] = (pv(p[b], vc[b]) / row_sum(p[b])).astype(o_ref.dtype)

    @pl.when(j > 0)
    def _():
        jj = j - 1
        ks = jnp.clip(NA_QROWS * jj - WIN_ROWS // 2, 0, rows - NA_KROWS)
        off = pl.multiple_of(n_ctx + ks * GRID_W, GRID_W)
        pat = jnp.where(jj == 0, 0, jnp.where(jj == n_blocks - 1, 2, 1))
        bias = bias_ref[pat]
        kw = [k_ref[b, pl.ds(off, NA_KROWS * GRID_W), :] for b in bs]
        vw = [v_ref[b, pl.ds(off, NA_KROWS * GRID_W), :] for b in bs]
        s_win = [lax.dot_general(q[b], kw[b], dn, preferred_element_type=F32) * scale + bias for b in bs]
        m = [jnp.maximum(row_max(s_win[b]), row_max(s_ctx[b])) for b in bs]
        p_win = [jnp.exp(s_win[b] - m[b]) for b in bs]
        p_ctx = [jnp.exp(s_ctx[b] - m[b]) for b in bs]
        for b in bs:
            l = row_sum(p_win[b]) + row_sum(p_ctx[b])
            o_ref[b] = ((pv(p_win[b], vw[b]) + pv(p_ctx[b], vc[b])) / l).astype(o_ref.dtype)


def _na_bias(rpb, rows):
    n_blocks = rows // NA_QROWS
    col = np.arange(GRID_W)
    col_start = np.clip(col - WIN_COLS // 2, 0, GRID_W - WIN_COLS)
    col_ok = (col[None, :] >= col_start[:, None]) & (col[None, :] < col_start[:, None] + WIN_COLS)
    dc = np.clip(col[None, :] - col[:, None] + (WIN_COLS - 1), 0, 2 * WIN_COLS - 2)
    pick_c = (dc[:, :, None] == np.arange(2 * WIN_COLS - 1)).astype(np.float32)
    pats = []
    for jj in (0, 1, n_blocks - 1):
        ks = min(max(NA_QROWS * jj - WIN_ROWS // 2, 0), rows - NA_KROWS)
        r = NA_QROWS * jj + np.arange(NA_QROWS)
        kr = ks + np.arange(NA_KROWS)
        r_start = np.clip(r - WIN_ROWS // 2, 0, rows - WIN_ROWS)
        row_ok = (kr[None, :] >= r_start[:, None]) & (kr[None, :] < r_start[:, None] + WIN_ROWS)
        dr = np.clip(kr[None, :] - r[:, None] + (WIN_ROWS - 1), 0, 2 * WIN_ROWS - 2)
        pick_r = (dr[:, :, None] == np.arange(2 * WIN_ROWS - 1)).astype(np.float32)
        bias = jnp.einsum('hab,rka,qcb->hrqkc', rpb.astype(F32), pick_r, pick_c, precision=HI)
        ok = row_ok[:, None, :, None] & col_ok[None, :, None, :]
        bias = jnp.where(ok[None], bias, NEG_BIG)
        pats.append(bias.reshape(rpb.shape[0], NA_QROWS * GRID_W, NA_KROWS * GRID_W))
    return jnp.stack(pats, axis=0)


def neighborhood_attention(qkv, rpb, n_ctx):
    b, t, width3 = qkv.shape
    width = width3 // 3
    hd = width // NA_HEADS
    seq = t - n_ctx
    rows = seq // GRID_W
    qb = NA_QROWS * GRID_W
    assert n_ctx == qb and rows % NA_QROWS == 0 and rows >= NA_KROWS and WIN_ROWS <= rows
    n_blocks = rows // NA_QROWS
    bias = _na_bias(rpb, rows)
    kern = functools.partial(_na_kernel, n_ctx=n_ctx, rows=rows, n_blocks=n_blocks)
    kb = NA_KROWS * GRID_W
    nbatch = NA_BATCH if b % NA_BATCH == 0 else 1
    return pl.pallas_call(
        kern,
        grid=(NA_HEADS, b // nbatch, n_blocks + 1),
        in_specs=[pl.BlockSpec((nbatch, qb, hd), lambda h, bi, j: (bi, j, h)),
                  pl.BlockSpec((nbatch, t, hd), lambda h, bi, j: (bi, 0, NA_HEADS + h)),
                  pl.BlockSpec((nbatch, t, hd), lambda h, bi, j: (bi, 0, 2 * NA_HEADS + h)),
                  pl.BlockSpec((3, None, qb, kb), lambda h, bi, j: (0, h, 0, 0))],
        out_specs=pl.BlockSpec((nbatch, qb, hd), lambda h, bi, j: (bi, j, h)),
        out_shape=jax.ShapeDtypeStruct((b, t, width), BF16),
        compiler_params=_params("parallel", "parallel", "parallel"),
        name="neighborhood_attention",
    )(qkv, qkv, qkv, bias)


def _s5_kernel(u_ref, t_ref, n_ref, m_ref, d_ref, y_ref, hl_sc, hp_sc, *, nc_ctx, nc_all, bsz):
    u = u_ref[0]
    for part in range(4):
        hl_sc[part] = jnp.dot(u, n_ref[0, part], preferred_element_type=F32)
    zero = jnp.zeros((bsz, S5_STATE), F32)

    def scan(direction):
        dr = d_ref[0, 2 * direction:2 * direction + 1, :]
        di = d_ref[0, 2 * direction + 1:2 * direction + 2, :]

        def body(step, carry):
            hr, hi = carry
            if direction == 0:
                c = step
            else:
                c = jnp.where(step < nc_ctx, nc_ctx - 1 - step, nc_all - 1 - (step - nc_ctx))
            r0 = pl.multiple_of(c * bsz, bsz)
            hp_sc[2 * direction, pl.ds(r0, bsz), :] = hr
            hp_sc[2 * direction + 1, pl.ds(r0, bsz), :] = hi
            lr = hl_sc[2 * direction, pl.ds(r0, bsz), :]
            li = hl_sc[2 * direction + 1, pl.ds(r0, bsz), :]
            return dr * hr - di * hi + lr, dr * hi + di * hr + li

        lax.fori_loop(0, nc_all, body, (zero, zero))

    scan(0)
    scan(1)
    y = jnp.dot(u, t_ref[0], preferred_element_type=F32)
    for part in range(4):
        y = y + jnp.dot(hp_sc[part].astype(BF16), m_ref[0, part], preferred_element_type=F32)
    y_ref[0] = y.astype(y_ref.dtype)


def _s5_tables(lam_re, lam_im, log_dt, b_re, b_im, c_re, c_im):
    c = S5_CHUNK
    g, p = lam_re.shape[1], lam_re.shape[2]
    h = b_re.shape[-1]
    toe = 0.0
    n_parts, m_parts, d_parts = [], [], []
    tt = jnp.arange(c)
    for d in range(2):
        lr, li = lam_re[d].astype(F32), lam_im[d].astype(F32)
        dt = jnp.exp(log_dt[d].astype(F32))[:, None]

        def lam_pow(e):
            e = jnp.asarray(e, F32)[..., None, None]
            mag = jnp.exp(e * lr * dt)
            return mag * jnp.cos(e * li * dt), mag * jnp.sin(e * li * dt)

        lbr, lbi = lam_pow(1)
        nr, ni = lbr - 1.0, lbi
        den = lr * lr + li * li
        fr, fi = (nr * lr + ni * li) / den, (ni * lr - nr * li) / den
        br, bi = b_re[d].astype(F32), b_im[d].astype(F32)
        bbr = fr[..., None] * br - fi[..., None] * bi
        bbi = fr[..., None] * bi + fi[..., None] * br
        cr, ci = c_re[d].astype(F32), c_im[d].astype(F32)
        pr, pi = lam_pow(tt)
        lbbr = pr[..., None] * bbr - pi[..., None] * bbi
        lbbi = pr[..., None] * bbi + pi[..., None] * bbr
        kern = (jnp.einsum('gop,tgpi->tgoi', cr, lbbr, precision=HI)
                - jnp.einsum('gop,tgpi->tgoi', ci, lbbi, precision=HI))
        steps = np.arange(c)
        lag = (steps[None, :] - steps[:, None]) if d == 0 else (steps[:, None] - steps[None, :])
        pick = (lag[:, :, None] == steps).astype(np.float32)
        kt = jnp.einsum('stx,xgoi->gsito', pick, kern, precision=HI)
        toe = toe + kt.reshape(g, c * h, c * h)
        e_in = (c - 1 - tt) if d == 0 else tt
        qr, qi = lam_pow(e_in)
        n_r = qr[..., None] * bbr - qi[..., None] * bbi
        n_i = qr[..., None] * bbi + qi[..., None] * bbr
        n_parts += [jnp.transpose(n_r, (1, 0, 3, 2)).reshape(g, c * h, p),
                    jnp.transpose(n_i, (1, 0, 3, 2)).reshape(g, c * h, p)]
        e_out = (tt + 1) if d == 0 else (c - tt)
        sr, si = lam_pow(e_out)
        clr = cr[None] * sr[:, :, None, :] - ci[None] * si[:, :, None, :]
        cli = cr[None] * si[:, :, None, :] + ci[None] * sr[:, :, None, :]
        m_parts += [jnp.transpose(clr, (1, 3, 0, 2)).reshape(g, p, c * h),
                    -jnp.transpose(cli, (1, 3, 0, 2)).reshape(g, p, c * h)]
        dcr, dci = lam_pow(c)
        d_parts += [dcr, dci]
    return (toe.astype(BF16), jnp.stack(n_parts, 1).astype(BF16), jnp.stack(m_parts, 1).astype(BF16),
            jnp.stack(d_parts, 1))


def s5_scan(u, n_ctx, lam_re, lam_im, log_dt, b_re, b_im, c_re, c_im):
    bsz, t, width = u.shape
    g = width // S5_GROUP
    c = S5_CHUNK
    assert bsz == V7X_SUBLANES and t % c == 0 and n_ctx % c == 0
    nc = t // c
    rows = nc * bsz
    toe, n_tab, m_tab, d_tab = _s5_tables(lam_re, lam_im, log_dt, b_re, b_im, c_re, c_im)
    uc = u.astype(BF16).reshape(bsz, nc, c, g, S5_GROUP)
    uc = jnp.transpose(uc, (3, 1, 0, 2, 4)).reshape(g, rows, c * S5_GROUP)
    kern = functools.partial(_s5_kernel, nc_ctx=n_ctx // c, nc_all=nc, bsz=bsz)
    cw = c * S5_GROUP
    y = pl.pallas_call(
        kern,
        grid=(g,),
        in_specs=[pl.BlockSpec((1, rows, cw), lambda i: (i, 0, 0)),
                  pl.BlockSpec((1, cw, cw), lambda i: (i, 0, 0)),
                  pl.BlockSpec((1, 4, cw, S5_STATE), lambda i: (i, 0, 0, 0)),
                  pl.BlockSpec((1, 4, S5_STATE, cw), lambda i: (i, 0, 0, 0)),
                  pl.BlockSpec((1, 4, S5_STATE), lambda i: (i, 0, 0))],
        out_specs=pl.BlockSpec((1, rows, cw), lambda i: (i, 0, 0)),
        out_shape=jax.ShapeDtypeStruct((g, rows, cw), BF16),
        scratch_shapes=[pltpu.VMEM((4, rows, S5_STATE), F32), pltpu.VMEM((4, rows, S5_STATE), F32)],
        compiler_params=_params("parallel"),
        name="s5_scan",
    )(uc, toe, n_tab, m_tab, d_tab)
    y = y.reshape(g, nc, bsz, c, S5_GROUP)
    return jnp.transpose(y, (2, 1, 3, 0, 4)).reshape(bsz, t, width)


def _rwkv_kernel(at_ref, rt_ref, bt_ref, kt_ref, v_ref, pc_ref, y_ref, s_sc, *, heads, c):
    n = pl.program_id(2)

    @pl.when(n == 0)
    def _():
        s_sc[...] = jnp.zeros(s_sc.shape, F32)

    row = lax.broadcasted_iota(jnp.int32, (c, c), 0)
    col = lax.broadcasted_iota(jnp.int32, (c, c), 1)
    ahead = (row - col) * jnp.where(pl.program_id(0) == 0, 1, -1)
    strict = ahead > 0
    incl = ahead >= 0
    nt = (((1,), (1,)), ((), ()))
    tn = (((0,), (0,)), ((), ()))
    n_factors = int(math.log2(c))
    hs = range(heads)
    sls = [slice(j * RK_HEAD, (j + 1) * RK_HEAD) for j in hs]

    def dot(a, b, dims=None):
        a, b = a.astype(BF16), b.astype(BF16)
        if dims is None:
            return jnp.dot(a, b, preferred_element_type=F32)
        return lax.dot_general(a, b, dims, preferred_element_type=F32)

    vv = [v_ref[0, :, sl] for sl in sls]
    s0 = [s_sc[j] for j in hs]
    ar = [jnp.concatenate([at_ref[0, 0, :, sl], rt_ref[0, 0, :, sl]], axis=0) for sl in sls]
    bk = [jnp.concatenate([bt_ref[0, 0, :, sl], kt_ref[0, 0, :, sl]], axis=0) for sl in sls]
    g = [dot(ar[j], bk[j], nt) for j in hs]
    ah = [dot(ar[j], s0[j], nt) for j in hs]
    x = [jnp.where(strict, g[j][:c, :c], 0.0) for j in hs]
    a_ak = [jnp.where(strict, g[j][:c, c:], 0.0) for j in hs]
    m = [jnp.concatenate([jnp.where(incl, g[j][c:, :c], 0.0), jnp.where(incl, g[j][c:, c:], 0.0)], axis=1)
         for j in hs]
    u = [ah[j][:c] + dot(a_ak[j], vv[j]) for j in hs]
    u = [u[j] + dot(x[j], u[j]) for j in hs]
    for _ in range(n_factors - 1):
        x = [dot(x[j], x[j]) for j in hs]
        u = [u[j] + dot(x[j], u[j]) for j in hs]
    uv = [jnp.concatenate([u[j].astype(BF16), vv[j]], axis=0) for j in hs]
    for j in hs:
        y_ref[0, 0, :, sls[j]] = ah[j][c:] + dot(m[j], uv[j])
    for j in hs:
        s_new = s0[j] + dot(uv[j], bk[j], tn)
        s_sc[j] = s_new * pc_ref[0, 0, 0, :, sls[j]]


def rwkv7_scan(at, rt, bt, kt, v, pc, n_ctx):
    _, bsz, t, width = at.shape
    heads = width // RK_HEAD
    c = RK_CHUNK
    assert t % c == 0 and n_ctx % c == 0
    ncc, nc = n_ctx // c, t // c

    def chunk(d, n):
        back = jnp.where(n < ncc, ncc - 1 - n, ncc + (nc - 1 - n))
        return jnp.where(d == 0, n, back)

    stream = pl.BlockSpec((1, 1, c, width), lambda d, bi, n: (d, bi, chunk(d, n), 0))
    kern = functools.partial(_rwkv_kernel, heads=heads, c=c)
    return pl.pallas_call(
        kern,
        grid=(2, bsz, nc),
        in_specs=[stream, stream, stream, stream,
                  pl.BlockSpec((1, c, width), lambda d, bi, n: (bi, chunk(d, n), 0)),
                  pl.BlockSpec((1, 1, 1, 1, width), lambda d, bi, n: (d, bi, chunk(d, n), 0, 0))],
        out_specs=stream,
        out_shape=jax.ShapeDtypeStruct((2, bsz, t, width), F32),
        scratch_shapes=[pltpu.VMEM((heads, RK_HEAD, RK_HEAD), F32)],
        compiler_params=_params("parallel", "parallel", "arbitrary"),
        name="rwkv7_scan",
    )(at, rt, bt, kt, v, pc)


def _deinterleave_kernel(w_ref, p_ref, o_ref):
    g = p_ref.shape[0]
    half = o_ref.shape[1] // 2
    for c in range(w_ref.shape[1] // g):
        t = jnp.dot(w_ref[:, c * g:(c + 1) * g].astype(BF16), p_ref[...], preferred_element_type=F32)
        o_ref[:, c * g // 2:(c + 1) * g // 2] = t[:, :g // 2].astype(o_ref.dtype)
        o_ref[:, half + c * g // 2:half + (c + 1) * g // 2] = t[:, g // 2:].astype(o_ref.dtype)


def deinterleave_columns(w_all, layer):
    _, r, n2 = w_all.shape
    g = V7X_MXU_DIM
    assert n2 % g == 0
    perm = np.concatenate([np.arange(0, g, 2), np.arange(1, g, 2)])
    p = jnp.asarray(np.arange(g)[:, None] == perm[None, :], BF16)
    tm = _tile(r, 1024, 16)
    w = w_all.reshape(-1, n2)
    first = layer * (r // tm)
    return pl.pallas_call(
        _deinterleave_kernel,
        grid=(r // tm,),
        in_specs=[pl.BlockSpec((tm, n2), lambda i: (first + i, 0)), pl.BlockSpec((g, g), lambda i: (0, 0))],
        out_specs=pl.BlockSpec((tm, n2), lambda i: (i, 0)),
        out_shape=jax.ShapeDtypeStruct((r, n2), BF16),
        compiler_params=_params("parallel"),
        name="deinterleave_columns",
    )(w, p)


def _moe_kernel(be_ref, nb_ref, x_ref, g_ref, w1_ref, b1g_ref, b1l_ref, w2_ref, b2_ref, o_ref, w2_sc):
    i = pl.program_id(0)
    de = w2_ref.shape[1]

    @pl.when((i == 0) | (be_ref[i] != be_ref[jnp.maximum(i - 1, 0)]))
    def _():
        w2_sc[...] = w2_ref[0].astype(w2_sc.dtype)

    @pl.when(i < nb_ref[0])
    def _():
        x = x_ref[...]
        z = jnp.dot(x, w1_ref[0], preferred_element_type=F32)
        zg = z[:, :de] + b1g_ref[0]
        zl = z[:, de:] + b1l_ref[0]
        zg = jnp.minimum(zg, SWIGLU_LIMIT)
        zl = jnp.clip(zl, -SWIGLU_LIMIT, SWIGLU_LIMIT)
        act = zg * jax.nn.sigmoid(SWIGLU_ALPHA * zg) * (zl + 1.0)
        y = jnp.dot(act.astype(BF16), w2_sc[...], preferred_element_type=F32) + b2_ref[0]
        o_ref[...] = (y * g_ref[...]).astype(o_ref.dtype)

    @pl.when(i >= nb_ref[0])
    def _():
        o_ref[...] = jnp.zeros(o_ref.shape, o_ref.dtype)


def moe_ffn(h, w_router, b_router, w1_all, w2_all, layer, b1, b2):
    n, dm = h.shape
    n_exp = w_router.shape[1]
    de = w2_all.shape[2]
    tm = MOE_TM
    logits = jnp.dot(h, w_router, precision=HI, preferred_element_type=F32) + b_router.astype(F32)
    top_val, top_idx = lax.top_k(logits, TOP_K)
    gates = jax.nn.softmax(top_val, axis=-1)
    flat_e = top_idx.reshape(-1).astype(jnp.int32)
    rb = V7X_LANES
    assert (n * TOP_K) % rb == 0
    onehot = (flat_e[:, None] == jnp.arange(n_exp, dtype=jnp.int32)[None, :]).astype(F32)
    blocks = onehot.reshape(-1, rb, n_exp)
    tri = np.tril(np.ones((rb, rb), np.float32))
    within = jnp.einsum('ij,bjk->bik', tri, blocks, precision=HI)
    totals = within[:, -1, :]
    before = jnp.cumsum(totals, axis=0) - totals
    csum = within + before[:, None, :]
    rank = (jnp.sum(csum * blocks, axis=-1).reshape(-1) - 1.0).astype(jnp.int32)
    counts = (before[-1] + totals[-1]).astype(jnp.int32)
    padded = (counts + tm - 1) // tm * tm
    pad_end = jnp.cumsum(padded)
    pad_start = pad_end - padded
    dest = pad_start[flat_e] + rank
    n_blocks = -(-(n * TOP_K) // tm) + n_exp
    cap = n_blocks * tm
    slot_src = jnp.full((cap,), -1, jnp.int32).at[dest].set(jnp.arange(n * TOP_K, dtype=jnp.int32))
    filled = slot_src >= 0
    slot_tok = jnp.where(filled, slot_src // TOP_K, 0)
    slot_gate = jnp.where(filled, gates.reshape(-1)[jnp.maximum(slot_src, 0)], 0.0)
    block_start = jnp.arange(n_blocks, dtype=jnp.int32) * tm
    block_exp = jnp.minimum(jnp.sum(pad_end[None, :] <= block_start[:, None], axis=1), n_exp - 1).astype(jnp.int32)
    n_used = (pad_end[-1] // tm).astype(jnp.int32).reshape(1)
    x_sorted = h.astype(BF16)[slot_tok]

    w1p = deinterleave_columns(w1_all.reshape(-1, n_exp * dm, 2 * de), layer).reshape(n_exp, dm, 2 * de)
    b1g = b1[:, None, 0::2].astype(F32)
    b1l = b1[:, None, 1::2].astype(F32)
    b2r = b2[:, None, :].astype(F32)

    grid_spec = pltpu.PrefetchScalarGridSpec(
        num_scalar_prefetch=2,
        grid=(n_blocks,),
        in_specs=[pl.BlockSpec((tm, dm), lambda i, be, nb: (i, 0)),
                  pl.BlockSpec((tm, 1), lambda i, be, nb: (i, 0)),
                  pl.BlockSpec((1, dm, 2 * de), lambda i, be, nb: (be[i], 0, 0)),
                  pl.BlockSpec((1, 1, de), lambda i, be, nb: (be[i], 0, 0)),
                  pl.BlockSpec((1, 1, de), lambda i, be, nb: (be[i], 0, 0)),
                  pl.BlockSpec((1, de, dm), lambda i, be, nb: (layer * n_exp + be[i], 0, 0)),
                  pl.BlockSpec((1, 1, dm), lambda i, be, nb: (be[i], 0, 0))],
        out_specs=pl.BlockSpec((tm, dm), lambda i, be, nb: (i, 0)),
        scratch_shapes=[pltpu.VMEM((de, dm), BF16)],
    )
    yb = pl.pallas_call(
        _moe_kernel,
        grid_spec=grid_spec,
        out_shape=jax.ShapeDtypeStruct((cap, dm), BF16),
        compiler_params=_params("arbitrary"),
        name="moe_experts",
    )(block_exp, n_used, x_sorted, slot_gate[:, None], w1p, b1g, b1l, w2_all.reshape(-1, de, dm), b2r)
    slots = dest.reshape(n, TOP_K)
    return [yb[slots[:, kk]] for kk in range(TOP_K)]


def _combine_kernel(*refs, n_parts):
    parts, (h_ref, g_ref, nw_ref, sc_ref, sh_ref, hn_ref, a_ref) = refs[:n_parts], refs[n_parts:]
    y = parts[0][...].astype(F32)
    for p in parts[1:]:
        y = y + p[...].astype(F32)
    hn = h_ref[...] + g_ref[0] * y
    hn_ref[...] = hn
    normed = hn * lax.rsqrt(jnp.mean(hn * hn, axis=-1, keepdims=True) + NORM_EPS) * nw_ref[...]
    a_ref[...] = (normed * (1.0 + sc_ref[0]) + sh_ref[0]).astype(a_ref.dtype)


def combine_residual_norm(parts, h, n_ctx, gate, norm_w, scale, shift):
    bsz, t, dm = h.shape
    tm = _tile(n_ctx, 256, 16)
    assert t % tm == 0
    nbt, ncb = t // tm, n_ctx // tm
    m = bsz * t
    rows = pl.BlockSpec((tm, dm), lambda i: (i, 0))
    mod = pl.BlockSpec((1, 1, dm), lambda i: (jnp.where(i % nbt < ncb, bsz, i // nbt), 0, 0))
    hn, a = pl.pallas_call(
        functools.partial(_combine_kernel, n_parts=len(parts)),
        grid=(m // tm,),
        in_specs=[rows] * len(parts) + [rows, mod, pl.BlockSpec((1, dm), lambda i: (0, 0)), mod, mod],
        out_specs=[rows, rows],
        out_shape=[jax.ShapeDtypeStruct((m, dm), F32), jax.ShapeDtypeStruct((m, dm), BF16)],
        compiler_params=_params("parallel"),
        name="combine_residual_norm",
    )(*parts, h.reshape(m, dm), gate[:, None, :], norm_w.astype(F32)[None, :], scale[:, None, :],
      shift[:, None, :])
    return hn.reshape(bsz, t, dm), a.reshape(bsz, t, dm)


def _rms(x, g, eps=NORM_EPS):
    xf = x.astype(F32)
    return xf * lax.rsqrt(jnp.mean(xf * xf, axis=-1, keepdims=True) + eps) * g.astype(F32)


def _modulated(h, n_ctx, norm_w, scale_c, shift_c, scale_x, shift_x):
    hn = _rms(h, norm_w)
    ac = hn[:, :n_ctx] * (1.0 + scale_c) + shift_c
    ax = hn[:, n_ctx:] * (1.0 + scale_x[:, None, :]) + shift_x[:, None, :]
    return jnp.concatenate([ac, ax], axis=1)


def _even_mixer(a, n_ctx, w_in, lam_re, lam_im, log_dt, b_re, b_im, c_re, c_im, d_skip,
                glu_w, glu_b, rpb):
    bsz, t, dm = a.shape
    mix_w = dm // 2
    a2 = a.reshape(bsz * t, dm).astype(BF16)
    u = matmul(a2, w_in[:, :mix_w]).reshape(bsz, t, mix_w)
    qkv = matmul(a2, w_in[:, mix_w:], out_dtype=BF16).reshape(bsz, t, 3 * mix_w)
    y = d_skip.astype(F32) * u + s5_scan(u, n_ctx, lam_re, lam_im, log_dt, b_re, b_im, c_re, c_im)
    gl = jax.nn.gelu(y)
    ya = gl * jax.nn.sigmoid(matmul(gl.reshape(bsz * t, mix_w), glu_w).reshape(bsz, t, mix_w)
                             + glu_b.astype(F32))
    yb = neighborhood_attention(qkv, rpb, n_ctx)
    return ya.astype(BF16), yb


def _rope_tables(n_ctx, seq, hd):
    half = hd // 2
    inv = ROPE_THETA ** (-jnp.arange(0, half, 2, dtype=F32) / half)
    pos = jnp.arange(seq)
    ang_r = (pos // GRID_W).astype(F32)[:, None] * inv[None, :]
    ang_c = (pos % GRID_W).astype(F32)[:, None] * inv[None, :]
    cos = jnp.concatenate([jnp.cos(ang_r), jnp.cos(ang_r), jnp.cos(ang_c), jnp.cos(ang_c)], axis=-1)
    sin = jnp.concatenate([-jnp.sin(ang_r), jnp.sin(ang_r), -jnp.sin(ang_c), jnp.sin(ang_c)], axis=-1)
    cos = jnp.concatenate([jnp.ones((n_ctx, hd), F32), cos], axis=0)
    sin = jnp.concatenate([jnp.zeros((n_ctx, hd), F32), sin], axis=0)
    return cos, sin


def _qk_prep_kernel(z_ref, cos_ref, sin_ref, gq_ref, gk_ref, o_ref, *, n_q_heads, hd):
    cosv = cos_ref[...]
    sinv = sin_ref[...]
    lane = lax.broadcasted_iota(jnp.int32, (1, hd), 1)
    first_quarter = (lane % (hd // 2)) < hd // 4
    for h in range(z_ref.shape[2] // hd):
        x = z_ref[0, :, h * hd:(h + 1) * hd]
        g = gq_ref[...] if h < n_q_heads else gk_ref[...]
        xn = x * lax.rsqrt(jnp.mean(x * x, axis=-1, keepdims=True) + NORM_EPS) * g
        partner = jnp.where(first_quarter, pltpu.roll(xn, hd - hd // 4, axis=1), pltpu.roll(xn, hd // 4, axis=1))
        o_ref[0, :, h * hd:(h + 1) * hd] = (xn * cosv + partner * sinv).astype(o_ref.dtype)


def qk_prepare(zqk, n_ctx, n_q_heads, q_norm, k_norm):
    bsz, t, width = zqk.shape
    hd = GQ_HEAD_DIM
    tm = _tile(t, 256, 16)
    cos, sin = _rope_tables(n_ctx, t - n_ctx, hd)
    rows = pl.BlockSpec((1, tm, width), lambda b, i: (b, i, 0))
    tab = pl.BlockSpec((tm, hd), lambda b, i: (i, 0))
    gain = pl.BlockSpec((1, hd), lambda b, i: (0, 0))
    return pl.pallas_call(
        functools.partial(_qk_prep_kernel, n_q_heads=n_q_heads, hd=hd),
        grid=(bsz, t // tm),
        in_specs=[rows, tab, tab, gain, gain],
        out_specs=rows,
        out_shape=jax.ShapeDtypeStruct((bsz, t, width), BF16),
        compiler_params=_params("parallel", "parallel"),
        name="qk_prepare",
    )(zqk, cos, sin, q_norm.astype(F32)[None, :], k_norm.astype(F32)[None, :])


def _split_bf16(x, terms):
    parts = []
    for _ in range(terms):
        p = x.astype(BF16)
        parts.append(p)
        x = x - p.astype(F32)
    return parts


def _head_sums(x, ones_ref):
    gw = ones_ref.shape[0]
    hi, lo = _split_bf16(x, 2)
    outs = []
    for g in range(x.shape[1] // gw):
        sl = slice(g * gw, (g + 1) * gw)
        outs.append(jnp.dot(hi[:, sl], ones_ref[...], preferred_element_type=F32)
                    + jnp.dot(lo[:, sl], ones_ref[...], preferred_element_type=F32))
    return jnp.concatenate(outs, axis=1)


def _rwkv_prep_kernel(z_ref, zp_ref, zn_ref, mu_ref, gup_ref, wup_ref, aup_ref, vec_ref, rk_ref, tri_ref,
                      ones_ref, at_ref, rt_ref, bt_ref, kt_ref, v_ref, pc_ref, gate_ref, bonus_ref,
                      *, width, tb, c, ncb, nb):
    i = pl.program_id(1)
    keep_prev = jnp.where((i == 0) | (i == ncb), 0.0, 1.0)
    keep_next = jnp.where((i == ncb - 1) | (i == nb - 1), 0.0, 1.0)
    z = z_ref[0]
    rid = lax.broadcasted_iota(jnp.int32, (tb, 1), 0)
    zp = jnp.where(rid == 0, zp_ref[0, 7:8, :] * keep_prev, pltpu.roll(z, 1, axis=0))
    zn = jnp.where(rid == tb - 1, zn_ref[0, 0:1, :] * keep_next, pltpu.roll(z, tb - 1, axis=0))
    zs = z + mu_ref[0:1, :] * (zp - z) + mu_ref[1:2, :] * (zn - z)
    w = width
    r, k, v = zs[:, :w], zs[:, w:2 * w], zs[:, 2 * w:3 * w]
    g_lo = zs[:, 3 * w:3 * w + RK_GATE_RANK]
    o = 3 * w + RK_GATE_RANK
    wl = jnp.tanh(zs[:, o:o + 2 * RK_DECAY_RANK]).astype(BF16)
    al = zs[:, o + 2 * RK_DECAY_RANK:o + 2 * RK_DECAY_RANK + 2 * RK_ICLR_RANK].astype(BF16)
    v_ref[0] = v.astype(v_ref.dtype)
    gate_ref[0] = jnp.dot(jax.nn.sigmoid(g_lo).astype(BF16), gup_ref[...], preferred_element_type=F32)
    ksum = jnp.zeros_like(k)
    for d in range(2):
        w0, a0 = vec_ref[d, 0:1, :], vec_ref[d, 1:2, :]
        k_k, k_a = vec_ref[d, 2:3, :], vec_ref[d, 3:4, :]
        neg = -(w0 + jnp.dot(wl, wup_ref[d], preferred_element_type=F32))
        softplus = jnp.maximum(neg, 0.0) + jnp.log(1.0 + jnp.exp(-jnp.abs(neg)))
        lw = -jnp.exp(-softplus - 0.5)
        iclr = jax.nn.sigmoid(a0 + jnp.dot(al, aup_ref[d], preferred_element_type=F32))
        kk = k * k_k
        kk = kk / jnp.maximum(jnp.sqrt(_head_sums(kk * kk, ones_ref)), 1e-12)
        k_d = k * (1.0 + (iclr - 1.0) * k_a)
        ksum = ksum + k_d
        cum = sum(jnp.dot(tri_ref[d], part, preferred_element_type=F32) for part in _split_bf16(lw, 3))
        p_inv = jnp.exp(-cum)
        at_ref[d, 0] = (-kk * jnp.exp(cum - lw)).astype(at_ref.dtype)
        rt_ref[d, 0] = (r * jnp.exp(cum)).astype(rt_ref.dtype)
        bt_ref[d, 0] = (kk * iclr * p_inv).astype(bt_ref.dtype)
        kt_ref[d, 0] = (k_d * p_inv).astype(kt_ref.dtype)
        for j in range(tb // c):
            row = j * c + (c - 1 if d == 0 else 0)
            pc_ref[d, 0, j] = jnp.exp(cum[row:row + 1, :])
    bonus_ref[0] = _head_sums(r * ksum * rk_ref[...], ones_ref) * v


def rwkv7_prepare(zr, n_ctx, mu, g_up, w0, w_up, a0, a_up, k_k, k_a, r_k):
    bsz, t, rk_in = zr.shape
    width = g_up.shape[1]
    c, tb = RK_CHUNK, RK_TBLOCK
    assert n_ctx % tb == 0 and t % tb == 0 and tb % c == 0 and tb % V7X_SUBLANES == 0
    ncb, nb = n_ctx // tb, t // tb
    rows_per_tile = V7X_SUBLANES

    def pad_rows(up, d, rank):
        return jnp.pad(up, ((d * rank, (1 - d) * rank), (0, 0)))

    wup = jnp.stack([pad_rows(w_up[d], d, RK_DECAY_RANK) for d in range(2)]).astype(BF16)
    aup = jnp.stack([pad_rows(a_up[d], d, RK_ICLR_RANK) for d in range(2)]).astype(BF16)
    vec = jnp.stack([w0, a0, k_k, k_a], axis=1).astype(F32)
    step = np.arange(tb)
    same = (step[:, None] // c) == (step[None, :] // c)
    tri = np.stack([same & (step[None, :] <= step[:, None]), same & (step[None, :] >= step[:, None])])
    seg = np.arange(V7X_MXU_DIM) // RK_HEAD
    ones = (seg[:, None] == seg[None, :])

    def full(shape):
        return pl.BlockSpec(shape, lambda b, i: (0,) * len(shape))

    per_dir = pl.BlockSpec((2, 1, tb, width), lambda b, i: (0, b, i, 0))
    rows = pl.BlockSpec((1, tb, width), lambda b, i: (b, i, 0))
    kern = functools.partial(_rwkv_prep_kernel, width=width, tb=tb, c=c, ncb=ncb, nb=nb)
    stream = jax.ShapeDtypeStruct((2, bsz, t, width), BF16)
    return pl.pallas_call(
        kern,
        grid=(bsz, nb),
        in_specs=[pl.BlockSpec((1, tb, rk_in), lambda b, i: (b, i, 0)),
                  pl.BlockSpec((1, rows_per_tile, rk_in),
                               lambda b, i: (b, jnp.maximum(i * (tb // rows_per_tile) - 1, 0), 0)),
                  pl.BlockSpec((1, rows_per_tile, rk_in),
                               lambda b, i: (b, jnp.minimum((i + 1) * (tb // rows_per_tile),
                                                            t // rows_per_tile - 1), 0)),
                  full((2, rk_in)), full((RK_GATE_RANK, width)), full((2, 2 * RK_DECAY_RANK, width)),
                  full((2, 2 * RK_ICLR_RANK, width)), full((2, 4, width)), full((1, width)),
                  full((2, tb, tb)), full((V7X_MXU_DIM, V7X_MXU_DIM))],
        out_specs=[per_dir, per_dir, per_dir, per_dir, rows,
                   pl.BlockSpec((2, 1, tb // c, 1, width), lambda b, i: (0, b, i, 0, 0)), rows, rows],
        out_shape=[stream, stream, stream, stream, jax.ShapeDtypeStruct((bsz, t, width), BF16),
                   jax.ShapeDtypeStruct((2, bsz, t // c, 1, width), F32),
                   jax.ShapeDtypeStruct((bsz, t, width), F32), jax.ShapeDtypeStruct((bsz, t, width), F32)],
        compiler_params=_params("parallel", "parallel"),
        name="rwkv7_prepare",
    )(zr, zr, zr, mu.astype(F32), g_up.astype(BF16), wup, aup, vec, r_k.astype(F32).reshape(1, width),
      jnp.asarray(tri, BF16), jnp.asarray(ones, BF16))


def _rwkv_finish_kernel(ys_ref, bonus_ref, gate_ref, lnw_ref, lnb_ref, ones_ref, o_ref):
    y = ys_ref[0, 0] + ys_ref[1, 0]
    inv_n = 1.0 / RK_HEAD
    mean = _head_sums(y, ones_ref) * inv_n
    dev = y - mean
    var = _head_sums(dev * dev, ones_ref) * inv_n
    yn = dev * lax.rsqrt(var + RK_GN_EPS) * lnw_ref[...] + lnb_ref[...]
    o_ref[0] = ((yn + bonus_ref[0]) * gate_ref[0]).astype(o_ref.dtype)


def rwkv7_finish(ys, bonus, gate, ln_w, ln_b):
    _, bsz, t, width = ys.shape
    tb = RK_TBLOCK
    seg = np.arange(V7X_MXU_DIM) // RK_HEAD
    ones = (seg[:, None] == seg[None, :])
    rows = pl.BlockSpec((1, tb, width), lambda b, i: (b, i, 0))
    vec = pl.BlockSpec((1, width), lambda b, i: (0, 0))
    return pl.pallas_call(
        _rwkv_finish_kernel,
        grid=(bsz, t // tb),
        in_specs=[pl.BlockSpec((2, 1, tb, width), lambda b, i: (0, b, i, 0)), rows, rows, vec, vec,
                  pl.BlockSpec((V7X_MXU_DIM, V7X_MXU_DIM), lambda b, i: (0, 0))],
        out_specs=rows,
        out_shape=jax.ShapeDtypeStruct((bsz, t, width), BF16),
        compiler_params=_params("parallel", "parallel"),
        name="rwkv7_finish",
    )(ys, bonus, gate, ln_w.astype(F32)[None, :], ln_b.astype(F32)[None, :], jnp.asarray(ones, BF16))


def _odd_mixer(a, n_ctx, w_in, mu, g_up, w0, w_up, a0, a_up, k_k, k_a, r_k, ln_w, ln_b,
               q_norm, k_norm):
    bsz, t, dm = a.shape
    mix_w = dm // 2
    heads = mix_w // RK_HEAD
    rk_in = 3 * mix_w + RK_GATE_RANK + 2 * RK_DECAY_RANK + 2 * RK_ICLR_RANK
    gq_heads = mix_w // GQ_HEAD_DIM
    kv_heads = gq_heads // 4
    kv_w = kv_heads * GQ_HEAD_DIM
    a2 = a.reshape(bsz * t, dm).astype(BF16)
    zr = matmul(a2, w_in[:, :rk_in]).reshape(bsz, t, rk_in)
    zqk = matmul(a2, w_in[:, rk_in:rk_in + mix_w + kv_w]).reshape(bsz, t, mix_w + kv_w)
    vv = matmul(a2, w_in[:, rk_in + mix_w + kv_w:], out_dtype=BF16).reshape(bsz, t, kv_w)

    at, rt, bt, kt, v, pc, gate, bonus = rwkv7_prepare(zr, n_ctx, mu, g_up, w0, w_up, a0, a_up, k_k, k_a,
                                                       r_k.reshape(-1))
    ys = rwkv7_scan(at, rt, bt, kt, v, pc, n_ctx)
    yc = rwkv7_finish(ys, bonus, gate, ln_w, ln_b)
    qk = qk_prepare(zqk, n_ctx, gq_heads, q_norm, k_norm)
    yd = gqa_attention(qk, vv, n_ctx, gq_heads, kv_heads)
    return yc, yd


def kernel(x, c, ctx, c_ctx, w_mod, b_mod, norm1, norm2, final_norm, ev_w_in, ev_w_out, s5_lam_re, s5_lam_im, s5_log_dt, s5_b_re, s5_b_im, s5_c_re, s5_c_im, s5_d, s5_glu_w, s5_glu_b, na_rpb, od_w_in, od_w_out, rk_mu, rk_g_up, rk_w0, rk_w_up, rk_a0, rk_a_up, rk_k_k, rk_k_a, rk_r_k, rk_ln_w, rk_ln_b, gq_q_norm, gq_k_norm, moe_w_router, moe_b_router, moe_w1, moe_b1, moe_w2, moe_b2):
    bsz, seq, dm = x.shape
    n_ctx = ctx.shape[1]
    depth = w_mod.shape[0]
    t = n_ctx + seq
    h = jnp.concatenate([ctx, x], axis=1).astype(F32)
    cond = jnp.concatenate([jax.nn.silu(c), jax.nn.silu(c_ctx)[None, :]], axis=0)
    mods = [jnp.split(matmul(cond, w_mod[i]) + b_mod[i].astype(F32), N_MOD, axis=-1) for i in range(depth)]
    sh1, sc1 = mods[0][0], mods[0][1]
    a = _modulated(h, n_ctx, norm1[0], sc1[bsz], sh1[bsz], sc1[:bsz], sh1[:bsz])
    for i in range(depth):
        last = i == depth - 1
        j = i // 2
        _, _, g1, sh2, sc2, g2 = mods[i]
        if i % 2 == 0:
            ya, yb = _even_mixer(a, n_ctx, ev_w_in[j], s5_lam_re[j], s5_lam_im[j], s5_log_dt[j],
                                 s5_b_re[j], s5_b_im[j], s5_c_re[j], s5_c_im[j], s5_d[j], s5_glu_w[j],
                                 s5_glu_b[j], na_rpb[j])
            w_out = ev_w_out[j]
        else:
            ya, yb = _odd_mixer(a, n_ctx, od_w_in[j], rk_mu[j], rk_g_up[j], rk_w0[j], rk_w_up[j],
                                rk_a0[j], rk_a_up[j], rk_k_k[j], rk_k_a[j], rk_r_k[j], rk_ln_w[j], rk_ln_b[j],
                                gq_q_norm[j], gq_k_norm[j])
            w_out = od_w_out[j]
        h, f = out_proj_residual_norm(ya, yb, w_out, h, n_ctx, g1, norm2[i], sc2, sh2)
        moe_args = (moe_w_router[i], moe_b_router[i], moe_w1, moe_w2, i, moe_b1[i], moe_b2[i])
        if last:
            parts = moe_ffn(f[:, n_ctx:].reshape(bsz * seq, dm), *moe_args)
            y = sum(p.astype(F32) for p in parts).reshape(bsz, seq, dm)
            hx = h[:, n_ctx:] + g2[:bsz][:, None, :] * y
            return _rms(hx, final_norm).astype(x.dtype)
        parts = moe_ffn(f.reshape(bsz * t, dm), *moe_args)
        h, a = combine_residual_norm(parts, h, n_ctx, g2, norm1[i + 1], mods[i + 1][1], mods[i + 1][0])
```

```python
import functools
import math

import numpy as np
import jax
import jax.numpy as jnp
from jax import lax
from jax.experimental import pallas as pl
from jax.experimental.pallas import tpu as pltpu

F32 = jnp.float32
BF16 = jnp.bfloat16

V7X_LANES = 128
V7X_SUBLANES = 8
V7X_MXU_DIM = 256
V7X_VMEM_BYTES = 64 * 1024 * 1024
VMEM_LIMIT = 56 * 1024 * 1024

GRID_W = 64
N_MOD = 6
NORM_EPS = 1e-6
S5_GROUP = 16
S5_STATE = 64
S5_CHUNK = 16
NA_HEADS = 8
WIN_ROWS = 8
WIN_COLS = 16
NA_QROWS = 4
NA_KROWS = NA_QROWS + WIN_ROWS - 1
NA_BATCH = 2
RK_HEAD = 64
RK_DECAY_RANK = 64
RK_ICLR_RANK = 64
RK_GATE_RANK = 128
RK_GN_EPS = 64e-5
RK_CHUNK = 64
RK_TBLOCK = 256
GQ_HEAD_DIM = 128
ROPE_THETA = 10000.0
GQ_TQ = 128
GQ_TK = 1024
N_EXPERTS = 32
TOP_K = 4
SWIGLU_ALPHA = 1.702
SWIGLU_LIMIT = 7.0
MOE_TM = 256
NEG_BIG = -1e30
HI = lax.Precision.HIGHEST


def _params(*sem):
    return pltpu.CompilerParams(dimension_semantics=sem, vmem_limit_bytes=VMEM_LIMIT)


def _tile(n, target, align):
    best = None
    t = align
    while t <= min(n, target):
        if n % t == 0:
            best = t
        t += align
    return n if best is None else best


def _mm_kernel(x_ref, w_ref, o_ref):
    o_ref[...] = jnp.dot(x_ref[...], w_ref[...], preferred_element_type=F32).astype(o_ref.dtype)


def matmul(x, w, out_dtype=F32):
    m, k = x.shape
    n = w.shape[1]
    x = x.astype(BF16)
    w = w.astype(BF16)
    m_pad = -(-m // 16) * 16
    if m_pad != m:
        x = jnp.pad(x, ((0, m_pad - m), (0, 0)))
    tm = _tile(m_pad, 512, 16)
    tn = _tile(n, 2048, V7X_LANES)
    out = pl.pallas_call(
        _mm_kernel,
        grid=(n // tn, m_pad // tm),
        in_specs=[pl.BlockSpec((tm, k), lambda j, i: (i, 0)),
                  pl.BlockSpec((k, tn), lambda j, i: (0, j))],
        out_specs=pl.BlockSpec((tm, tn), lambda j, i: (i, j)),
        out_shape=jax.ShapeDtypeStruct((m_pad, n), out_dtype),
        compiler_params=_params("parallel", "parallel"),
        name="matmul",
    )(x, w)
    return out[:m] if m_pad != m else out


def _out_proj_kernel(xa_ref, xb_ref, wa_ref, wb_ref, h_ref, g_ref, nw_ref, sc_ref, sh_ref, hn_ref, f_ref):
    out = (jnp.dot(xa_ref[...], wa_ref[...], preferred_element_type=F32)
           + jnp.dot(xb_ref[...], wb_ref[...], preferred_element_type=F32))
    hn = h_ref[...] + g_ref[0] * out
    hn_ref[...] = hn
    normed = hn * lax.rsqrt(jnp.mean(hn * hn, axis=-1, keepdims=True) + NORM_EPS) * nw_ref[...]
    f_ref[...] = (normed * (1.0 + sc_ref[0]) + sh_ref[0]).astype(f_ref.dtype)


def out_proj_residual_norm(xa, xb, w_out, h, n_ctx, gate, norm_w, scale, shift):
    bsz, t, dm = h.shape
    wa = xa.shape[-1]
    tm = _tile(n_ctx, 256, 16)
    assert t % tm == 0
    nbt, ncb = t // tm, n_ctx // tm

    def mod_row(i):
        return jnp.where(i % nbt < ncb, bsz, i // nbt)

    rows = pl.BlockSpec((tm, dm), lambda i: (i, 0))
    mod = pl.BlockSpec((1, 1, dm), lambda i: (mod_row(i), 0, 0))
    m = bsz * t
    w_out = w_out.astype(BF16)
    hn, f = pl.pallas_call(
        _out_proj_kernel,
        grid=(m // tm,),
        in_specs=[pl.BlockSpec((tm, wa), lambda i: (i, 0)),
                  pl.BlockSpec((tm, dm - wa), lambda i: (i, 0)),
                  pl.BlockSpec((wa, dm), lambda i: (0, 0)),
                  pl.BlockSpec((dm - wa, dm), lambda i: (0, 0)),
                  rows, mod, pl.BlockSpec((1, dm), lambda i: (0, 0)), mod, mod],
        out_specs=[rows, rows],
        out_shape=[jax.ShapeDtypeStruct((m, dm), F32), jax.ShapeDtypeStruct((m, dm), BF16)],
        compiler_params=_params("parallel"),
        name="out_proj_residual_norm",
    )(xa.reshape(m, wa), xb.reshape(m, dm - wa), w_out[:wa], w_out[wa:], h.reshape(m, dm),
      gate[:, None, :], norm_w.astype(F32)[None, :], scale[:, None, :], shift[:, None, :])
    return hn.reshape(bsz, t, dm), f.reshape(bsz, t, dm)


def _flash_kernel(q_ref, k_ref, v_ref, o_ref, *, grp, hd, tq, tk, n_ctx, n_ctx_qblocks, n_lat_chunks):
    qi = pl.program_id(2)
    q = jnp.concatenate([q_ref[0, :, g * hd:(g + 1) * hd] for g in range(grp)], axis=0)
    scale = hd ** -0.5

    def scores(lo, size):
        return lax.dot_general(q, k_ref[0, lo:lo + size, :], (((1,), (1,)), ((), ())),
                               preferred_element_type=F32) * scale

    def write(acc, l):
        o = acc / l
        for g in range(grp):
            o_ref[0, :, g * hd:(g + 1) * hd] = o[g * tq:(g + 1) * tq].astype(o_ref.dtype)

    s = scores(0, n_ctx)
    m = jnp.max(s, axis=-1, keepdims=True)
    p = jnp.exp(s - m)
    l = jnp.sum(p, axis=-1, keepdims=True)
    acc = jnp.dot(p.astype(BF16), v_ref[0, 0:n_ctx, :], preferred_element_type=F32)

    @pl.when(qi < n_ctx_qblocks)
    def _():
        write(acc, l)

    @pl.when(qi >= n_ctx_qblocks)
    def _():
        lat = [scores(n_ctx + j * tk, tk) for j in range(n_lat_chunks)]
        m_run, l_run, acc_run = m, l, acc
        for j in range(n_lat_chunks):
            m_new = jnp.maximum(m_run, jnp.max(lat[j], axis=-1, keepdims=True))
            alpha = jnp.exp(m_run - m_new)
            pj = jnp.exp(lat[j] - m_new)
            l_run = alpha * l_run + jnp.sum(pj, axis=-1, keepdims=True)
            vj = v_ref[0, n_ctx + j * tk:n_ctx + (j + 1) * tk, :]
            acc_run = alpha * acc_run + jnp.dot(pj.astype(BF16), vj, preferred_element_type=F32)
            m_run = m_new
        write(acc_run, l_run)


def gqa_attention(qk, v, n_ctx, n_q_heads, n_kv_heads):
    b, t, _ = qk.shape
    hd = GQ_HEAD_DIM
    qw = n_q_heads * hd
    grp = n_q_heads // n_kv_heads
    tq = GQ_TQ
    tk = _tile(t - n_ctx, GQ_TK, V7X_LANES)
    assert n_ctx % tq == 0 and t % tq == 0 and n_ctx % V7X_LANES == 0
    kern = functools.partial(_flash_kernel, grp=grp, hd=hd, tq=tq, tk=tk, n_ctx=n_ctx,
                             n_ctx_qblocks=n_ctx // tq, n_lat_chunks=(t - n_ctx) // tk)
    return pl.pallas_call(
        kern,
        grid=(b, n_kv_heads, t // tq),
        in_specs=[pl.BlockSpec((1, tq, grp * hd), lambda bi, h, i: (bi, i, h)),
                  pl.BlockSpec((1, t, hd), lambda bi, h, i: (bi, 0, n_q_heads + h)),
                  pl.BlockSpec((1, t, hd), lambda bi, h, i: (bi, 0, h))],
        out_specs=pl.BlockSpec((1, tq, grp * hd), lambda bi, h, i: (bi, i, h)),
        out_shape=jax.ShapeDtypeStruct((b, t, qw), BF16),
        compiler_params=_params("parallel", "parallel", "parallel"),
        name="gqa_attention",
    )(qk, qk, v)


def _na_kernel(q_ref, k_ref, v_ref, bias_ref, o_ref, *, n_ctx, rows, n_blocks):
    j = pl.program_id(2)
    bs = range(q_ref.shape[0])
    dn = (((1,), (1,)), ((), ()))
    scale = q_ref.shape[-1] ** -0.5
    q = [q_ref[b] for b in bs]
    kc = [k_ref[b, 0:n_ctx, :] for b in bs]
    vc = [v_ref[b, 0:n_ctx, :] for b in bs]
    s_ctx = [lax.dot_general(q[b], kc[b], dn, preferred_element_type=F32) * scale for b in bs]

    def row_max(x):
        return jnp.max(x, axis=-1, keepdims=True)

    def row_sum(x):
        return jnp.sum(x, axis=-1, keepdims=True)

    def pv(p, v):
        return jnp.dot(p.astype(BF16), v, preferred_element_type=F32)

    @pl.when(j == 0)
    def _():
        p = [jnp.exp(s_ctx[b] - row_max(s_ctx[b])) for b in bs]
        for b in bs:
            o_ref[b] = (pv(p[b], vc[b]) / row_sum(p[b])).astype(o_ref.dtype)

    @pl.when(j > 0)
    def _():
        jj = j - 1
        ks = jnp.clip(NA_QROWS * jj - WIN_ROWS // 2, 0, rows - NA_KROWS)
        off = pl.multiple_of(n_ctx + ks * GRID_W, GRID_W)
        pat = jnp.where(jj == 0, 0, jnp.where(jj == n_blocks - 1, 2, 1))
        bias = bias_ref[pat]
        kw = [k_ref[b, pl.ds(off, NA_KROWS * GRID_W), :] for b in bs]
        vw = [v_ref[b, pl.ds(off, NA_KROWS * GRID_W), :] for b in bs]
        s_win = [lax.dot_general(q[b], kw[b], dn, preferred_element_type=F32) * scale + bias for b in bs]
        m = [jnp.maximum(row_max(s_win[b]), row_max(s_ctx[b])) for b in bs]
        p_win = [jnp.exp(s_win[b] - m[b]) for b in bs]
        p_ctx = [jnp.exp(s_ctx[b] - m[b]) for b in bs]
        for b in bs:
            l = row_sum(p_win[b]) + row_sum(p_ctx[b])
            o_ref[b] = ((pv(p_win[b], vw[b]) + pv(p_ctx[b], vc[b])) / l).astype(o_ref.dtype)


def _na_bias(rpb, rows):
    n_blocks = rows // NA_QROWS
    col = np.arange(GRID_W)
    col_start = np.clip(col - WIN_COLS // 2, 0, GRID_W - WIN_COLS)
    col_ok = (col[None, :] >= col_start[:, None]) & (col[None, :] < col_start[:, None] + WIN_COLS)
    dc = np.clip(col[None, :] - col[:, None] + (WIN_COLS - 1), 0, 2 * WIN_COLS - 2)
    pick_c = (dc[:, :, None] == np.arange(2 * WIN_COLS - 1)).astype(np.float32)
    pats = []
    for jj in (0, 1, n_blocks - 1):
        ks = min(max(NA_QROWS * jj - WIN_ROWS // 2, 0), rows - NA_KROWS)
        r = NA_QROWS * jj + np.arange(NA_QROWS)
        kr = ks + np.arange(NA_KROWS)
        r_start = np.clip(r - WIN_ROWS // 2, 0, rows - WIN_ROWS)
        row_ok = (kr[None, :] >= r_start[:, None]) & (kr[None, :] < r_start[:, None] + WIN_ROWS)
        dr = np.clip(kr[None, :] - r[:, None] + (WIN_ROWS - 1), 0, 2 * WIN_ROWS - 2)
        pick_r = (dr[:, :, None] == np.arange(2 * WIN_ROWS - 1)).astype(np.float32)
        bias = jnp.einsum('hab,rka,qcb->hrqkc', rpb.astype(F32), pick_r, pick_c, precision=HI)
        ok = row_ok[:, None, :, None] & col_ok[None, :, None, :]
        bias = jnp.where(ok[None], bias, NEG_BIG)
        pats.append(bias.reshape(rpb.shape[0], NA_QROWS * GRID_W, NA_KROWS * GRID_W))
    return jnp.stack(pats, axis=0)


def neighborhood_attention(qkv, rpb, n_ctx):
    b, t, width3 = qkv.shape
    width = width3 // 3
    hd = width // NA_HEADS
    seq = t - n_ctx
    rows = seq // GRID_W
    qb = NA_QROWS * GRID_W
    assert n_ctx == qb and rows % NA_QROWS == 0 and rows >= NA_KROWS and WIN_ROWS <= rows
    n_blocks = rows // NA_QROWS
    bias = _na_bias(rpb, rows)
    kern = functools.partial(_na_kernel, n_ctx=n_ctx, rows=rows, n_blocks=n_blocks)
    kb = NA_KROWS * GRID_W
    nbatch = NA_BATCH if b % NA_BATCH == 0 else 1
    return pl.pallas_call(
        kern,
        grid=(NA_HEADS, b // nbatch, n_blocks + 1),
        in_specs=[pl.BlockSpec((nbatch, qb, hd), lambda h, bi, j: (bi, j, h)),
                  pl.BlockSpec((nbatch, t, hd), lambda h, bi, j: (bi, 0, NA_HEADS + h)),
                  pl.BlockSpec((nbatch, t, hd), lambda h, bi, j: (bi, 0, 2 * NA_HEADS + h)),
                  pl.BlockSpec((3, None, qb, kb), lambda h, bi, j: (0, h, 0, 0))],
        out_specs=pl.BlockSpec((nbatch, qb, hd), lambda h, bi, j: (bi, j, h)),
        out_shape=jax.ShapeDtypeStruct((b, t, width), BF16),
        compiler_params=_params("parallel", "parallel", "parallel"),
        name="neighborhood_attention",
    )(qkv, qkv, qkv, bias)


def _s5_kernel(u_ref, t_ref, n_ref, m_ref, d_ref, y_ref, hl_sc, hp_sc, *, nc_ctx, nc_all, bsz):
    u = u_ref[0]
    for part in range(4):
        hl_sc[part] = jnp.dot(u, n_ref[0, part], preferred_element_type=F32)
    zero = jnp.zeros((bsz, S5_STATE), F32)

    def scan(direction):
        dr = d_ref[0, 2 * direction:2 * direction + 1, :]
        di = d_ref[0, 2 * direction + 1:2 * direction + 2, :]

        def body(step, carry):
            hr, hi = carry
            if direction == 0:
                c = step
            else:
                c = jnp.where(step < nc_ctx, nc_ctx - 1 - step, nc_all - 1 - (step - nc_ctx))
            r0 = pl.multiple_of(c * bsz, bsz)
            hp_sc[2 * direction, pl.ds(r0, bsz), :] = hr
            hp_sc[2 * direction + 1, pl.ds(r0, bsz), :] = hi
            lr = hl_sc[2 * direction, pl.ds(r0, bsz), :]
            li = hl_sc[2 * direction + 1, pl.ds(r0, bsz), :]
            return dr * hr - di * hi + lr, dr * hi + di * hr + li

        lax.fori_loop(0, nc_all, body, (zero, zero))

    scan(0)
    scan(1)
    y = jnp.dot(u, t_ref[0], preferred_element_type=F32)
    for part in range(4):
        y = y + jnp.dot(hp_sc[part].astype(BF16), m_ref[0, part], preferred_element_type=F32)
    y_ref[0] = y.astype(y_ref.dtype)


def _s5_tables(lam_re, lam_im, log_dt, b_re, b_im, c_re, c_im):
    c = S5_CHUNK
    g, p = lam_re.shape[1], lam_re.shape[2]
    h = b_re.shape[-1]
    toe = 0.0
    n_parts, m_parts, d_parts = [], [], []
    tt = jnp.arange(c)
    for d in range(2):
        lr, li = lam_re[d].astype(F32), lam_im[d].astype(F32)
        dt = jnp.exp(log_dt[d].astype(F32))[:, None]

        def lam_pow(e):
            e = jnp.asarray(e, F32)[..., None, None]
            mag = jnp.exp(e * lr * dt)
            return mag * jnp.cos(e * li * dt), mag * jnp.sin(e * li * dt)

        lbr, lbi = lam_pow(1)
        nr, ni = lbr - 1.0, lbi
        den = lr * lr + li * li
        fr, fi = (nr * lr + ni * li) / den, (ni * lr - nr * li) / den
        br, bi = b_re[d].astype(F32), b_im[d].astype(F32)
        bbr = fr[..., None] * br - fi[..., None] * bi
        bbi = fr[..., None] * bi + fi[..., None] * br
        cr, ci = c_re[d].astype(F32), c_im[d].astype(F32)
        pr, pi = lam_pow(tt)
        lbbr = pr[..., None] * bbr - pi[..., None] * bbi
        lbbi = pr[..., None] * bbi + pi[..., None] * bbr
        kern = (jnp.einsum('gop,tgpi->tgoi', cr, lbbr, precision=HI)
                - jnp.einsum('gop,tgpi->tgoi', ci, lbbi, precision=HI))
        steps = np.arange(c)
        lag = (steps[None, :] - steps[:, None]) if d == 0 else (steps[:, None] - steps[None, :])
        pick = (lag[:, :, None] == steps).astype(np.float32)
        kt = jnp.einsum('stx,xgoi->gsito', pick, kern, precision=HI)
        toe = toe + kt.reshape(g, c * h, c * h)
        e_in = (c - 1 - tt) if d == 0 else tt
        qr, qi = lam_pow(e_in)
        n_r = qr[..., None] * bbr - qi[..., None] * bbi
        n_i = qr[..., None] * bbi + qi[..., None] * bbr
        n_parts += [jnp.transpose(n_r, (1, 0, 3, 2)).reshape(g, c * h, p),
                    jnp.transpose(n_i, (1, 0, 3, 2)).reshape(g, c * h, p)]
        e_out = (tt + 1) if d == 0 else (c - tt)
        sr, si = lam_pow(e_out)
        clr = cr[None] * sr[:, :, None, :] - ci[None] * si[:, :, None, :]
        cli = cr[None] * si[:, :, None, :] + ci[None] * sr[:, :, None, :]
        m_parts += [jnp.transpose(clr, (1, 3, 0, 2)).reshape(g, p, c * h),
                    -jnp.transpose(cli, (1, 3, 0, 2)).reshape(g, p, c * h)]
        dcr, dci = lam_pow(c)
        d_parts += [dcr, dci]
    return (toe.astype(BF16), jnp.stack(n_parts, 1).astype(BF16), jnp.stack(m_parts, 1).astype(BF16),
            jnp.stack(d_parts, 1))


def s5_scan(u, n_ctx, lam_re, lam_im, log_dt, b_re, b_im, c_re, c_im):
    bsz, t, width = u.shape
    g = width // S5_GROUP
    c = S5_CHUNK
    assert bsz == V7X_SUBLANES and t % c == 0 and n_ctx % c == 0
    nc = t // c
    rows = nc * bsz
    toe, n_tab, m_tab, d_tab = _s5_tables(lam_re, lam_im, log_dt, b_re, b_im, c_re, c_im)
    uc = u.astype(BF16).reshape(bsz, nc, c, g, S5_GROUP)
    uc = jnp.transpose(uc, (3, 1, 0, 2, 4)).reshape(g, rows, c * S5_GROUP)
    kern = functools.partial(_s5_kernel, nc_ctx=n_ctx // c, nc_all=nc, bsz=bsz)
    cw = c * S5_GROUP
    y = pl.pallas_call(
        kern,
        grid=(g,),
        in_specs=[pl.BlockSpec((1, rows, cw), lambda i: (i, 0, 0)),
                  pl.BlockSpec((1, cw, cw), lambda i: (i, 0, 0)),
                  pl.BlockSpec((1, 4, cw, S5_STATE), lambda i: (i, 0, 0, 0)),
                  pl.BlockSpec((1, 4, S5_STATE, cw), lambda i: (i, 0, 0, 0)),
                  pl.BlockSpec((1, 4, S5_STATE), lambda i: (i, 0, 0))],
        out_specs=pl.BlockSpec((1, rows, cw), lambda i: (i, 0, 0)),
        out_shape=jax.ShapeDtypeStruct((g, rows, cw), BF16),
        scratch_shapes=[pltpu.VMEM((4, rows, S5_STATE), F32), pltpu.VMEM((4, rows, S5_STATE), F32)],
        compiler_params=_params("parallel"),
        name="s5_scan",
    )(uc, toe, n_tab, m_tab, d_tab)
    y = y.reshape(g, nc, bsz, c, S5_GROUP)
    return jnp.transpose(y, (2, 1, 3, 0, 4)).reshape(bsz, t, width)


def _rwkv_kernel(at_ref, rt_ref, bt_ref, kt_ref, v_ref, pc_ref, y_ref, s_sc, *, heads, c):
    n = pl.program_id(2)

    @pl.when(n == 0)
    def _():
        s_sc[...] = jnp.zeros(s_sc.shape, F32)

    row = lax.broadcasted_iota(jnp.int32, (c, c), 0)
    col = lax.broadcasted_iota(jnp.int32, (c, c), 1)
    ahead = (row - col) * jnp.where(pl.program_id(0) == 0, 1, -1)
    strict = ahead > 0
    incl = ahead >= 0
    nt = (((1,), (1,)), ((), ()))
    tn = (((0,), (0,)), ((), ()))
    n_factors = int(math.log2(c))
    hs = range(heads)
    sls = [slice(j * RK_HEAD, (j + 1) * RK_HEAD) for j in hs]

    def dot(a, b, dims=None):
        a, b = a.astype(BF16), b.astype(BF16)
        if dims is None:
            return jnp.dot(a, b, preferred_element_type=F32)
        return lax.dot_general(a, b, dims, preferred_element_type=F32)

    vv = [v_ref[0, :, sl] for sl in sls]
    s0 = [s_sc[j] for j in hs]
    ar = [jnp.concatenate([at_ref[0, 0, :, sl], rt_ref[0, 0, :, sl]], axis=0) for sl in sls]
    bk = [jnp.concatenate([bt_ref[0, 0, :, sl], kt_ref[0, 0, :, sl]], axis=0) for sl in sls]
    g = [dot(ar[j], bk[j], nt) for j in hs]
    ah = [dot(ar[j], s0[j], nt) for j in hs]
    x = [jnp.where(strict, g[j][:c, :c], 0.0) for j in hs]
    a_ak = [jnp.where(strict, g[j][:c, c:], 0.0) for j in hs]
    m = [jnp.concatenate([jnp.where(incl, g[j][c:, :c], 0.0), jnp.where(incl, g[j][c:, c:], 0.0)], axis=1)
         for j in hs]
    u = [ah[j][:c] + dot(a_ak[j], vv[j]) for j in hs]
    u = [u[j] + dot(x[j], u[j]) for j in hs]
    for _ in range(n_factors - 1):
        x = [dot(x[j], x[j]) for j in hs]
        u = [u[j] + dot(x[j], u[j]) for j in hs]
    uv = [jnp.concatenate([u[j].astype(BF16), vv[j]], axis=0) for j in hs]
    for j in hs:
        y_ref[0, 0, :, sls[j]] = ah[j][c:] + dot(m[j], uv[j])
    for j in hs:
        s_new = s0[j] + dot(uv[j], bk[j], tn)
        s_sc[j] = s_new * pc_ref[0, 0, 0, :, sls[j]]


def rwkv7_scan(at, rt, bt, kt, v, pc, n_ctx):
    _, bsz, t, width = at.shape
    heads = width // RK_HEAD
    c = RK_CHUNK
    assert t % c == 0 and n_ctx % c == 0
    ncc, nc = n_ctx // c, t // c

    def chunk(d, n):
        back = jnp.where(n < ncc, ncc - 1 - n, ncc + (nc - 1 - n))
        return jnp.where(d == 0, n, back)

    stream = pl.BlockSpec((1, 1, c, width), lambda d, bi, n: (d, bi, chunk(d, n), 0))
    kern = functools.partial(_rwkv_kernel, heads=heads, c=c)
    return pl.pallas_call(
        kern,
        grid=(2, bsz, nc),
        in_specs=[stream, stream, stream, stream,
                  pl.BlockSpec((1, c, width), lambda d, bi, n: (bi, chunk(d, n), 0)),
                  pl.BlockSpec((1, 1, 1, 1, width), lambda d, bi, n: (d, bi, chunk(d, n), 0, 0))],
        out_specs=stream,
        out_shape=jax.ShapeDtypeStruct((2, bsz, t, width), F32),
        scratch_shapes=[pltpu.VMEM((heads, RK_HEAD, RK_HEAD), F32)],
        compiler_params=_params("parallel", "parallel", "arbitrary"),
        name="rwkv7_scan",
    )(at, rt, bt, kt, v, pc)


def _deinterleave_kernel(w_ref, p_ref, o_ref):
    g = p_ref.shape[0]
    half = o_ref.shape[1] // 2
    for c in range(w_ref.shape[1] // g):
        t = jnp.dot(w_ref[:, c * g:(c + 1) * g].astype(BF16), p_ref[...], preferred_element_type=F32)
        o_ref[:, c * g // 2:(c + 1) * g // 2] = t[:, :g // 2].astype(o_ref.dtype)
        o_ref[:, half + c * g // 2:half + (c + 1) * g // 2] = t[:, g // 2:].astype(o_ref.dtype)


def deinterleave_columns(w_all, layer):
    _, r, n2 = w_all.shape
    g = V7X_MXU_DIM
    assert n2 % g == 0
    perm = np.concatenate([np.arange(0, g, 2), np.arange(1, g, 2)])
    p = jnp.asarray(np.arange(g)[:, None] == perm[None, :], BF16)
    tm = _tile(r, 1024, 16)
    w = w_all.reshape(-1, n2)
    first = layer * (r // tm)
    return pl.pallas_call(
        _deinterleave_kernel,
        grid=(r // tm,),
        in_specs=[pl.BlockSpec((tm, n2), lambda i: (first + i, 0)), pl.BlockSpec((g, g), lambda i: (0, 0))],
        out_specs=pl.BlockSpec((tm, n2), lambda i: (i, 0)),
        out_shape=jax.ShapeDtypeStruct((r, n2), BF16),
        compiler_params=_params("parallel"),
        name="deinterleave_columns",
    )(w, p)


def _moe_kernel(be_ref, nb_ref, x_ref, g_ref, w1_ref, b1g_ref, b1l_ref, w2_ref, b2_ref, o_ref, w2_sc):
    i = pl.program_id(0)
    de = w2_ref.shape[1]

    @pl.when((i == 0) | (be_ref[i] != be_ref[jnp.maximum(i - 1, 0)]))
    def _():
        w2_sc[...] = w2_ref[0].astype(w2_sc.dtype)

    @pl.when(i < nb_ref[0])
    def _():
        x = x_ref[...]
        z = jnp.dot(x, w1_ref[0], preferred_element_type=F32)
        zg = z[:, :de] + b1g_ref[0]
        zl = z[:, de:] + b1l_ref[0]
        zg = jnp.minimum(zg, SWIGLU_LIMIT)
        zl = jnp.clip(zl, -SWIGLU_LIMIT, SWIGLU_LIMIT)
        act = zg * jax.nn.sigmoid(SWIGLU_ALPHA * zg) * (zl + 1.0)
        y = jnp.dot(act.astype(BF16), w2_sc[...], preferred_element_type=F32) + b2_ref[0]
        o_ref[...] = (y * g_ref[...]).astype(o_ref.dtype)

    @pl.when(i >= nb_ref[0])
    def _():
        o_ref[...] = jnp.zeros(o_ref.shape, o_ref.dtype)


def moe_ffn(h, w_router, b_router, w1_all, w2_all, layer, b1, b2):
    n, dm = h.shape
    n_exp = w_router.shape[1]
    de = w2_all.shape[2]
    tm = MOE_TM
    logits = jnp.dot(h, w_router, precision=HI, preferred_element_type=F32) + b_router.astype(F32)
    top_val, top_idx = lax.top_k(logits, TOP_K)
    gates = jax.nn.softmax(top_val, axis=-1)
    flat_e = top_idx.reshape(-1).astype(jnp.int32)
    rb = V7X_LANES
    assert (n * TOP_K) % rb == 0
    onehot = (flat_e[:, None] == jnp.arange(n_exp, dtype=jnp.int32)[None, :]).astype(F32)
    blocks = onehot.reshape(-1, rb, n_exp)
    tri = np.tril(np.ones((rb, rb), np.float32))
    within = jnp.einsum('ij,bjk->bik', tri, blocks, precision=HI)
    totals = within[:, -1, :]
    before = jnp.cumsum(totals, axis=0) - totals
    csum = within + before[:, None, :]
    rank = (jnp.sum(csum * blocks, axis=-1).reshape(-1) - 1.0).astype(jnp.int32)
    counts = (before[-1] + totals[-1]).astype(jnp.int32)
    padded = (counts + tm - 1) // tm * tm
    pad_end = jnp.cumsum(padded)
    pad_start = pad_end - padded
    dest = pad_start[flat_e] + rank
    n_blocks = -(-(n * TOP_K) // tm) + n_exp
    cap = n_blocks * tm
    slot_src = jnp.full((cap,), -1, jnp.int32).at[dest].set(jnp.arange(n * TOP_K, dtype=jnp.int32))
    filled = slot_src >= 0
    slot_tok = jnp.where(filled, slot_src // TOP_K, 0)
    slot_gate = jnp.where(filled, gates.reshape(-1)[jnp.maximum(slot_src, 0)], 0.0)
    block_start = jnp.arange(n_blocks, dtype=jnp.int32) * tm
    block_exp = jnp.minimum(jnp.sum(pad_end[None, :] <= block_start[:, None], axis=1), n_exp - 1).astype(jnp.int32)
    n_used = (pad_end[-1] // tm).astype(jnp.int32).reshape(1)
    x_sorted = h.astype(BF16)[slot_tok]

    w1p = deinterleave_columns(w1_all.reshape(-1, n_exp * dm, 2 * de), layer).reshape(n_exp, dm, 2 * de)
    b1g = b1[:, None, 0::2].astype(F32)
    b1l = b1[:, None, 1::2].astype(F32)
    b2r = b2[:, None, :].astype(F32)

    grid_spec = pltpu.PrefetchScalarGridSpec(
        num_scalar_prefetch=2,
        grid=(n_blocks,),
        in_specs=[pl.BlockSpec((tm, dm), lambda i, be, nb: (i, 0)),
                  pl.BlockSpec((tm, 1), lambda i, be, nb: (i, 0)),
                  pl.BlockSpec((1, dm, 2 * de), lambda i, be, nb: (be[i], 0, 0)),
                  pl.BlockSpec((1, 1, de), lambda i, be, nb: (be[i], 0, 0)),
                  pl.BlockSpec((1, 1, de), lambda i, be, nb: (be[i], 0, 0)),
                  pl.BlockSpec((1, de, dm), lambda i, be, nb: (layer * n_exp + be[i], 0, 0)),
                  pl.BlockSpec((1, 1, dm), lambda i, be, nb: (be[i], 0, 0))],
        out_specs=pl.BlockSpec((tm, dm), lambda i, be, nb: (i, 0)),
        scratch_shapes=[pltpu.VMEM((de, dm), BF16)],
    )
    yb = pl.pallas_call(
        _moe_kernel,
        grid_spec=grid_spec,
        out_shape=jax.ShapeDtypeStruct((cap, dm), BF16),
        compiler_params=_params("arbitrary"),
        name="moe_experts",
    )(block_exp, n_used, x_sorted, slot_gate[:, None], w1p, b1g, b1l, w2_all.reshape(-1, de, dm), b2r)
    slots = dest.reshape(n, TOP_K)
    return [yb[slots[:, kk]] for kk in range(TOP_K)]


def _combine_kernel(*refs, n_parts):
    parts, (h_ref, g_ref, nw_ref, sc_ref, sh_ref, hn_ref, a_ref) = refs[:n_parts], refs[n_parts:]
    y = parts[0][...].astype(F32)
    for p in parts[1:]:
        y = y + p[...].astype(F32)
    hn = h_ref[...] + g_ref[0] * y
    hn_ref[...] = hn
    normed = hn * lax.rsqrt(jnp.mean(hn * hn, axis=-1, keepdims=True) + NORM_EPS) * nw_ref[...]
    a_ref[...] = (normed * (1.0 + sc_ref[0]) + sh_ref[0]).astype(a_ref.dtype)


def combine_residual_norm(parts, h, n_ctx, gate, norm_w, scale, shift):
    bsz, t, dm = h.shape
    tm = _tile(n_ctx, 256, 16)
    assert t % tm == 0
    nbt, ncb = t // tm, n_ctx // tm
    m = bsz * t
    rows = pl.BlockSpec((tm, dm), lambda i: (i, 0))
    mod = pl.BlockSpec((1, 1, dm), lambda i: (jnp.where(i % nbt < ncb, bsz, i // nbt), 0, 0))
    hn, a = pl.pallas_call(
        functools.partial(_combine_kernel, n_parts=len(parts)),
        grid=(m // tm,),
        in_specs=[rows] * len(parts) + [rows, mod, pl.BlockSpec((1, dm), lambda i: (0, 0)), mod, mod],
        out_specs=[rows, rows],
        out_shape=[jax.ShapeDtypeStruct((m, dm), F32), jax.ShapeDtypeStruct((m, dm), BF16)],
        compiler_params=_params("parallel"),
        name="combine_residual_norm",
    )(*parts, h.reshape(m, dm), gate[:, None, :], norm_w.astype(F32)[None, :], scale[:, None, :],
      shift[:, None, :])
    return hn.reshape(bsz, t, dm), a.reshape(bsz, t, dm)


def _rms(x, g, eps=NORM_EPS):
    xf = x.astype(F32)
    return xf * lax.rsqrt(jnp.mean(xf * xf, axis=-1, keepdims=True) + eps) * g.astype(F32)


def _modulated(h, n_ctx, norm_w, scale_c, shift_c, scale_x, shift_x):
    hn = _rms(h, norm_w)
    ac = hn[:, :n_ctx] * (1.0 + scale_c) + shift_c
    ax = hn[:, n_ctx:] * (1.0 + scale_x[:, None, :]) + shift_x[:, None, :]
    return jnp.concatenate([ac, ax], axis=1)


def _even_mixer(a, n_ctx, w_in, lam_re, lam_im, log_dt, b_re, b_im, c_re, c_im, d_skip,
                glu_w, glu_b, rpb):
    bsz, t, dm = a.shape
    mix_w = dm // 2
    a2 = a.reshape(bsz * t, dm).astype(BF16)
    u = matmul(a2, w_in[:, :mix_w]).reshape(bsz, t, mix_w)
    qkv = matmul(a2, w_in[:, mix_w:], out_dtype=BF16).reshape(bsz, t, 3 * mix_w)
    y = d_skip.astype(F32) * u + s5_scan(u, n_ctx, lam_re, lam_im, log_dt, b_re, b_im, c_re, c_im)
    gl = jax.nn.gelu(y)
    ya = gl * jax.nn.sigmoid(matmul(gl.reshape(bsz * t, mix_w), glu_w).reshape(bsz, t, mix_w)
                             + glu_b.astype(F32))
    yb = neighborhood_attention(qkv, rpb, n_ctx)
    return ya.astype(BF16), yb


def _rope_tables(n_ctx, seq, hd):
    half = hd // 2
    inv = ROPE_THETA ** (-jnp.arange(0, half, 2, dtype=F32) / half)
    pos = jnp.arange(seq)
    ang_r = (pos // GRID_W).astype(F32)[:, None] * inv[None, :]
    ang_c = (pos % GRID_W).astype(F32)[:, None] * inv[None, :]
    cos = jnp.concatenate([jnp.cos(ang_r), jnp.cos(ang_r), jnp.cos(ang_c), jnp.cos(ang_c)], axis=-1)
    sin = jnp.concatenate([-jnp.sin(ang_r), jnp.sin(ang_r), -jnp.sin(ang_c), jnp.sin(ang_c)], axis=-1)
    cos = jnp.concatenate([jnp.ones((n_ctx, hd), F32), cos], axis=0)
    sin = jnp.concatenate([jnp.zeros((n_ctx, hd), F32), sin], axis=0)
    return cos, sin


def _qk_prep_kernel(z_ref, cos_ref, sin_ref, gq_ref, gk_ref, o_ref, *, n_q_heads, hd):
    cosv = cos_ref[...]
    sinv = sin_ref[...]
    lane = lax.broadcasted_iota(jnp.int32, (1, hd), 1)
    first_quarter = (lane % (hd // 2)) < hd // 4
    for h in range(z_ref.shape[2] // hd):
        x = z_ref[0, :, h * hd:(h + 1) * hd]
        g = gq_ref[...] if h < n_q_heads else gk_ref[...]
        xn = x * lax.rsqrt(jnp.mean(x * x, axis=-1, keepdims=True) + NORM_EPS) * g
        partner = jnp.where(first_quarter, pltpu.roll(xn, hd - hd // 4, axis=1), pltpu.roll(xn, hd // 4, axis=1))
        o_ref[0, :, h * hd:(h + 1) * hd] = (xn * cosv + partner * sinv).astype(o_ref.dtype)


def qk_prepare(zqk, n_ctx, n_q_heads, q_norm, k_norm):
    bsz, t, width = zqk.shape
    hd = GQ_HEAD_DIM
    tm = _tile(t, 256, 16)
    cos, sin = _rope_tables(n_ctx, t - n_ctx, hd)
    rows = pl.BlockSpec((1, tm, width), lambda b, i: (b, i, 0))
    tab = pl.BlockSpec((tm, hd), lambda b, i: (i, 0))
    gain = pl.BlockSpec((1, hd), lambda b, i: (0, 0))
    return pl.pallas_call(
        functools.partial(_qk_prep_kernel, n_q_heads=n_q_heads, hd=hd),
        grid=(bsz, t // tm),
        in_specs=[rows, tab, tab, gain, gain],
        out_specs=rows,
        out_shape=jax.ShapeDtypeStruct((bsz, t, width), BF16),
        compiler_params=_params("parallel", "parallel"),
        name="qk_prepare",
    )(zqk, cos, sin, q_norm.astype(F32)[None, :], k_norm.astype(F32)[None, :])


def _split_bf16(x, terms):
    parts = []
    for _ in range(terms):
        p = x.astype(BF16)
        parts.append(p)
        x = x - p.astype(F32)
    return parts


def _head_sums(x, ones_ref):
    gw = ones_ref.shape[0]
    hi, lo = _split_bf16(x, 2)
    outs = []
    for g in range(x.shape[1] // gw):
        sl = slice(g * gw, (g + 1) * gw)
        outs.append(jnp.dot(hi[:, sl], ones_ref[...], preferred_element_type=F32)
                    + jnp.dot(lo[:, sl], ones_ref[...], preferred_element_type=F32))
    return jnp.concatenate(outs, axis=1)


def _rwkv_prep_kernel(z_ref, zp_ref, zn_ref, mu_ref, gup_ref, wup_ref, aup_ref, vec_ref, rk_ref, tri_ref,
                      ones_ref, at_ref, rt_ref, bt_ref, kt_ref, v_ref, pc_ref, gate_ref, bonus_ref,
                      *, width, tb, c, ncb, nb):
    i = pl.program_id(1)
    keep_prev = jnp.where((i == 0) | (i == ncb), 0.0, 1.0)
    keep_next = jnp.where((i == ncb - 1) | (i == nb - 1), 0.0, 1.0)
    z = z_ref[0]
    rid = lax.broadcasted_iota(jnp.int32, (tb, 1), 0)
    zp = jnp.where(rid == 0, zp_ref[0, 7:8, :] * keep_prev, pltpu.roll(z, 1, axis=0))
    zn = jnp.where(rid == tb - 1, zn_ref[0, 0:1, :] * keep_next, pltpu.roll(z, tb - 1, axis=0))
    zs = z + mu_ref[0:1, :] * (zp - z) + mu_ref[1:2, :] * (zn - z)
    w = width
    r, k, v = zs[:, :w], zs[:, w:2 * w], zs[:, 2 * w:3 * w]
    g_lo = zs[:, 3 * w:3 * w + RK_GATE_RANK]
    o = 3 * w + RK_GATE_RANK
    wl = jnp.tanh(zs[:, o:o + 2 * RK_DECAY_RANK]).astype(BF16)
    al = zs[:, o + 2 * RK_DECAY_RANK:o + 2 * RK_DECAY_RANK + 2 * RK_ICLR_RANK].astype(BF16)
    v_ref[0] = v.astype(v_ref.dtype)
    gate_ref[0] = jnp.dot(jax.nn.sigmoid(g_lo).astype(BF16), gup_ref[...], preferred_element_type=F32)
    ksum = jnp.zeros_like(k)
    for d in range(2):
        w0, a0 = vec_ref[d, 0:1, :], vec_ref[d, 1:2, :]
        k_k, k_a = vec_ref[d, 2:3, :], vec_ref[d, 3:4, :]
        neg = -(w0 + jnp.dot(wl, wup_ref[d], preferred_element_type=F32))
        softplus = jnp.maximum(neg, 0.0) + jnp.log(1.0 + jnp.exp(-jnp.abs(neg)))
        lw = -jnp.exp(-softplus - 0.5)
        iclr = jax.nn.sigmoid(a0 + jnp.dot(al, aup_ref[d], preferred_element_type=F32))
        kk = k * k_k
        kk = kk / jnp.maximum(jnp.sqrt(_head_sums(kk * kk, ones_ref)), 1e-12)
        k_d = k * (1.0 + (iclr - 1.0) * k_a)
        ksum = ksum + k_d
        cum = sum(jnp.dot(tri_ref[d], part, preferred_element_type=F32) for part in _split_bf16(lw, 3))
        p_inv = jnp.exp(-cum)
        at_ref[d, 0] = (-kk * jnp.exp(cum - lw)).astype(at_ref.dtype)
        rt_ref[d, 0] = (r * jnp.exp(cum)).astype(rt_ref.dtype)
        bt_ref[d, 0] = (kk * iclr * p_inv).astype(bt_ref.dtype)
        kt_ref[d, 0] = (k_d * p_inv).astype(kt_ref.dtype)
        for j in range(tb // c):
            row = j * c + (c - 1 if d == 0 else 0)
            pc_ref[d, 0, j] = jnp.exp(cum[row:row + 1, :])
    bonus_ref[0] = _head_sums(r * ksum * rk_ref[...], ones_ref) * v


def rwkv7_prepare(zr, n_ctx, mu, g_up, w0, w_up, a0, a_up, k_k, k_a, r_k):
    bsz, t, rk_in = zr.shape
    width = g_up.shape[1]
    c, tb = RK_CHUNK, RK_TBLOCK
    assert n_ctx % tb == 0 and t % tb == 0 and tb % c == 0 and tb % V7X_SUBLANES == 0
    ncb, nb = n_ctx // tb, t // tb
    rows_per_tile = V7X_SUBLANES

    def pad_rows(up, d, rank):
        return jnp.pad(up, ((d * rank, (1 - d) * rank), (0, 0)))

    wup = jnp.stack([pad_rows(w_up[d], d, RK_DECAY_RANK) for d in range(2)]).astype(BF16)
    aup = jnp.stack([pad_rows(a_up[d], d, RK_ICLR_RANK) for d in range(2)]).astype(BF16)
    vec = jnp.stack([w0, a0, k_k, k_a], axis=1).astype(F32)
    step = np.arange(tb)
    same = (step[:, None] // c) == (step[None, :] // c)
    tri = np.stack([same & (step[None, :] <= step[:, None]), same & (step[None, :] >= step[:, None])])
    seg = np.arange(V7X_MXU_DIM) // RK_HEAD
    ones = (seg[:, None] == seg[None, :])

    def full(shape):
        return pl.BlockSpec(shape, lambda b, i: (0,) * len(shape))

    per_dir = pl.BlockSpec((2, 1, tb, width), lambda b, i: (0, b, i, 0))
    rows = pl.BlockSpec((1, tb, width), lambda b, i: (b, i, 0))
    kern = functools.partial(_rwkv_prep_kernel, width=width, tb=tb, c=c, ncb=ncb, nb=nb)
    stream = jax.ShapeDtypeStruct((2, bsz, t, width), BF16)
    return pl.pallas_call(
        kern,
        grid=(bsz, nb),
        in_specs=[pl.BlockSpec((1, tb, rk_in), lambda b, i: (b, i, 0)),
                  pl.BlockSpec((1, rows_per_tile, rk_in),
                               lambda b, i: (b, jnp.maximum(i * (tb // rows_per_tile) - 1, 0), 0)),
                  pl.BlockSpec((1, rows_per_tile, rk_in),
                               lambda b, i: (b, jnp.minimum((i + 1) * (tb // rows_per_tile),
                                                            t // rows_per_tile - 1), 0)),
                  full((2, rk_in)), full((RK_GATE_RANK, width)), full((2, 2 * RK_DECAY_RANK, width)),
                  full((2, 2 * RK_ICLR_RANK, width)), full((2, 4, width)), full((1, width)),
                  full((2, tb, tb)), full((V7X_MXU_DIM, V7X_MXU_DIM))],
        out_specs=[per_dir, per_dir, per_dir, per_dir, rows,
                   pl.BlockSpec((2, 1, tb // c, 1, width), lambda b, i: (0, b, i, 0, 0)), rows, rows],
        out_shape=[stream, stream, stream, stream, jax.ShapeDtypeStruct((bsz, t, width), BF16),
                   jax.ShapeDtypeStruct((2, bsz, t // c, 1, width), F32),
                   jax.ShapeDtypeStruct((bsz, t, width), F32), jax.ShapeDtypeStruct((bsz, t, width), F32)],
        compiler_params=_params("parallel", "parallel"),
        name="rwkv7_prepare",
    )(zr, zr, zr, mu.astype(F32), g_up.astype(BF16), wup, aup, vec, r_k.astype(F32).reshape(1, width),
      jnp.asarray(tri, BF16), jnp.asarray(ones, BF16))


def _rwkv_finish_kernel(ys_ref, bonus_ref, gate_ref, lnw_ref, lnb_ref, ones_ref, o_ref):
    y = ys_ref[0, 0] + ys_ref[1, 0]
    inv_n = 1.0 / RK_HEAD
    mean = _head_sums(y, ones_ref) * inv_n
    dev = y - mean
    var = _head_sums(dev * dev, ones_ref) * inv_n
    yn = dev * lax.rsqrt(var + RK_GN_EPS) * lnw_ref[...] + lnb_ref[...]
    o_ref[0] = ((yn + bonus_ref[0]) * gate_ref[0]).astype(o_ref.dtype)


def rwkv7_finish(ys, bonus, gate, ln_w, ln_b):
    _, bsz, t, width = ys.shape
    tb = RK_TBLOCK
    seg = np.arange(V7X_MXU_DIM) // RK_HEAD
    ones = (seg[:, None] == seg[None, :])
    rows = pl.BlockSpec((1, tb, width), lambda b, i: (b, i, 0))
    vec = pl.BlockSpec((1, width), lambda b, i: (0, 0))
    return pl.pallas_call(
        _rwkv_finish_kernel,
        grid=(bsz, t // tb),
        in_specs=[pl.BlockSpec((2, 1, tb, width), lambda b, i: (0, b, i, 0)), rows, rows, vec, vec,
                  pl.BlockSpec((V7X_MXU_DIM, V7X_MXU_DIM), lambda b, i: (0, 0))],
        out_specs=rows,
        out_shape=jax.ShapeDtypeStruct((bsz, t, width), BF16),
        compiler_params=_params("parallel", "parallel"),
        name="rwkv7_finish",
    )(ys, bonus, gate, ln_w.astype(F32)[None, :], ln_b.astype(F32)[None, :], jnp.asarray(ones, BF16))


def _odd_mixer(a, n_ctx, w_in, mu, g_up, w0, w_up, a0, a_up, k_k, k_a, r_k, ln_w, ln_b,
               q_norm, k_norm):
    bsz, t, dm = a.shape
    mix_w = dm // 2
    heads = mix_w // RK_HEAD
    rk_in = 3 * mix_w + RK_GATE_RANK + 2 * RK_DECAY_RANK + 2 * RK_ICLR_RANK
    gq_heads = mix_w // GQ_HEAD_DIM
    kv_heads = gq_heads // 4
    kv_w = kv_heads * GQ_HEAD_DIM
    a2 = a.reshape(bsz * t, dm).astype(BF16)
    zr = matmul(a2, w_in[:, :rk_in]).reshape(bsz, t, rk_in)
    zqk = matmul(a2, w_in[:, rk_in:rk_in + mix_w + kv_w]).reshape(bsz, t, mix_w + kv_w)
    vv = matmul(a2, w_in[:, rk_in + mix_w + kv_w:], out_dtype=BF16).reshape(bsz, t, kv_w)

    at, rt, bt, kt, v, pc, gate, bonus = rwkv7_prepare(zr, n_ctx, mu, g_up, w0, w_up, a0, a_up, k_k, k_a,
                                                       r_k.reshape(-1))
    ys = rwkv7_scan(at, rt, bt, kt, v, pc, n_ctx)
    yc = rwkv7_finish(ys, bonus, gate, ln_w, ln_b)
    qk = qk_prepare(zqk, n_ctx, gq_heads, q_norm, k_norm)
    yd = gqa_attention(qk, vv, n_ctx, gq_heads, kv_heads)
    return yc, yd


def kernel(x, c, ctx, c_ctx, w_mod, b_mod, norm1, norm2, final_norm, ev_w_in, ev_w_out, s5_lam_re, s5_lam_im, s5_log_dt, s5_b_re, s5_b_im, s5_c_re, s5_c_im, s5_d, s5_glu_w, s5_glu_b, na_rpb, od_w_in, od_w_out, rk_mu, rk_g_up, rk_w0, rk_w_up, rk_a0, rk_a_up, rk_k_k, rk_k_a, rk_r_k, rk_ln_w, rk_ln_b, gq_q_norm, gq_k_norm, moe_w_router, moe_b_router, moe_w1, moe_b1, moe_w2, moe_b2):
    bsz, seq, dm = x.shape
    n_ctx = ctx.shape[1]
    depth = w_mod.shape[0]
    t = n_ctx + seq
    h = jnp.concatenate([ctx, x], axis=1).astype(F32)
    cond = jnp.concatenate([jax.nn.silu(c), jax.nn.silu(c_ctx)[None, :]], axis=0)
    mods = [jnp.split(matmul(cond, w_mod[i]) + b_mod[i].astype(F32), N_MOD, axis=-1) for i in range(depth)]
    sh1, sc1 = mods[0][0], mods[0][1]
    a = _modulated(h, n_ctx, norm1[0], sc1[bsz], sh1[bsz], sc1[:bsz], sh1[:bsz])
    for i in range(depth):
        last = i == depth - 1
        j = i // 2
        _, _, g1, sh2, sc2, g2 = mods[i]
        if i % 2 == 0:
            ya, yb = _even_mixer(a, n_ctx, ev_w_in[j], s5_lam_re[j], s5_lam_im[j], s5_log_dt[j],
                                 s5_b_re[j], s5_b_im[j], s5_c_re[j], s5_c_im[j], s5_d[j], s5_glu_w[j],
                                 s5_glu_b[j], na_rpb[j])
            w_out = ev_w_out[j]
        else:
            ya, yb = _odd_mixer(a, n_ctx, od_w_in[j], rk_mu[j], rk_g_up[j], rk_w0[j], rk_w_up[j],
                                rk_a0[j], rk_a_up[j], rk_k_k[j], rk_k_a[j], rk_r_k[j], rk_ln_w[j], rk_ln_b[j],
                                gq_q_norm[j], gq_k_norm[j])
            w_out = od_w_out[j]
        h, f = out_proj_residual_norm(ya, yb, w_out, h, n_ctx, g1, norm2[i], sc2, sh2)
        moe_args = (moe_w_router[i], moe_b_router[i], moe_w1, moe_w2, i, moe_b1[i], moe_b2[i])
        if last:
            parts = moe_ffn(f[:, n_ctx:].reshape(bsz * seq, dm), *moe_args)
            y = sum(p.astype(F32) for p in parts).reshape(bsz, seq, dm)
            hx = h[:, n_ctx:] + g2[:bsz][:, None, :] * y
            return _rms(hx, final_norm).astype(x.dtype)
        parts = moe_ffn(f.reshape(bsz * t, dm), *moe_args)
        h, a = combine_residual_norm(parts, h, n_ctx, g2, norm1[i + 1], mods[i + 1][1], mods[i + 1][0])
```

```python
import functools
import math

import numpy as np
import jax
import jax.numpy as jnp
from jax import lax
from jax.experimental import pallas as pl
from jax.experimental.pallas import tpu as pltpu

F32 = jnp.float32
BF16 = jnp.bfloat16

V7X_LANES = 128
V7X_SUBLANES = 8
V7X_MXU_DIM = 256
V7X_VMEM_BYTES = 64 * 1024 * 1024
VMEM_LIMIT = 56 * 1024 * 1024

GRID_W = 64
N_MOD = 6
NORM_EPS = 1e-6
S5_GROUP = 16
S5_STATE = 64
S5_CHUNK = 16
NA_HEADS = 8
WIN_ROWS = 8
WIN_COLS = 16
NA_QROWS = 4
NA_KROWS = NA_QROWS + WIN_ROWS - 1
NA_BATCH = 2
RK_HEAD = 64
RK_DECAY_RANK = 64
RK_ICLR_RANK = 64
RK_GATE_RANK = 128
RK_GN_EPS = 64e-5
RK_CHUNK = 64
RK_TBLOCK = 256
GQ_HEAD_DIM = 128
ROPE_THETA = 10000.0
GQ_TQ = 128
GQ_TK = 1024
N_EXPERTS = 32
TOP_K = 4
SWIGLU_ALPHA = 1.702
SWIGLU_LIMIT = 7.0
MOE_TM = 256
NEG_BIG = -1e30
HI = lax.Precision.HIGHEST


def _params(*sem):
    return pltpu.CompilerParams(dimension_semantics=sem, vmem_limit_bytes=VMEM_LIMIT)


def _tile(n, target, align):
    best = None
    t = align
    while t <= min(n, target):
        if n % t == 0:
            best = t
        t += align
    return n if best is None else best


def _mm_kernel(x_ref, w_ref, o_ref):
    o_ref[...] = jnp.dot(x_ref[...], w_ref[...], preferred_element_type=F32).astype(o_ref.dtype)


def matmul(x, w, out_dtype=F32):
    m, k = x.shape
    n = w.shape[1]
    x = x.astype(BF16)
    w = w.astype(BF16)
    m_pad = -(-m // 16) * 16
    if m_pad != m:
        x = jnp.pad(x, ((0, m_pad - m), (0, 0)))
    tm = _tile(m_pad, 512, 16)
    tn = _tile(n, 2048, V7X_LANES)
    out = pl.pallas_call(
        _mm_kernel,
        grid=(n // tn, m_pad // tm),
        in_specs=[pl.BlockSpec((tm, k), lambda j, i: (i, 0)),
                  pl.BlockSpec((k, tn), lambda j, i: (0, j))],
        out_specs=pl.BlockSpec((tm, tn), lambda j, i: (i, j)),
        out_shape=jax.ShapeDtypeStruct((m_pad, n), out_dtype),
        compiler_params=_params("parallel", "parallel"),
        name="matmul",
    )(x, w)
    return out[:m] if m_pad != m else out


def _out_proj_kernel(xa_ref, xb_ref, wa_ref, wb_ref, h_ref, g_ref, nw_ref, sc_ref, sh_ref, hn_ref, f_ref):
    out = (jnp.dot(xa_ref[...], wa_ref[...], preferred_element_type=F32)
           + jnp.dot(xb_ref[...], wb_ref[...], preferred_element_type=F32))
    hn = h_ref[...] + g_ref[0] * out
    hn_ref[...] = hn
    normed = hn * lax.rsqrt(jnp.mean(hn * hn, axis=-1, keepdims=True) + NORM_EPS) * nw_ref[...]
    f_ref[...] = (normed * (1.0 + sc_ref[0]) + sh_ref[0]).astype(f_ref.dtype)


def out_proj_residual_norm(xa, xb, w_out, h, n_ctx, gate, norm_w, scale, shift):
    bsz, t, dm = h.shape
    wa = xa.shape[-1]
    tm = _tile(n_ctx, 256, 16)
    assert t % tm == 0
    nbt, ncb = t // tm, n_ctx // tm

    def mod_row(i):
        return jnp.where(i % nbt < ncb, bsz, i // nbt)

    rows = pl.BlockSpec((tm, dm), lambda i: (i, 0))
    mod = pl.BlockSpec((1, 1, dm), lambda i: (mod_row(i), 0, 0))
    m = bsz * t
    w_out = w_out.astype(BF16)
    hn, f = pl.pallas_call(
        _out_proj_kernel,
        grid=(m // tm,),
        in_specs=[pl.BlockSpec((tm, wa), lambda i: (i, 0)),
                  pl.BlockSpec((tm, dm - wa), lambda i: (i, 0)),
                  pl.BlockSpec((wa, dm), lambda i: (0, 0)),
                  pl.BlockSpec((dm - wa, dm), lambda i: (0, 0)),
                  rows, mod, pl.BlockSpec((1, dm), lambda i: (0, 0)), mod, mod],
        out_specs=[rows, rows],
        out_shape=[jax.ShapeDtypeStruct((m, dm), F32), jax.ShapeDtypeStruct((m, dm), BF16)],
        compiler_params=_params("parallel"),
        name="out_proj_residual_norm",
    )(xa.reshape(m, wa), xb.reshape(m, dm - wa), w_out[:wa], w_out[wa:], h.reshape(m, dm),
      gate[:, None, :], norm_w.astype(F32)[None, :], scale[:, None, :], shift[:, None, :])
    return hn.reshape(bsz, t, dm), f.reshape(bsz, t, dm)


def _flash_kernel(q_ref, k_ref, v_ref, o_ref, *, grp, hd, tq, tk, n_ctx, n_ctx_qblocks, n_lat_chunks):
    qi = pl.program_id(2)
    q = jnp.concatenate([q_ref[0, :, g * hd:(g + 1) * hd] for g in range(grp)], axis=0)
    scale = hd ** -0.5

    def scores(lo, size):
        return lax.dot_general(q, k_ref[0, lo:lo + size, :], (((1,), (1,)), ((), ())),
                               preferred_element_type=F32) * scale

    def write(acc, l):
        o = acc / l
        for g in range(grp):
            o_ref[0, :, g * hd:(g + 1) * hd] = o[g * tq:(g + 1) * tq].astype(o_ref.dtype)

    s = scores(0, n_ctx)
    m = jnp.max(s, axis=-1, keepdims=True)
    p = jnp.exp(s - m)
    l = jnp.sum(p, axis=-1, keepdims=True)
    acc = jnp.dot(p.astype(BF16), v_ref[0, 0:n_ctx, :], preferred_element_type=F32)

    @pl.when(qi < n_ctx_qblocks)
    def _():
        write(acc, l)

    @pl.when(qi >= n_ctx_qblocks)
    def _():
        lat = [scores(n_ctx + j * tk, tk) for j in range(n_lat_chunks)]
        m_run, l_run, acc_run = m, l, acc
        for j in range(n_lat_chunks):
            m_new = jnp.maximum(m_run, jnp.max(lat[j], axis=-1, keepdims=True))
            alpha = jnp.exp(m_run - m_new)
            pj = jnp.exp(lat[j] - m_new)
            l_run = alpha * l_run + jnp.sum(pj, axis=-1, keepdims=True)
            vj = v_ref[0, n_ctx + j * tk:n_ctx + (j + 1) * tk, :]
            acc_run = alpha * acc_run + jnp.dot(pj.astype(BF16), vj, preferred_element_type=F32)
            m_run = m_new
        write(acc_run, l_run)


def gqa_attention(qk, v, n_ctx, n_q_heads, n_kv_heads):
    b, t, _ = qk.shape
    hd = GQ_HEAD_DIM
    qw = n_q_heads * hd
    grp = n_q_heads // n_kv_heads
    tq = GQ_TQ
    tk = _tile(t - n_ctx, GQ_TK, V7X_LANES)
    assert n_ctx % tq == 0 and t % tq == 0 and n_ctx % V7X_LANES == 0
    kern = functools.partial(_flash_kernel, grp=grp, hd=hd, tq=tq, tk=tk, n_ctx=n_ctx,
                             n_ctx_qblocks=n_ctx // tq, n_lat_chunks=(t - n_ctx) // tk)
    return pl.pallas_call(
        kern,
        grid=(b, n_kv_heads, t // tq),
        in_specs=[pl.BlockSpec((1, tq, grp * hd), lambda bi, h, i: (bi, i, h)),
                  pl.BlockSpec((1, t, hd), lambda bi, h, i: (bi, 0, n_q_heads + h)),
                  pl.BlockSpec((1, t, hd), lambda bi, h, i: (bi, 0, h))],
        out_specs=pl.BlockSpec((1, tq, grp * hd), lambda bi, h, i: (bi, i, h)),
        out_shape=jax.ShapeDtypeStruct((b, t, qw), BF16),
        compiler_params=_params("parallel", "parallel", "parallel"),
        name="gqa_attention",
    )(qk, qk, v)


def _na_kernel(q_ref, k_ref, v_ref, bias_ref, o_ref, *, n_ctx, rows, n_blocks):
    j = pl.program_id(2)
    bs = range(q_ref.shape[0])
    dn = (((1,), (1,)), ((), ()))
    scale = q_ref.shape[-1] ** -0.5
    q = [q_ref[b] for b in bs]
    kc = [k_ref[b, 0:n_ctx, :] for b in bs]
    vc = [v_ref[b, 0:n_ctx, :] for b in bs]
    s_ctx = [lax.dot_general(q[b], kc[b], dn, preferred_element_type=F32) * scale for b in bs]

    def row_max(x):
        return jnp.max(x, axis=-1, keepdims=True)

    def row_sum(x):
        return jnp.sum(x, axis=-1, keepdims=True)

    def pv(p, v):
        return jnp.dot(p.astype(BF16), v, preferred_element_type=F32)

    @pl.when(j == 0)
    def _():
        p = [jnp.exp(s_ctx[b] - row_max(s_ctx[b])) for b in bs]
        for b in bs:
            o_ref[b] = (pv(p[b], vc[b]) / row_sum(p[b])).astype(o_ref.dtype)

    @pl.when(j > 0)
    def _():
        jj = j - 1
        ks = jnp.clip(NA_QROWS * jj - WIN_ROWS // 2, 0, rows - NA_KROWS)
        off = pl.multiple_of(n_ctx + ks * GRID_W, GRID_W)
        pat = jnp.where(jj == 0, 0, jnp.where(jj == n_blocks - 1, 2, 1))
        bias = bias_ref[pat]
        kw = [k_ref[b, pl.ds(off, NA_KROWS * GRID_W), :] for b in bs]
        vw = [v_ref[b, pl.ds(off, NA_KROWS * GRID_W), :] for b in bs]
        s_win = [lax.dot_general(q[b], kw[b], dn, preferred_element_type=F32) * scale + bias for b in bs]
        m = [jnp.maximum(row_max(s_win[b]), row_max(s_ctx[b])) for b in bs]
        p_win = [jnp.exp(s_win[b] - m[b]) for b in bs]
        p_ctx = [jnp.exp(s_ctx[b] - m[b]) for b in bs]
        for b in bs:
            l = row_sum(p_win[b]) + row_sum(p_ctx[b])
            o_ref[b] = ((pv(p_win[b], vw[b]) + pv(p_ctx[b], vc[b])) / l).astype(o_ref.dtype)


def _na_bias(rpb, rows):
    n_blocks = rows // NA_QROWS
    col = np.arange(GRID_W)
    col_start = np.clip(col - WIN_COLS // 2, 0, GRID_W - WIN_COLS)
    col_ok = (col[None, :] >= col_start[:, None]) & (col[None, :] < col_start[:, None] + WIN_COLS)
    dc = np.clip(col[None, :] - col[:, None] + (WIN_COLS - 1), 0, 2 * WIN_COLS - 2)
    pick_c = (dc[:, :, None] == np.arange(2 * WIN_COLS - 1)).astype(np.float32)
    pats = []
    for jj in (0, 1, n_blocks - 1):
        ks = min(max(NA_QROWS * jj - WIN_ROWS // 2, 0), rows - NA_KROWS)
        r = NA_QROWS * jj + np.arange(NA_QROWS)
        kr = ks + np.arange(NA_KROWS)
        r_start = np.clip(r - WIN_ROWS // 2, 0, rows - WIN_ROWS)
        row_ok = (kr[None, :] >= r_start[:, None]) & (kr[None, :] < r_start[:, None] + WIN_ROWS)
        dr = np.clip(kr[None, :] - r[:, None] + (WIN_ROWS - 1), 0, 2 * WIN_ROWS - 2)
        pick_r = (dr[:, :, None] == np.arange(2 * WIN_ROWS - 1)).astype(np.float32)
        bias = jnp.einsum('hab,rka,qcb->hrqkc', rpb.astype(F32), pick_r, pick_c, precision=HI)
        ok = row_ok[:, None, :, None] & col_ok[None, :, None, :]
        bias = jnp.where(ok[None], bias, NEG_BIG)
        pats.append(bias.reshape(rpb.shape[0], NA_QROWS * GRID_W, NA_KROWS * GRID_W))
    return jnp.stack(pats, axis=0)


def neighborhood_attention(qkv, rpb, n_ctx):
    b, t, width3 = qkv.shape
    width = width3 // 3
    hd = width // NA_HEADS
    seq = t - n_ctx
    rows = seq // GRID_W
    qb = NA_QROWS * GRID_W
    assert n_ctx == qb and rows % NA_QROWS == 0 and rows >= NA_KROWS and WIN_ROWS <= rows
    n_blocks = rows // NA_QROWS
    bias = _na_bias(rpb, rows)
    kern = functools.partial(_na_kernel, n_ctx=n_ctx, rows=rows, n_blocks=n_blocks)
    kb = NA_KROWS * GRID_W
    nbatch = NA_BATCH if b % NA_BATCH == 0 else 1
    return pl.pallas_call(
        kern,
        grid=(NA_HEADS, b // nbatch, n_blocks + 1),
        in_specs=[pl.BlockSpec((nbatch, qb, hd), lambda h, bi, j: (bi, j, h)),
                  pl.BlockSpec((nbatch, t, hd), lambda h, bi, j: (bi, 0, NA_HEADS + h)),
                  pl.BlockSpec((nbatch, t, hd), lambda h, bi, j: (bi, 0, 2 * NA_HEADS + h)),
                  pl.BlockSpec((3, None, qb, kb), lambda h, bi, j: (0, h, 0, 0))],
        out_specs=pl.BlockSpec((nbatch, qb, hd), lambda h, bi, j: (bi, j, h)),
        out_shape=jax.ShapeDtypeStruct((b, t, width), BF16),
        compiler_params=_params("parallel", "parallel", "parallel"),
        name="neighborhood_attention",
    )(qkv, qkv, qkv, bias)


def _s5_kernel(u_ref, t_ref, n_ref, m_ref, d_ref, y_ref, hl_sc, hp_sc, *, nc_ctx, nc_all, bsz):
    u = u_ref[0]
    for part in range(4):
        hl_sc[part] = jnp.dot(u, n_ref[0, part], preferred_element_type=F32)
    zero = jnp.zeros((bsz, S5_STATE), F32)

    def scan(direction):
        dr = d_ref[0, 2 * direction:2 * direction + 1, :]
        di = d_ref[0, 2 * direction + 1:2 * direction + 2, :]

        def body(step, carry):
            hr, hi = carry
            if direction == 0:
                c = step
            else:
                c = jnp.where(step < nc_ctx, nc_ctx - 1 - step, nc_all - 1 - (step - nc_ctx))
            r0 = pl.multiple_of(c * bsz, bsz)
            hp_sc[2 * direction, pl.ds(r0, bsz), :] = hr
            hp_sc[2 * direction + 1, pl.ds(r0, bsz), :] = hi
            lr = hl_sc[2 * direction, pl.ds(r0, bsz), :]
            li = hl_sc[2 * direction + 1, pl.ds(r0, bsz), :]
            return dr * hr - di * hi + lr, dr * hi + di * hr + li

        lax.fori_loop(0, nc_all, body, (zero, zero))

    scan(0)
    scan(1)
    y = jnp.dot(u, t_ref[0], preferred_element_type=F32)
    for part in range(4):
        y = y + jnp.dot(hp_sc[part].astype(BF16), m_ref[0, part], preferred_element_type=F32)
    y_ref[0] = y.astype(y_ref.dtype)


def _s5_tables(lam_re, lam_im, log_dt, b_re, b_im, c_re, c_im):
    c = S5_CHUNK
    g, p = lam_re.shape[1], lam_re.shape[2]
    h = b_re.shape[-1]
    toe = 0.0
    n_parts, m_parts, d_parts = [], [], []
    tt = jnp.arange(c)
    for d in range(2):
        lr, li = lam_re[d].astype(F32), lam_im[d].astype(F32)
        dt = jnp.exp(log_dt[d].astype(F32))[:, None]

        def lam_pow(e):
            e = jnp.asarray(e, F32)[..., None, None]
            mag = jnp.exp(e * lr * dt)
            return mag * jnp.cos(e * li * dt), mag * jnp.sin(e * li * dt)

        lbr, lbi = lam_pow(1)
        nr, ni = lbr - 1.0, lbi
        den = lr * lr + li * li
        fr, fi = (nr * lr + ni * li) / den, (ni * lr - nr * li) / den
        br, bi = b_re[d].astype(F32), b_im[d].astype(F32)
        bbr = fr[..., None] * br - fi[..., None] * bi
        bbi = fr[..., None] * bi + fi[..., None] * br
        cr, ci = c_re[d].astype(F32), c_im[d].astype(F32)
        pr, pi = lam_pow(tt)
        lbbr = pr[..., None] * bbr - pi[..., None] * bbi
        lbbi = pr[..., None] * bbi + pi[..., None] * bbr
        kern = (jnp.einsum('gop,tgpi->tgoi', cr, lbbr, precision=HI)
                - jnp.einsum('gop,tgpi->tgoi', ci, lbbi, precision=HI))
        steps = np.arange(c)
        lag = (steps[None, :] - steps[:, None]) if d == 0 else (steps[:, None] - steps[None, :])
        pick = (lag[:, :, None] == steps).astype(np.float32)
        kt = jnp.einsum('stx,xgoi->gsito', pick, kern, precision=HI)
        toe = toe + kt.reshape(g, c * h, c * h)
        e_in = (c - 1 - tt) if d == 0 else tt
        qr, qi = lam_pow(e_in)
        n_r = qr[..., None] * bbr - qi[..., None] * bbi
        n_i = qr[..., None] * bbi + qi[..., None] * bbr
        n_parts += [jnp.transpose(n_r, (1, 0, 3, 2)).reshape(g, c * h, p),
                    jnp.transpose(n_i, (1, 0, 3, 2)).reshape(g, c * h, p)]
        e_out = (tt + 1) if d == 0 else (c - tt)
        sr, si = lam_pow(e_out)
        clr = cr[None] * sr[:, :, None, :] - ci[None] * si[:, :, None, :]
        cli = cr[None] * si[:, :, None, :] + ci[None] * sr[:, :, None, :]
        m_parts += [jnp.transpose(clr, (1, 3, 0, 2)).reshape(g, p, c * h),
                    -jnp.transpose(cli, (1, 3, 0, 2)).reshape(g, p, c * h)]
        dcr, dci = lam_pow(c)
        d_parts += [dcr, dci]
    return (toe.astype(BF16), jnp.stack(n_parts, 1).astype(BF16), jnp.stack(m_parts, 1).astype(BF16),
            jnp.stack(d_parts, 1))


def s5_scan(u, n_ctx, lam_re, lam_im, log_dt, b_re, b_im, c_re, c_im):
    bsz, t, width = u.shape
    g = width // S5_GROUP
    c = S5_CHUNK
    assert bsz == V7X_SUBLANES and t % c == 0 and n_ctx % c == 0
    nc = t // c
    rows = nc * bsz
    toe, n_tab, m_tab, d_tab = _s5_tables(lam_re, lam_im, log_dt, b_re, b_im, c_re, c_im)
    uc = u.astype(BF16).reshape(bsz, nc, c, g, S5_GROUP)
    uc = jnp.transpose(uc, (3, 1, 0, 2, 4)).reshape(g, rows, c * S5_GROUP)
    kern = functools.partial(_s5_kernel, nc_ctx=n_ctx // c, nc_all=nc, bsz=bsz)
    cw = c * S5_GROUP
    y = pl.pallas_call(
        kern,
        grid=(g,),
        in_specs=[pl.BlockSpec((1, rows, cw), lambda i: (i, 0, 0)),
                  pl.BlockSpec((1, cw, cw), lambda i: (i, 0, 0)),
                  pl.BlockSpec((1, 4, cw, S5_STATE), lambda i: (i, 0, 0, 0)),
                  pl.BlockSpec((1, 4, S5_STATE, cw), lambda i: (i, 0, 0, 0)),
                  pl.BlockSpec((1, 4, S5_STATE), lambda i: (i, 0, 0))],
        out_specs=pl.BlockSpec((1, rows, cw), lambda i: (i, 0, 0)),
        out_shape=jax.ShapeDtypeStruct((g, rows, cw), BF16),
        scratch_shapes=[pltpu.VMEM((4, rows, S5_STATE), F32), pltpu.VMEM((4, rows, S5_STATE), F32)],
        compiler_params=_params("parallel"),
        name="s5_scan",
    )(uc, toe, n_tab, m_tab, d_tab)
    y = y.reshape(g, nc, bsz, c, S5_GROUP)
    return jnp.transpose(y, (2, 1, 3, 0, 4)).reshape(bsz, t, width)


def _rwkv_kernel(at_ref, rt_ref, bt_ref, kt_ref, v_ref, pc_ref, y_ref, s_sc, *, heads, c):
    n = pl.program_id(2)

    @pl.when(n == 0)
    def _():
        s_sc[...] = jnp.zeros(s_sc.shape, F32)

    row = lax.broadcasted_iota(jnp.int32, (c, c), 0)
    col = lax.broadcasted_iota(jnp.int32, (c, c), 1)
    ahead = (row - col) * jnp.where(pl.program_id(0) == 0, 1, -1)
    strict = ahead > 0
    incl = ahead >= 0
    nt = (((1,), (1,)), ((), ()))
    tn = (((0,), (0,)), ((), ()))
    n_factors = int(math.log2(c))
    hs = range(heads)
    sls = [slice(j * RK_HEAD, (j + 1) * RK_HEAD) for j in hs]

    def dot(a, b, dims=None):
        a, b = a.astype(BF16), b.astype(BF16)
        if dims is None:
            return jnp.dot(a, b, preferred_element_type=F32)
        return lax.dot_general(a, b, dims, preferred_element_type=F32)

    vv = [v_ref[0, :, sl] for sl in sls]
    s0 = [s_sc[j] for j in hs]
    ar = [jnp.concatenate([at_ref[0, 0, :, sl], rt_ref[0, 0, :, sl]], axis=0) for sl in sls]
    bk = [jnp.concatenate([bt_ref[0, 0, :, sl], kt_ref[0, 0, :, sl]], axis=0) for sl in sls]
    g = [dot(ar[j], bk[j], nt) for j in hs]
    ah = [dot(ar[j], s0[j], nt) for j in hs]
    x = [jnp.where(strict, g[j][:c, :c], 0.0) for j in hs]
    a_ak = [jnp.where(strict, g[j][:c, c:], 0.0) for j in hs]
    m = [jnp.concatenate([jnp.where(incl, g[j][c:, :c], 0.0), jnp.where(incl, g[j][c:, c:], 0.0)], axis=1)
         for j in hs]
    u = [ah[j][:c] + dot(a_ak[j], vv[j]) for j in hs]
    u = [u[j] + dot(x[j], u[j]) for j in hs]
    for _ in range(n_factors - 1):
        x = [dot(x[j], x[j]) for j in hs]
        u = [u[j] + dot(x[j], u[j]) for j in hs]
    uv = [jnp.concatenate([u[j].astype(BF16), vv[j]], axis=0) for j in hs]
    for j in hs:
        y_ref[0, 0, :, sls[j]] = ah[j][c:] + dot(m[j], uv[j])
    for j in hs:
        s_new = s0[j] + dot(uv[j], bk[j], tn)
        s_sc[j] = s_new * pc_ref[0, 0, 0, :, sls[j]]


def rwkv7_scan(at, rt, bt, kt, v, pc, n_ctx):
    _, bsz, t, width = at.shape
    heads = width // RK_HEAD
    c = RK_CHUNK
    assert t % c == 0 and n_ctx % c == 0
    ncc, nc = n_ctx // c, t // c

    def chunk(d, n):
        back = jnp.where(n < ncc, ncc - 1 - n, ncc + (nc - 1 - n))
        return jnp.where(d == 0, n, back)

    stream = pl.BlockSpec((1, 1, c, width), lambda d, bi, n: (d, bi, chunk(d, n), 0))
    kern = functools.partial(_rwkv_kernel, heads=heads, c=c)
    return pl.pallas_call(
        kern,
        grid=(2, bsz, nc),
        in_specs=[stream, stream, stream, stream,
                  pl.BlockSpec((1, c, width), lambda d, bi, n: (bi, chunk(d, n), 0)),
                  pl.BlockSpec((1, 1, 1, 1, width), lambda d, bi, n: (d, bi, chunk(d, n), 0, 0))],
        out_specs=stream,
        out_shape=jax.ShapeDtypeStruct((2, bsz, t, width), F32),
        scratch_shapes=[pltpu.VMEM((heads, RK_HEAD, RK_HEAD), F32)],
        compiler_params=_params("parallel", "parallel", "arbitrary"),
        name="rwkv7_scan",
    )(at, rt, bt, kt, v, pc)


def _deinterleave_kernel(w_ref, p_ref, o_ref):
    g = p_ref.shape[0]
    half = o_ref.shape[1] // 2
    for c in range(w_ref.shape[1] // g):
        t = jnp.dot(w_ref[:, c * g:(c + 1) * g].astype(BF16), p_ref[...], preferred_element_type=F32)
        o_ref[:, c * g // 2:(c + 1) * g // 2] = t[:, :g // 2].astype(o_ref.dtype)
        o_ref[:, half + c * g // 2:half + (c + 1) * g // 2] = t[:, g // 2:].astype(o_ref.dtype)


def deinterleave_columns(w_all, layer):
    _, r, n2 = w_all.shape
    g = V7X_MXU_DIM
    assert n2 % g == 0
    perm = np.concatenate([np.arange(0, g, 2), np.arange(1, g, 2)])
    p = jnp.asarray(np.arange(g)[:, None] == perm[None, :], BF16)
    tm = _tile(r, 1024, 16)
    w = w_all.reshape(-1, n2)
    first = layer * (r // tm)
    return pl.pallas_call(
        _deinterleave_kernel,
        grid=(r // tm,),
        in_specs=[pl.BlockSpec((tm, n2), lambda i: (first + i, 0)), pl.BlockSpec((g, g), lambda i: (0, 0))],
        out_specs=pl.BlockSpec((tm, n2), lambda i: (i, 0)),
        out_shape=jax.ShapeDtypeStruct((r, n2), BF16),
        compiler_params=_params("parallel"),
        name="deinterleave_columns",
    )(w, p)


def _moe_kernel(be_ref, nb_ref, x_ref, g_ref, w1_ref, b1g_ref, b1l_ref, w2_ref, b2_ref, o_ref, w2_sc):
    i = pl.program_id(0)
    de = w2_ref.shape[1]

    @pl.when((i == 0) | (be_ref[i] != be_ref[jnp.maximum(i - 1, 0)]))
    def _():
        w2_sc[...] = w2_ref[0].astype(w2_sc.dtype)

    @pl.when(i < nb_ref[0])
    def _():
        x = x_ref[...]
        z = jnp.dot(x, w1_ref[0], preferred_element_type=F32)
        zg = z[:, :de] + b1g_ref[0]
        zl = z[:, de:] + b1l_ref[0]
        zg = jnp.minimum(zg, SWIGLU_LIMIT)
        zl = jnp.clip(zl, -SWIGLU_LIMIT, SWIGLU_LIMIT)
        act = zg * jax.nn.sigmoid(SWIGLU_ALPHA * zg) * (zl + 1.0)
        y = jnp.dot(act.astype(BF16), w2_sc[...], preferred_element_type=F32) + b2_ref[0]
        o_ref[...] = (y * g_ref[...]).astype(o_ref.dtype)

    @pl.when(i >= nb_ref[0])
    def _():
        o_ref[...] = jnp.zeros(o_ref.shape, o_ref.dtype)


def moe_ffn(h, w_router, b_router, w1p, w2_all, layer, b1, b2):
    n, dm = h.shape
    n_exp = w_router.shape[1]
    de = w2_all.shape[2]
    tm = MOE_TM
    logits = jnp.dot(h, w_router, precision=HI, preferred_element_type=F32) + b_router.astype(F32)
    top_val, top_idx = lax.top_k(logits, TOP_K)
    gates = jax.nn.softmax(top_val, axis=-1)
    flat_e = top_idx.reshape(-1).astype(jnp.int32)
    rb = V7X_LANES
    assert (n * TOP_K) % rb == 0
    onehot = (flat_e[:, None] == jnp.arange(n_exp, dtype=jnp.int32)[None, :]).astype(F32)
    blocks = onehot.reshape(-1, rb, n_exp)
    tri = np.tril(np.ones((rb, rb), np.float32))
    within = jnp.einsum('ij,bjk->bik', tri, blocks, precision=HI)
    totals = within[:, -1, :]
    before = jnp.cumsum(totals, axis=0) - totals
    csum = within + before[:, None, :]
    rank = (jnp.sum(csum * blocks, axis=-1).reshape(-1) - 1.0).astype(jnp.int32)
    counts = (before[-1] + totals[-1]).astype(jnp.int32)
    padded = (counts + tm - 1) // tm * tm
    pad_end = jnp.cumsum(padded)
    pad_start = pad_end - padded
    dest = pad_start[flat_e] + rank
    n_blocks = -(-(n * TOP_K) // tm) + n_exp
    cap = n_blocks * tm
    slot_src = jnp.full((cap,), -1, jnp.int32).at[dest].set(jnp.arange(n * TOP_K, dtype=jnp.int32))
    filled = slot_src >= 0
    slot_tok = jnp.where(filled, slot_src // TOP_K, 0)
    slot_gate = jnp.where(filled, gates.reshape(-1)[jnp.maximum(slot_src, 0)], 0.0)
    block_start = jnp.arange(n_blocks, dtype=jnp.int32) * tm
    block_exp = jnp.minimum(jnp.sum(pad_end[None, :] <= block_start[:, None], axis=1), n_exp - 1).astype(jnp.int32)
    n_used = (pad_end[-1] // tm).astype(jnp.int32).reshape(1)
    x_sorted = h.astype(BF16)[slot_tok]

    b1g = b1[:, None, 0::2].astype(F32)
    b1l = b1[:, None, 1::2].astype(F32)
    b2r = b2[:, None, :].astype(F32)

    grid_spec = pltpu.PrefetchScalarGridSpec(
        num_scalar_prefetch=2,
        grid=(n_blocks,),
        in_specs=[pl.BlockSpec((tm, dm), lambda i, be, nb: (i, 0)),
                  pl.BlockSpec((tm, 1), lambda i, be, nb: (i, 0)),
                  pl.BlockSpec((1, dm, 2 * de), lambda i, be, nb: (be[i], 0, 0)),
                  pl.BlockSpec((1, 1, de), lambda i, be, nb: (be[i], 0, 0)),
                  pl.BlockSpec((1, 1, de), lambda i, be, nb: (be[i], 0, 0)),
                  pl.BlockSpec((1, de, dm), lambda i, be, nb: (layer * n_exp + be[i], 0, 0)),
                  pl.BlockSpec((1, 1, dm), lambda i, be, nb: (be[i], 0, 0))],
        out_specs=pl.BlockSpec((tm, dm), lambda i, be, nb: (i, 0)),
        scratch_shapes=[pltpu.VMEM((de, dm), BF16)],
    )
    yb = pl.pallas_call(
        _moe_kernel,
        grid_spec=grid_spec,
        out_shape=jax.ShapeDtypeStruct((cap, dm), BF16),
        compiler_params=_params("arbitrary"),
        name="moe_experts",
    )(block_exp, n_used, x_sorted, slot_gate[:, None], w1p, b1g, b1l, w2_all.reshape(-1, de, dm), b2r)
    slots = dest.reshape(n, TOP_K)
    return [yb[slots[:, kk]] for kk in range(TOP_K)]


def _expert_sum(refs, n_parts, half_blocks):
    def total(parts):
        y = parts[0][...].astype(F32)
        for p in parts[1:]:
            y = y + p[...].astype(F32)
        return y

    first = pl.program_id(0) < half_blocks
    return jnp.where(first, total(refs[:n_parts]), total(refs[n_parts:2 * n_parts]))


def _combine_kernel(*refs, n_parts, half_blocks):
    h_ref, g_ref, nw_ref, sc_ref, sh_ref, hn_ref, a_ref = refs[2 * n_parts:]
    hn = h_ref[...] + g_ref[0] * _expert_sum(refs, n_parts, half_blocks)
    hn_ref[...] = hn
    normed = hn * lax.rsqrt(jnp.mean(hn * hn, axis=-1, keepdims=True) + NORM_EPS) * nw_ref[...]
    a_ref[...] = (normed * (1.0 + sc_ref[0]) + sh_ref[0]).astype(a_ref.dtype)


def _combine_final_kernel(*refs, n_parts, half_blocks):
    h_ref, g_ref, nw_ref, o_ref = refs[2 * n_parts:]
    hn = h_ref[...] + g_ref[0] * _expert_sum(refs, n_parts, half_blocks)
    o_ref[...] = (hn * lax.rsqrt(jnp.mean(hn * hn, axis=-1, keepdims=True) + NORM_EPS)
                  * nw_ref[...]).astype(o_ref.dtype)


def _half_specs(n_parts, tm, dm, half_blocks):
    lo = pl.BlockSpec((tm, dm), lambda i: (jnp.minimum(i, half_blocks - 1), 0))
    hi = pl.BlockSpec((tm, dm), lambda i: (jnp.maximum(i - half_blocks, 0), 0))
    return [lo] * n_parts + [hi] * n_parts


def combine_residual_norm(parts_lo, parts_hi, h, n_ctx, gate, norm_w, scale, shift):
    bsz, t, dm = h.shape
    tm = _tile(n_ctx, 256, 16)
    m = bsz * t
    assert t % tm == 0 and (m // 2) % tm == 0
    nbt, ncb = t // tm, n_ctx // tm
    half_blocks = m // 2 // tm
    n_parts = len(parts_lo)
    rows = pl.BlockSpec((tm, dm), lambda i: (i, 0))
    mod = pl.BlockSpec((1, 1, dm), lambda i: (jnp.where(i % nbt < ncb, bsz, i // nbt), 0, 0))
    hn, a = pl.pallas_call(
        functools.partial(_combine_kernel, n_parts=n_parts, half_blocks=half_blocks),
        grid=(m // tm,),
        in_specs=_half_specs(n_parts, tm, dm, half_blocks)
        + [rows, mod, pl.BlockSpec((1, dm), lambda i: (0, 0)), mod, mod],
        out_specs=[rows, rows],
        out_shape=[jax.ShapeDtypeStruct((m, dm), F32), jax.ShapeDtypeStruct((m, dm), BF16)],
        compiler_params=_params("parallel"),
        name="combine_residual_norm",
    )(*parts_lo, *parts_hi, h.reshape(m, dm), gate[:, None, :], norm_w.astype(F32)[None, :],
      scale[:, None, :], shift[:, None, :])
    return hn.reshape(bsz, t, dm), a.reshape(bsz, t, dm)


def combine_final_norm(parts_lo, parts_hi, h, n_ctx, gate, norm_w, out_dtype):
    bsz, t, dm = h.shape
    seq = t - n_ctx
    tm = _tile(n_ctx, 256, 16)
    m = bsz * seq
    assert t % tm == 0 and seq % tm == 0 and (m // 2) % tm == 0
    nbt, ncb, nbs = t // tm, n_ctx // tm, seq // tm
    half_blocks = m // 2 // tm
    n_parts = len(parts_lo)
    out = pl.pallas_call(
        functools.partial(_combine_final_kernel, n_parts=n_parts, half_blocks=half_blocks),
        grid=(m // tm,),
        in_specs=_half_specs(n_parts, tm, dm, half_blocks)
        + [pl.BlockSpec((tm, dm), lambda i: ((i // nbs) * nbt + ncb + i % nbs, 0)),
           pl.BlockSpec((1, 1, dm), lambda i: (i // nbs, 0, 0)),
           pl.BlockSpec((1, dm), lambda i: (0, 0))],
        out_specs=pl.BlockSpec((tm, dm), lambda i: (i, 0)),
        out_shape=jax.ShapeDtypeStruct((m, dm), out_dtype),
        compiler_params=_params("parallel"),
        name="combine_final_norm",
    )(*parts_lo, *parts_hi, h.reshape(bsz * t, dm), gate[:, None, :], norm_w.astype(F32)[None, :])
    return out.reshape(bsz, seq, dm)


def _rms(x, g, eps=NORM_EPS):
    xf = x.astype(F32)
    return xf * lax.rsqrt(jnp.mean(xf * xf, axis=-1, keepdims=True) + eps) * g.astype(F32)


def _modulated(h, n_ctx, norm_w, scale_c, shift_c, scale_x, shift_x):
    hn = _rms(h, norm_w)
    ac = hn[:, :n_ctx] * (1.0 + scale_c) + shift_c
    ax = hn[:, n_ctx:] * (1.0 + scale_x[:, None, :]) + shift_x[:, None, :]
    return jnp.concatenate([ac, ax], axis=1)


def _even_mixer(a, n_ctx, w_in, lam_re, lam_im, log_dt, b_re, b_im, c_re, c_im, d_skip,
                glu_w, glu_b, rpb):
    bsz, t, dm = a.shape
    mix_w = dm // 2
    a2 = a.reshape(bsz * t, dm).astype(BF16)
    u = matmul(a2, w_in[:, :mix_w]).reshape(bsz, t, mix_w)
    qkv = matmul(a2, w_in[:, mix_w:], out_dtype=BF16).reshape(bsz, t, 3 * mix_w)
    y = d_skip.astype(F32) * u + s5_scan(u, n_ctx, lam_re, lam_im, log_dt, b_re, b_im, c_re, c_im)
    gl = jax.nn.gelu(y)
    ya = gl * jax.nn.sigmoid(matmul(gl.reshape(bsz * t, mix_w), glu_w).reshape(bsz, t, mix_w)
                             + glu_b.astype(F32))
    yb = neighborhood_attention(qkv, rpb, n_ctx)
    return ya.astype(BF16), yb


def _rope_tables(n_ctx, seq, hd):
    half = hd // 2
    inv = ROPE_THETA ** (-jnp.arange(0, half, 2, dtype=F32) / half)
    pos = jnp.arange(seq)
    ang_r = (pos // GRID_W).astype(F32)[:, None] * inv[None, :]
    ang_c = (pos % GRID_W).astype(F32)[:, None] * inv[None, :]
    cos = jnp.concatenate([jnp.cos(ang_r), jnp.cos(ang_r), jnp.cos(ang_c), jnp.cos(ang_c)], axis=-1)
    sin = jnp.concatenate([-jnp.sin(ang_r), jnp.sin(ang_r), -jnp.sin(ang_c), jnp.sin(ang_c)], axis=-1)
    cos = jnp.concatenate([jnp.ones((n_ctx, hd), F32), cos], axis=0)
    sin = jnp.concatenate([jnp.zeros((n_ctx, hd), F32), sin], axis=0)
    return cos, sin


def _qk_prep_kernel(z_ref, cos_ref, sin_ref, gq_ref, gk_ref, o_ref, *, n_q_heads, hd):
    cosv = cos_ref[...]
    sinv = sin_ref[...]
    lane = lax.broadcasted_iota(jnp.int32, (1, hd), 1)
    first_quarter = (lane % (hd // 2)) < hd // 4
    for h in range(z_ref.shape[2] // hd):
        x = z_ref[0, :, h * hd:(h + 1) * hd]
        g = gq_ref[...] if h < n_q_heads else gk_ref[...]
        xn = x * lax.rsqrt(jnp.mean(x * x, axis=-1, keepdims=True) + NORM_EPS) * g
        partner = jnp.where(first_quarter, pltpu.roll(xn, hd - hd // 4, axis=1), pltpu.roll(xn, hd // 4, axis=1))
        o_ref[0, :, h * hd:(h + 1) * hd] = (xn * cosv + partner * sinv).astype(o_ref.dtype)


def qk_prepare(zqk, n_ctx, n_q_heads, q_norm, k_norm):
    bsz, t, width = zqk.shape
    hd = GQ_HEAD_DIM
    tm = _tile(t, 256, 16)
    cos, sin = _rope_tables(n_ctx, t - n_ctx, hd)
    rows = pl.BlockSpec((1, tm, width), lambda b, i: (b, i, 0))
    tab = pl.BlockSpec((tm, hd), lambda b, i: (i, 0))
    gain = pl.BlockSpec((1, hd), lambda b, i: (0, 0))
    return pl.pallas_call(
        functools.partial(_qk_prep_kernel, n_q_heads=n_q_heads, hd=hd),
        grid=(bsz, t // tm),
        in_specs=[rows, tab, tab, gain, gain],
        out_specs=rows,
        out_shape=jax.ShapeDtypeStruct((bsz, t, width), BF16),
        compiler_params=_params("parallel", "parallel"),
        name="qk_prepare",
    )(zqk, cos, sin, q_norm.astype(F32)[None, :], k_norm.astype(F32)[None, :])


def _split_bf16(x, terms):
    parts = []
    for _ in range(terms):
        p = x.astype(BF16)
        parts.append(p)
        x = x - p.astype(F32)
    return parts


def _head_sums(x, ones_ref):
    gw = ones_ref.shape[0]
    hi, lo = _split_bf16(x, 2)
    outs = []
    for g in range(x.shape[1] // gw):
        sl = slice(g * gw, (g + 1) * gw)
        outs.append(jnp.dot(hi[:, sl], ones_ref[...], preferred_element_type=F32)
                    + jnp.dot(lo[:, sl], ones_ref[...], preferred_element_type=F32))
    return jnp.concatenate(outs, axis=1)


def _rwkv_prep_kernel(z_ref, zp_ref, zn_ref, mu_ref, gup_ref, wup_ref, aup_ref, vec_ref, rk_ref, tri_ref,
                      ones_ref, at_ref, rt_ref, bt_ref, kt_ref, v_ref, pc_ref, gate_ref, bonus_ref,
                      *, width, tb, c, ncb, nb):
    i = pl.program_id(1)
    keep_prev = jnp.where((i == 0) | (i == ncb), 0.0, 1.0)
    keep_next = jnp.where((i == ncb - 1) | (i == nb - 1), 0.0, 1.0)
    z = z_ref[0]
    rid = lax.broadcasted_iota(jnp.int32, (tb, 1), 0)
    zp = jnp.where(rid == 0, zp_ref[0, 7:8, :] * keep_prev, pltpu.roll(z, 1, axis=0))
    zn = jnp.where(rid == tb - 1, zn_ref[0, 0:1, :] * keep_next, pltpu.roll(z, tb - 1, axis=0))
    zs = z + mu_ref[0:1, :] * (zp - z) + mu_ref[1:2, :] * (zn - z)
    w = width
    r, k, v = zs[:, :w], zs[:, w:2 * w], zs[:, 2 * w:3 * w]
    g_lo = zs[:, 3 * w:3 * w + RK_GATE_RANK]
    o = 3 * w + RK_GATE_RANK
    wl = jnp.tanh(zs[:, o:o + 2 * RK_DECAY_RANK]).astype(BF16)
    al = zs[:, o + 2 * RK_DECAY_RANK:o + 2 * RK_DECAY_RANK + 2 * RK_ICLR_RANK].astype(BF16)
    v_ref[0] = v.astype(v_ref.dtype)
    gate_ref[0] = jnp.dot(jax.nn.sigmoid(g_lo).astype(BF16), gup_ref[...], preferred_element_type=F32)
    ksum = jnp.zeros_like(k)
    for d in range(2):
        w0, a0 = vec_ref[d, 0:1, :], vec_ref[d, 1:2, :]
        k_k, k_a = vec_ref[d, 2:3, :], vec_ref[d, 3:4, :]
        neg = -(w0 + jnp.dot(wl, wup_ref[d], preferred_element_type=F32))
        softplus = jnp.maximum(neg, 0.0) + jnp.log(1.0 + jnp.exp(-jnp.abs(neg)))
        lw = -jnp.exp(-softplus - 0.5)
        iclr = jax.nn.sigmoid(a0 + jnp.dot(al, aup_ref[d], preferred_element_type=F32))
        kk = k * k_k
        kk = kk / jnp.maximum(jnp.sqrt(_head_sums(kk * kk, ones_ref)), 1e-12)
        k_d = k * (1.0 + (iclr - 1.0) * k_a)
        ksum = ksum + k_d
        cum = sum(jnp.dot(tri_ref[d], part, preferred_element_type=F32) for part in _split_bf16(lw, 3))
        p_inv = jnp.exp(-cum)
        at_ref[d, 0] = (-kk * jnp.exp(cum - lw)).astype(at_ref.dtype)
        rt_ref[d, 0] = (r * jnp.exp(cum)).astype(rt_ref.dtype)
        bt_ref[d, 0] = (kk * iclr * p_inv).astype(bt_ref.dtype)
        kt_ref[d, 0] = (k_d * p_inv).astype(kt_ref.dtype)
        for j in range(tb // c):
            row = j * c + (c - 1 if d == 0 else 0)
            pc_ref[d, 0, j] = jnp.exp(cum[row:row + 1, :])
    bonus_ref[0] = _head_sums(r * ksum * rk_ref[...], ones_ref) * v


def rwkv7_prepare(zr, n_ctx, mu, g_up, w0, w_up, a0, a_up, k_k, k_a, r_k):
    bsz, t, rk_in = zr.shape
    width = g_up.shape[1]
    c, tb = RK_CHUNK, RK_TBLOCK
    assert n_ctx % tb == 0 and t % tb == 0 and tb % c == 0 and tb % V7X_SUBLANES == 0
    ncb, nb = n_ctx // tb, t // tb
    rows_per_tile = V7X_SUBLANES

    def pad_rows(up, d, rank):
        return jnp.pad(up, ((d * rank, (1 - d) * rank), (0, 0)))

    wup = jnp.stack([pad_rows(w_up[d], d, RK_DECAY_RANK) for d in range(2)]).astype(BF16)
    aup = jnp.stack([pad_rows(a_up[d], d, RK_ICLR_RANK) for d in range(2)]).astype(BF16)
    vec = jnp.stack([w0, a0, k_k, k_a], axis=1).astype(F32)
    step = np.arange(tb)
    same = (step[:, None] // c) == (step[None, :] // c)
    tri = np.stack([same & (step[None, :] <= step[:, None]), same & (step[None, :] >= step[:, None])])
    seg = np.arange(V7X_MXU_DIM) // RK_HEAD
    ones = (seg[:, None] == seg[None, :])

    def full(shape):
        return pl.BlockSpec(shape, lambda b, i: (0,) * len(shape))

    per_dir = pl.BlockSpec((2, 1, tb, width), lambda b, i: (0, b, i, 0))
    rows = pl.BlockSpec((1, tb, width), lambda b, i: (b, i, 0))
    kern = functools.partial(_rwkv_prep_kernel, width=width, tb=tb, c=c, ncb=ncb, nb=nb)
    stream = jax.ShapeDtypeStruct((2, bsz, t, width), BF16)
    return pl.pallas_call(
        kern,
        grid=(bsz, nb),
        in_specs=[pl.BlockSpec((1, tb, rk_in), lambda b, i: (b, i, 0)),
                  pl.BlockSpec((1, rows_per_tile, rk_in),
                               lambda b, i: (b, jnp.maximum(i * (tb // rows_per_tile) - 1, 0), 0)),
                  pl.BlockSpec((1, rows_per_tile, rk_in),
                               lambda b, i: (b, jnp.minimum((i + 1) * (tb // rows_per_tile),
                                                            t // rows_per_tile - 1), 0)),
                  full((2, rk_in)), full((RK_GATE_RANK, width)), full((2, 2 * RK_DECAY_RANK, width)),
                  full((2, 2 * RK_ICLR_RANK, width)), full((2, 4, width)), full((1, width)),
                  full((2, tb, tb)), full((V7X_MXU_DIM, V7X_MXU_DIM))],
        out_specs=[per_dir, per_dir, per_dir, per_dir, rows,
                   pl.BlockSpec((2, 1, tb // c, 1, width), lambda b, i: (0, b, i, 0, 0)), rows, rows],
        out_shape=[stream, stream, stream, stream, jax.ShapeDtypeStruct((bsz, t, width), BF16),
                   jax.ShapeDtypeStruct((2, bsz, t // c, 1, width), F32),
                   jax.ShapeDtypeStruct((bsz, t, width), F32), jax.ShapeDtypeStruct((bsz, t, width), F32)],
        compiler_params=_params("parallel", "parallel"),
        name="rwkv7_prepare",
    )(zr, zr, zr, mu.astype(F32), g_up.astype(BF16), wup, aup, vec, r_k.astype(F32).reshape(1, width),
      jnp.asarray(tri, BF16), jnp.asarray(ones, BF16))


def _rwkv_finish_kernel(ys_ref, bonus_ref, gate_ref, lnw_ref, lnb_ref, ones_ref, o_ref):
    y = ys_ref[0, 0] + ys_ref[1, 0]
    inv_n = 1.0 / RK_HEAD
    mean = _head_sums(y, ones_ref) * inv_n
    dev = y - mean
    var = _head_sums(dev * dev, ones_ref) * inv_n
    yn = dev * lax.rsqrt(var + RK_GN_EPS) * lnw_ref[...] + lnb_ref[...]
    o_ref[0] = ((yn + bonus_ref[0]) * gate_ref[0]).astype(o_ref.dtype)


def rwkv7_finish(ys, bonus, gate, ln_w, ln_b):
    _, bsz, t, width = ys.shape
    tb = RK_TBLOCK
    seg = np.arange(V7X_MXU_DIM) // RK_HEAD
    ones = (seg[:, None] == seg[None, :])
    rows = pl.BlockSpec((1, tb, width), lambda b, i: (b, i, 0))
    vec = pl.BlockSpec((1, width), lambda b, i: (0, 0))
    return pl.pallas_call(
        _rwkv_finish_kernel,
        grid=(bsz, t // tb),
        in_specs=[pl.BlockSpec((2, 1, tb, width), lambda b, i: (0, b, i, 0)), rows, rows, vec, vec,
                  pl.BlockSpec((V7X_MXU_DIM, V7X_MXU_DIM), lambda b, i: (0, 0))],
        out_specs=rows,
        out_shape=jax.ShapeDtypeStruct((bsz, t, width), BF16),
        compiler_params=_params("parallel", "parallel"),
        name="rwkv7_finish",
    )(ys, bonus, gate, ln_w.astype(F32)[None, :], ln_b.astype(F32)[None, :], jnp.asarray(ones, BF16))


def _odd_mixer(a, n_ctx, w_in, mu, g_up, w0, w_up, a0, a_up, k_k, k_a, r_k, ln_w, ln_b,
               q_norm, k_norm):
    bsz, t, dm = a.shape
    mix_w = dm // 2
    heads = mix_w // RK_HEAD
    rk_in = 3 * mix_w + RK_GATE_RANK + 2 * RK_DECAY_RANK + 2 * RK_ICLR_RANK
    gq_heads = mix_w // GQ_HEAD_DIM
    kv_heads = gq_heads // 4
    kv_w = kv_heads * GQ_HEAD_DIM
    a2 = a.reshape(bsz * t, dm).astype(BF16)
    zr = matmul(a2, w_in[:, :rk_in]).reshape(bsz, t, rk_in)
    zqk = matmul(a2, w_in[:, rk_in:rk_in + mix_w + kv_w]).reshape(bsz, t, mix_w + kv_w)
    vv = matmul(a2, w_in[:, rk_in + mix_w + kv_w:], out_dtype=BF16).reshape(bsz, t, kv_w)

    at, rt, bt, kt, v, pc, gate, bonus = rwkv7_prepare(zr, n_ctx, mu, g_up, w0, w_up, a0, a_up, k_k, k_a,
                                                       r_k.reshape(-1))
    ys = rwkv7_scan(at, rt, bt, kt, v, pc, n_ctx)
    yc = rwkv7_finish(ys, bonus, gate, ln_w, ln_b)
    qk = qk_prepare(zqk, n_ctx, gq_heads, q_norm, k_norm)
    yd = gqa_attention(qk, vv, n_ctx, gq_heads, kv_heads)
    return yc, yd


def kernel(x, c, ctx, c_ctx, w_mod, b_mod, norm1, norm2, final_norm, ev_w_in, ev_w_out, s5_lam_re, s5_lam_im, s5_log_dt, s5_b_re, s5_b_im, s5_c_re, s5_c_im, s5_d, s5_glu_w, s5_glu_b, na_rpb, od_w_in, od_w_out, rk_mu, rk_g_up, rk_w0, rk_w_up, rk_a0, rk_a_up, rk_k_k, rk_k_a, rk_r_k, rk_ln_w, rk_ln_b, gq_q_norm, gq_k_norm, moe_w_router, moe_b_router, moe_w1, moe_b1, moe_w2, moe_b2):
    bsz, seq, dm = x.shape
    n_ctx = ctx.shape[1]
    depth = w_mod.shape[0]
    t = n_ctx + seq
    h = jnp.concatenate([ctx, x], axis=1).astype(F32)
    cond = jnp.concatenate([jax.nn.silu(c), jax.nn.silu(c_ctx)[None, :]], axis=0)
    mods = [jnp.split(matmul(cond, w_mod[i]) + b_mod[i].astype(F32), N_MOD, axis=-1) for i in range(depth)]
    sh1, sc1 = mods[0][0], mods[0][1]
    a = _modulated(h, n_ctx, norm1[0], sc1[bsz], sh1[bsz], sc1[:bsz], sh1[:bsz])
    for i in range(depth):
        last = i == depth - 1
        j = i // 2
        _, _, g1, sh2, sc2, g2 = mods[i]
        if i % 2 == 0:
            ya, yb = _even_mixer(a, n_ctx, ev_w_in[j], s5_lam_re[j], s5_lam_im[j], s5_log_dt[j],
                                 s5_b_re[j], s5_b_im[j], s5_c_re[j], s5_c_im[j], s5_d[j], s5_glu_w[j],
                                 s5_glu_b[j], na_rpb[j])
            w_out = ev_w_out[j]
        else:
            ya, yb = _odd_mixer(a, n_ctx, od_w_in[j], rk_mu[j], rk_g_up[j], rk_w0[j], rk_w_up[j],
                                rk_a0[j], rk_a_up[j], rk_k_k[j], rk_k_a[j], rk_r_k[j], rk_ln_w[j], rk_ln_b[j],
                                gq_q_norm[j], gq_k_norm[j])
            w_out = od_w_out[j]
        h, f = out_proj_residual_norm(ya, yb, w_out, h, n_ctx, g1, norm2[i], sc2, sh2)
        n_exp, de = moe_w2.shape[1], moe_w2.shape[2]
        w1p = deinterleave_columns(moe_w1.reshape(depth, n_exp * dm, 2 * de), i).reshape(n_exp, dm, 2 * de)
        moe_args = (moe_w_router[i], moe_b_router[i], w1p, moe_w2, i, moe_b1[i], moe_b2[i])
        tokens = f[:, n_ctx:].reshape(bsz * seq, dm) if last else f.reshape(bsz * t, dm)
        half = tokens.shape[0] // 2
        parts_lo = moe_ffn(tokens[:half], *moe_args)
        parts_hi = moe_ffn(tokens[half:], *moe_args)
        if last:
            return combine_final_norm(parts_lo, parts_hi, h, n_ctx, g2, final_norm, x.dtype)
        h, a = combine_residual_norm(parts_lo, parts_hi, h, n_ctx, g2, norm1[i + 1], mods[i + 1][1],
                                     mods[i + 1][0])
```

```python
import functools
import math

import numpy as np
import jax
import jax.numpy as jnp
from jax import lax
from jax.experimental import pallas as pl
from jax.experimental.pallas import tpu as pltpu

F32 = jnp.float32
BF16 = jnp.bfloat16

V7X_LANES = 128
V7X_SUBLANES = 8
V7X_MXU_DIM = 256
V7X_VMEM_BYTES = 64 * 1024 * 1024
VMEM_LIMIT = V7X_VMEM_BYTES * 7 // 8

GRID_W = 64
N_MOD = 6
NORM_EPS = 1e-6
S5_GROUP = 16
S5_STATE = 64
S5_CHUNK = 16
NA_HEADS = 8
WIN_ROWS = 8
WIN_COLS = 16
NA_QROWS = 4
NA_KROWS = NA_QROWS + WIN_ROWS - 1
NA_BATCH = 2
RK_HEAD = 64
RK_DECAY_RANK = 64
RK_ICLR_RANK = 64
RK_GATE_RANK = 128
RK_GN_EPS = 64e-5
RK_CHUNK = 16
RK_TBLOCK = 256
GQ_HEAD_DIM = 128
GQ_GROUP = 4
ROPE_THETA = 10000.0
GQ_TQ = 128
GQ_TK = 1024
TOP_K = 4
SWIGLU_ALPHA = 1.702
SWIGLU_LIMIT = 7.0
MOE_TM = 256
NEG_BIG = -1e30
HI = lax.Precision.HIGHEST


def _params(*sem):
    return pltpu.CompilerParams(dimension_semantics=sem, vmem_limit_bytes=VMEM_LIMIT)


def _tile(n, target, align):
    best = None
    t = align
    while t <= min(n, target):
        if n % t == 0:
            best = t
        t += align
    return n if best is None else best


def _mm_kernel(x_ref, w_ref, o_ref):
    o_ref[...] = jnp.dot(x_ref[...], w_ref[...], preferred_element_type=F32).astype(o_ref.dtype)


def matmul(x, w, out_dtype=F32):
    m, k = x.shape
    n = w.shape[1]
    x = x.astype(BF16)
    w = w.astype(BF16)
    m_pad = -(-m // 16) * 16
    if m_pad != m:
        x = jnp.pad(x, ((0, m_pad - m), (0, 0)))
    tm = _tile(m_pad, 512, 16)
    tn = _tile(n, 2048, V7X_LANES)
    out = pl.pallas_call(
        _mm_kernel,
        grid=(n // tn, m_pad // tm),
        in_specs=[pl.BlockSpec((tm, k), lambda j, i: (i, 0)),
                  pl.BlockSpec((k, tn), lambda j, i: (0, j))],
        out_specs=pl.BlockSpec((tm, tn), lambda j, i: (i, j)),
        out_shape=jax.ShapeDtypeStruct((m_pad, n), out_dtype),
        compiler_params=_params("parallel", "parallel"),
        name="matmul",
    )(x, w)
    return out[:m] if m_pad != m else out


def _out_proj_kernel(xa_ref, xb_ref, wa_ref, wb_ref, h_ref, g_ref, nw_ref, sc_ref, sh_ref, hn_ref, f_ref):
    out = (jnp.dot(xa_ref[...], wa_ref[...], preferred_element_type=F32)
           + jnp.dot(xb_ref[...], wb_ref[...], preferred_element_type=F32))
    hn = h_ref[...] + g_ref[0] * out
    hn_ref[...] = hn
    normed = hn * lax.rsqrt(jnp.mean(hn * hn, axis=-1, keepdims=True) + NORM_EPS) * nw_ref[...]
    f_ref[...] = (normed * (1.0 + sc_ref[0]) + sh_ref[0]).astype(f_ref.dtype)


def out_proj_residual_norm(xa, xb, w_out, h, n_ctx, gate, norm_w, scale, shift):
    bsz, t, dm = h.shape
    wa = xa.shape[-1]
    tm = _tile(n_ctx, 256, 16)
    assert t % tm == 0
    nbt, ncb = t // tm, n_ctx // tm

    def mod_row(i):
        return jnp.where(i % nbt < ncb, bsz, i // nbt)

    rows = pl.BlockSpec((tm, dm), lambda i: (i, 0))
    mod = pl.BlockSpec((1, 1, dm), lambda i: (mod_row(i), 0, 0))
    m = bsz * t
    w_out = w_out.astype(BF16)
    hn, f = pl.pallas_call(
        _out_proj_kernel,
        grid=(m // tm,),
        in_specs=[pl.BlockSpec((tm, wa), lambda i: (i, 0)),
                  pl.BlockSpec((tm, dm - wa), lambda i: (i, 0)),
                  pl.BlockSpec((wa, dm), lambda i: (0, 0)),
                  pl.BlockSpec((dm - wa, dm), lambda i: (0, 0)),
                  rows, mod, pl.BlockSpec((1, dm), lambda i: (0, 0)), mod, mod],
        out_specs=[rows, rows],
        out_shape=[jax.ShapeDtypeStruct((m, dm), F32), jax.ShapeDtypeStruct((m, dm), BF16)],
        compiler_params=_params("parallel"),
        name="out_proj_residual_norm",
    )(xa.reshape(m, wa), xb.reshape(m, dm - wa), w_out[:wa], w_out[wa:], h.reshape(m, dm),
      gate[:, None, :], norm_w.astype(F32)[None, :], scale[:, None, :], shift[:, None, :])
    return hn.reshape(bsz, t, dm), f.reshape(bsz, t, dm)


def _flash_kernel(q_ref, k_ref, v_ref, o_ref, *, grp, hd, tq, tk, n_ctx, n_ctx_qblocks, n_lat_chunks):
    qi = pl.program_id(2)
    q = jnp.concatenate([q_ref[0, :, g * hd:(g + 1) * hd] for g in range(grp)], axis=0)
    scale = hd ** -0.5

    def scores(lo, size):
        return lax.dot_general(q, k_ref[0, lo:lo + size, :], (((1,), (1,)), ((), ())),
                               preferred_element_type=F32) * scale

    def write(acc, l):
        o = acc / l
        for g in range(grp):
            o_ref[0, :, g * hd:(g + 1) * hd] = o[g * tq:(g + 1) * tq].astype(o_ref.dtype)

    s = scores(0, n_ctx)
    m = jnp.max(s, axis=-1, keepdims=True)
    p = jnp.exp(s - m)
    l = jnp.sum(p, axis=-1, keepdims=True)
    acc = jnp.dot(p.astype(BF16), v_ref[0, 0:n_ctx, :], preferred_element_type=F32)

    @pl.when(qi < n_ctx_qblocks)
    def _():
        write(acc, l)

    @pl.when(qi >= n_ctx_qblocks)
    def _():
        lat = [scores(n_ctx + j * tk, tk) for j in range(n_lat_chunks)]
        m_run, l_run, acc_run = m, l, acc
        for j in range(n_lat_chunks):
            m_new = jnp.maximum(m_run, jnp.max(lat[j], axis=-1, keepdims=True))
            alpha = jnp.exp(m_run - m_new)
            pj = jnp.exp(lat[j] - m_new)
            l_run = alpha * l_run + jnp.sum(pj, axis=-1, keepdims=True)
            vj = v_ref[0, n_ctx + j * tk:n_ctx + (j + 1) * tk, :]
            acc_run = alpha * acc_run + jnp.dot(pj.astype(BF16), vj, preferred_element_type=F32)
            m_run = m_new
        write(acc_run, l_run)


def gqa_attention(qk, v, n_ctx, n_q_heads, n_kv_heads):
    b, t, _ = qk.shape
    hd = GQ_HEAD_DIM
    qw = n_q_heads * hd
    grp = n_q_heads // n_kv_heads
    tq = GQ_TQ
    tk = _tile(t - n_ctx, GQ_TK, V7X_LANES)
    assert n_ctx % tq == 0 and t % tq == 0 and n_ctx % V7X_LANES == 0
    kern = functools.partial(_flash_kernel, grp=grp, hd=hd, tq=tq, tk=tk, n_ctx=n_ctx,
                             n_ctx_qblocks=n_ctx // tq, n_lat_chunks=(t - n_ctx) // tk)
    return pl.pallas_call(
        kern,
        grid=(b, n_kv_heads, t // tq),
        in_specs=[pl.BlockSpec((1, tq, grp * hd), lambda bi, h, i: (bi, i, h)),
                  pl.BlockSpec((1, t, hd), lambda bi, h, i: (bi, 0, n_q_heads + h)),
                  pl.BlockSpec((1, t, hd), lambda bi, h, i: (bi, 0, h))],
        out_specs=pl.BlockSpec((1, tq, grp * hd), lambda bi, h, i: (bi, i, h)),
        out_shape=jax.ShapeDtypeStruct((b, t, qw), BF16),
        compiler_params=_params("parallel", "parallel", "parallel"),
        name="gqa_attention",
    )(qk, qk, v)


def _na_kernel(q_ref, k_ref, v_ref, bias_ref, o_ref, *, n_ctx, rows, n_blocks):
    j = pl.program_id(2)
    bs = range(q_ref.shape[0])
    dn = (((1,), (1,)), ((), ()))
    scale = q_ref.shape[-1] ** -0.5
    q = [q_ref[b] for b in bs]
    kc = [k_ref[b, 0:n_ctx, :] for b in bs]
    vc = [v_ref[b, 0:n_ctx, :] for b in bs]
    s_ctx = [lax.dot_general(q[b], kc[b], dn, preferred_element_type=F32) * scale for b in bs]

    def row_max(x):
        return jnp.max(x, axis=-1, keepdims=True)

    def row_sum(x):
        return jnp.sum(x, axis=-1, keepdims=True)

    def pv(p, v):
        return jnp.dot(p.astype(BF16), v, preferred_element_type=F32)

    @pl.when(j == 0)
    def _():
        p = [jnp.exp(s_ctx[b] - row_max(s_ctx[b])) for b in bs]
        for b in bs:
            o_ref[b] = (pv(p[b], vc[b]) / row_sum(p[b])).astype(o_ref.dtype)

    @pl.when(j > 0)
    def _():
        jj = j - 1
        ks = jnp.clip(NA_QROWS * jj - WIN_ROWS // 2, 0, rows - NA_KROWS)
        off = pl.multiple_of(n_ctx + ks * GRID_W, GRID_W)
        pat = jnp.where(jj == 0, 0, jnp.where(jj == n_blocks - 1, 2, 1))
        bias = bias_ref[pat]
        kw = [k_ref[b, pl.ds(off, NA_KROWS * GRID_W), :] for b in bs]
        vw = [v_ref[b, pl.ds(off, NA_KROWS * GRID_W), :] for b in bs]
        s_win = [lax.dot_general(q[b], kw[b], dn, preferred_element_type=F32) * scale + bias for b in bs]
        m = [jnp.maximum(row_max(s_win[b]), row_max(s_ctx[b])) for b in bs]
        p_win = [jnp.exp(s_win[b] - m[b]) for b in bs]
        p_ctx = [jnp.exp(s_ctx[b] - m[b]) for b in bs]
        for b in bs:
            l = row_sum(p_win[b]) + row_sum(p_ctx[b])
            o_ref[b] = ((pv(p_win[b], vw[b]) + pv(p_ctx[b], vc[b])) / l).astype(o_ref.dtype)


def _na_bias(rpb, rows):
    n_blocks = rows // NA_QROWS
    col = np.arange(GRID_W)
    col_start = np.clip(col - WIN_COLS // 2, 0, GRID_W - WIN_COLS)
    col_ok = (col[None, :] >= col_start[:, None]) & (col[None, :] < col_start[:, None] + WIN_COLS)
    dc = np.clip(col[None, :] - col[:, None] + (WIN_COLS - 1), 0, 2 * WIN_COLS - 2)
    pick_c = (dc[:, :, None] == np.arange(2 * WIN_COLS - 1)).astype(np.float32)
    pats = []
    for jj in (0, 1, n_blocks - 1):
        ks = min(max(NA_QROWS * jj - WIN_ROWS // 2, 0), rows - NA_KROWS)
        r = NA_QROWS * jj + np.arange(NA_QROWS)
        kr = ks + np.arange(NA_KROWS)
        r_start = np.clip(r - WIN_ROWS // 2, 0, rows - WIN_ROWS)
        row_ok = (kr[None, :] >= r_start[:, None]) & (kr[None, :] < r_start[:, None] + WIN_ROWS)
        dr = np.clip(kr[None, :] - r[:, None] + (WIN_ROWS - 1), 0, 2 * WIN_ROWS - 2)
        pick_r = (dr[:, :, None] == np.arange(2 * WIN_ROWS - 1)).astype(np.float32)
        bias = jnp.einsum('hab,rka,qcb->hrqkc', rpb.astype(F32), pick_r, pick_c, precision=HI)
        ok = row_ok[:, None, :, None] & col_ok[None, :, None, :]
        bias = jnp.where(ok[None], bias, NEG_BIG)
        pats.append(bias.reshape(rpb.shape[0], NA_QROWS * GRID_W, NA_KROWS * GRID_W))
    return jnp.stack(pats, axis=0)


def neighborhood_attention(qkv, rpb, n_ctx):
    b, t, width3 = qkv.shape
    width = width3 // 3
    hd = width // NA_HEADS
    seq = t - n_ctx
    rows = seq // GRID_W
    qb = NA_QROWS * GRID_W
    assert n_ctx == qb and rows % NA_QROWS == 0 and rows >= NA_KROWS and WIN_ROWS <= rows
    n_blocks = rows // NA_QROWS
    bias = _na_bias(rpb, rows)
    kern = functools.partial(_na_kernel, n_ctx=n_ctx, rows=rows, n_blocks=n_blocks)
    kb = NA_KROWS * GRID_W
    nbatch = NA_BATCH if b % NA_BATCH == 0 else 1
    return pl.pallas_call(
        kern,
        grid=(NA_HEADS, b // nbatch, n_blocks + 1),
        in_specs=[pl.BlockSpec((nbatch, qb, hd), lambda h, bi, j: (bi, j, h)),
                  pl.BlockSpec((nbatch, t, hd), lambda h, bi, j: (bi, 0, NA_HEADS + h)),
                  pl.BlockSpec((nbatch, t, hd), lambda h, bi, j: (bi, 0, 2 * NA_HEADS + h)),
                  pl.BlockSpec((3, None, qb, kb), lambda h, bi, j: (0, h, 0, 0))],
        out_specs=pl.BlockSpec((nbatch, qb, hd), lambda h, bi, j: (bi, j, h)),
        out_shape=jax.ShapeDtypeStruct((b, t, width), BF16),
        compiler_params=_params("parallel", "parallel", "parallel"),
        name="neighborhood_attention",
    )(qkv, qkv, qkv, bias)


def _s5_kernel(u_ref, t_ref, n_ref, m_ref, d_ref, y_ref, hl_sc, hp_sc, *, nc_ctx, nc_all, bsz):
    u = u_ref[0]
    for part in range(4):
        hl_sc[part] = jnp.dot(u, n_ref[0, part], preferred_element_type=F32)
    zero = jnp.zeros((bsz, S5_STATE), F32)

    def scan(direction):
        dr = d_ref[0, 2 * direction:2 * direction + 1, :]
        di = d_ref[0, 2 * direction + 1:2 * direction + 2, :]

        def body(step, carry):
            hr, hi = carry
            if direction == 0:
                c = step
            else:
                c = jnp.where(step < nc_ctx, nc_ctx - 1 - step, nc_all - 1 - (step - nc_ctx))
            r0 = pl.multiple_of(c * bsz, bsz)
            hp_sc[2 * direction, pl.ds(r0, bsz), :] = hr
            hp_sc[2 * direction + 1, pl.ds(r0, bsz), :] = hi
            lr = hl_sc[2 * direction, pl.ds(r0, bsz), :]
            li = hl_sc[2 * direction + 1, pl.ds(r0, bsz), :]
            return dr * hr - di * hi + lr, dr * hi + di * hr + li

        lax.fori_loop(0, nc_all, body, (zero, zero))

    scan(0)
    scan(1)
    y = jnp.dot(u, t_ref[0], preferred_element_type=F32)
    for part in range(4):
        y = y + jnp.dot(hp_sc[part].astype(BF16), m_ref[0, part], preferred_element_type=F32)
    y_ref[0] = y.astype(y_ref.dtype)


def _s5_tables(lam_re, lam_im, log_dt, b_re, b_im, c_re, c_im):
    c = S5_CHUNK
    g, p = lam_re.shape[1], lam_re.shape[2]
    h = b_re.shape[-1]
    toe = 0.0
    n_parts, m_parts, d_parts = [], [], []
    tt = jnp.arange(c)
    for d in range(2):
        lr, li = lam_re[d].astype(F32), lam_im[d].astype(F32)
        dt = jnp.exp(log_dt[d].astype(F32))[:, None]

        def lam_pow(e):
            e = jnp.asarray(e, F32)[..., None, None]
            mag = jnp.exp(e * lr * dt)
            return mag * jnp.cos(e * li * dt), mag * jnp.sin(e * li * dt)

        lbr, lbi = lam_pow(1)
        nr, ni = lbr - 1.0, lbi
        den = lr * lr + li * li
        fr, fi = (nr * lr + ni * li) / den, (ni * lr - nr * li) / den
        br, bi = b_re[d].astype(F32), b_im[d].astype(F32)
        bbr = fr[..., None] * br - fi[..., None] * bi
        bbi = fr[..., None] * bi + fi[..., None] * br
        cr, ci = c_re[d].astype(F32), c_im[d].astype(F32)
        pr, pi = lam_pow(tt)
        lbbr = pr[..., None] * bbr - pi[..., None] * bbi
        lbbi = pr[..., None] * bbi + pi[..., None] * bbr
        kern = (jnp.einsum('gop,tgpi->tgoi', cr, lbbr, precision=HI)
                - jnp.einsum('gop,tgpi->tgoi', ci, lbbi, precision=HI))
        steps = np.arange(c)
        lag = (steps[None, :] - steps[:, None]) if d == 0 else (steps[:, None] - steps[None, :])
        pick = (lag[:, :, None] == steps).astype(np.float32)
        kt = jnp.einsum('stx,xgoi->gsito', pick, kern, precision=HI)
        toe = toe + kt.reshape(g, c * h, c * h)
        e_in = (c - 1 - tt) if d == 0 else tt
        qr, qi = lam_pow(e_in)
        n_r = qr[..., None] * bbr - qi[..., None] * bbi
        n_i = qr[..., None] * bbi + qi[..., None] * bbr
        n_parts += [jnp.transpose(n_r, (1, 0, 3, 2)).reshape(g, c * h, p),
                    jnp.transpose(n_i, (1, 0, 3, 2)).reshape(g, c * h, p)]
        e_out = (tt + 1) if d == 0 else (c - tt)
        sr, si = lam_pow(e_out)
        clr = cr[None] * sr[:, :, None, :] - ci[None] * si[:, :, None, :]
        cli = cr[None] * si[:, :, None, :] + ci[None] * sr[:, :, None, :]
        m_parts += [jnp.transpose(clr, (1, 3, 0, 2)).reshape(g, p, c * h),
                    -jnp.transpose(cli, (1, 3, 0, 2)).reshape(g, p, c * h)]
        dcr, dci = lam_pow(c)
        d_parts += [dcr, dci]
    return (toe.astype(BF16), jnp.stack(n_parts, 1).astype(BF16), jnp.stack(m_parts, 1).astype(BF16),
            jnp.stack(d_parts, 1))


def s5_scan(u, n_ctx, lam_re, lam_im, log_dt, b_re, b_im, c_re, c_im):
    bsz, t, width = u.shape
    g = width // S5_GROUP
    c = S5_CHUNK
    assert bsz == V7X_SUBLANES and t % c == 0 and n_ctx % c == 0
    nc = t // c
    rows = nc * bsz
    toe, n_tab, m_tab, d_tab = _s5_tables(lam_re, lam_im, log_dt, b_re, b_im, c_re, c_im)
    uc = u.astype(BF16).reshape(bsz, nc, c, g, S5_GROUP)
    uc = jnp.transpose(uc, (3, 1, 0, 2, 4)).reshape(g, rows, c * S5_GROUP)
    kern = functools.partial(_s5_kernel, nc_ctx=n_ctx // c, nc_all=nc, bsz=bsz)
    cw = c * S5_GROUP
    y = pl.pallas_call(
        kern,
        grid=(g,),
        in_specs=[pl.BlockSpec((1, rows, cw), lambda i: (i, 0, 0)),
                  pl.BlockSpec((1, cw, cw), lambda i: (i, 0, 0)),
                  pl.BlockSpec((1, 4, cw, S5_STATE), lambda i: (i, 0, 0, 0)),
                  pl.BlockSpec((1, 4, S5_STATE, cw), lambda i: (i, 0, 0, 0)),
                  pl.BlockSpec((1, 4, S5_STATE), lambda i: (i, 0, 0))],
        out_specs=pl.BlockSpec((1, rows, cw), lambda i: (i, 0, 0)),
        out_shape=jax.ShapeDtypeStruct((g, rows, cw), BF16),
        scratch_shapes=[pltpu.VMEM((4, rows, S5_STATE), F32), pltpu.VMEM((4, rows, S5_STATE), F32)],
        compiler_params=_params("parallel"),
        name="s5_scan",
    )(uc, toe, n_tab, m_tab, d_tab)
    y = y.reshape(g, nc, bsz, c, S5_GROUP)
    return jnp.transpose(y, (2, 1, 3, 0, 4)).reshape(bsz, t, width)


def _rwkv_kernel(at_ref, rt_ref, bt_ref, kt_ref, v_ref, pc_ref, y_ref, s_sc, *, heads, c):
    n = pl.program_id(2)

    @pl.when(n == 0)
    def _():
        s_sc[...] = jnp.zeros(s_sc.shape, F32)

    row = lax.broadcasted_iota(jnp.int32, (c, c), 0)
    col = lax.broadcasted_iota(jnp.int32, (c, c), 1)
    ahead = (row - col) * jnp.where(pl.program_id(0) == 0, 1, -1)
    strict = ahead > 0
    incl = ahead >= 0
    nt = (((1,), (1,)), ((), ()))
    tn = (((0,), (0,)), ((), ()))
    n_factors = int(math.log2(c))
    hs = range(heads)
    sls = [slice(j * RK_HEAD, (j + 1) * RK_HEAD) for j in hs]

    def dot(a, b, dims=None):
        a, b = a.astype(BF16), b.astype(BF16)
        if dims is None:
            return jnp.dot(a, b, preferred_element_type=F32)
        return lax.dot_general(a, b, dims, preferred_element_type=F32)

    vv = [v_ref[0, :, sl] for sl in sls]
    s0 = [s_sc[j] for j in hs]
    ar = [jnp.concatenate([at_ref[0, 0, :, sl], rt_ref[0, 0, :, sl]], axis=0) for sl in sls]
    bk = [jnp.concatenate([bt_ref[0, 0, :, sl], kt_ref[0, 0, :, sl]], axis=0) for sl in sls]
    g = [dot(ar[j], bk[j], nt) for j in hs]
    ah = [dot(ar[j], s0[j], nt) for j in hs]
    x = [jnp.where(strict, g[j][:c, :c], 0.0) for j in hs]
    a_ak = [jnp.where(strict, g[j][:c, c:], 0.0) for j in hs]
    m = [jnp.concatenate([jnp.where(incl, g[j][c:, :c], 0.0), jnp.where(incl, g[j][c:, c:], 0.0)], axis=1)
         for j in hs]
    u = [ah[j][:c] + dot(a_ak[j], vv[j]) for j in hs]
    u = [u[j] + dot(x[j], u[j]) for j in hs]
    for _ in range(n_factors - 1):
        x = [dot(x[j], x[j]) for j in hs]
        u = [u[j] + dot(x[j], u[j]) for j in hs]
    uv = [jnp.concatenate([u[j].astype(BF16), vv[j]], axis=0) for j in hs]
    for j in hs:
        y_ref[0, 0, :, sls[j]] = ah[j][c:] + dot(m[j], uv[j])
    for j in hs:
        s_new = s0[j] + dot(uv[j], bk[j], tn)
        s_sc[j] = s_new * pc_ref[0, 0, 0, :, sls[j]]


def rwkv7_scan(at, rt, bt, kt, v, pc, n_ctx):
    _, bsz, t, width = at.shape
    heads = width // RK_HEAD
    c = RK_CHUNK
    assert t % c == 0 and n_ctx % c == 0
    ncc, nc = n_ctx // c, t // c

    def chunk(d, n):
        back = jnp.where(n < ncc, ncc - 1 - n, ncc + (nc - 1 - n))
        return jnp.where(d == 0, n, back)

    stream = pl.BlockSpec((1, 1, c, width), lambda d, bi, n: (d, bi, chunk(d, n), 0))
    kern = functools.partial(_rwkv_kernel, heads=heads, c=c)
    return pl.pallas_call(
        kern,
        grid=(2, bsz, nc),
        in_specs=[stream, stream, stream, stream,
                  pl.BlockSpec((1, c, width), lambda d, bi, n: (bi, chunk(d, n), 0)),
                  pl.BlockSpec((1, 1, 1, 1, width), lambda d, bi, n: (d, bi, chunk(d, n), 0, 0))],
        out_specs=stream,
        out_shape=jax.ShapeDtypeStruct((2, bsz, t, width), F32),
        scratch_shapes=[pltpu.VMEM((heads, RK_HEAD, RK_HEAD), F32)],
        compiler_params=_params("parallel", "parallel", "arbitrary"),
        name="rwkv7_scan",
    )(at, rt, bt, kt, v, pc)


def _deinterleave_kernel(w_ref, p_ref, o_ref):
    g = p_ref.shape[0]
    half = o_ref.shape[1] // 2
    for c in range(w_ref.shape[1] // g):
        t = jnp.dot(w_ref[:, c * g:(c + 1) * g].astype(BF16), p_ref[...], preferred_element_type=F32)
        o_ref[:, c * g // 2:(c + 1) * g // 2] = t[:, :g // 2].astype(o_ref.dtype)
        o_ref[:, half + c * g // 2:half + (c + 1) * g // 2] = t[:, g // 2:].astype(o_ref.dtype)


def deinterleave_columns(w_all, layer):
    _, r, n2 = w_all.shape
    g = V7X_MXU_DIM
    assert n2 % g == 0
    perm = np.concatenate([np.arange(0, g, 2), np.arange(1, g, 2)])
    p = jnp.asarray(np.arange(g)[:, None] == perm[None, :], BF16)
    tm = _tile(r, 1024, 16)
    w = w_all.reshape(-1, n2)
    first = layer * (r // tm)
    return pl.pallas_call(
        _deinterleave_kernel,
        grid=(r // tm,),
        in_specs=[pl.BlockSpec((tm, n2), lambda i: (first + i, 0)), pl.BlockSpec((g, g), lambda i: (0, 0))],
        out_specs=pl.BlockSpec((tm, n2), lambda i: (i, 0)),
        out_shape=jax.ShapeDtypeStruct((r, n2), BF16),
        compiler_params=_params("parallel"),
        name="deinterleave_columns",
    )(w, p)


def _moe_kernel(be_ref, nb_ref, x_ref, g_ref, w1_ref, b1g_ref, b1l_ref, w2_ref, b2_ref, o_ref, w2_sc):
    i = pl.program_id(0)
    de = w2_ref.shape[1]

    @pl.when((i == 0) | (be_ref[i] != be_ref[jnp.maximum(i - 1, 0)]))
    def _():
        w2_sc[...] = w2_ref[0].astype(w2_sc.dtype)

    @pl.when(i < nb_ref[0])
    def _():
        x = x_ref[...]
        z = jnp.dot(x, w1_ref[0], preferred_element_type=F32)
        zg = z[:, :de] + b1g_ref[0]
        zl = z[:, de:] + b1l_ref[0]
        zg = jnp.minimum(zg, SWIGLU_LIMIT)
        zl = jnp.clip(zl, -SWIGLU_LIMIT, SWIGLU_LIMIT)
        act = zg * jax.nn.sigmoid(SWIGLU_ALPHA * zg) * (zl + 1.0)
        y = jnp.dot(act.astype(BF16), w2_sc[...], preferred_element_type=F32) + b2_ref[0]
        o_ref[...] = (y * g_ref[...]).astype(o_ref.dtype)

    @pl.when(i >= nb_ref[0])
    def _():
        o_ref[...] = jnp.zeros(o_ref.shape, o_ref.dtype)


def moe_ffn(h, w_router, b_router, w1p, w2_all, layer, b1, b2):
    n, dm = h.shape
    n_exp = w_router.shape[1]
    de = w2_all.shape[2]
    tm = MOE_TM
    logits = jnp.dot(h, w_router, precision=HI, preferred_element_type=F32) + b_router.astype(F32)
    top_val, top_idx = lax.top_k(logits, TOP_K)
    gates = jax.nn.softmax(top_val, axis=-1)
    flat_e = top_idx.reshape(-1).astype(jnp.int32)
    rb = V7X_LANES
    assert (n * TOP_K) % rb == 0
    onehot = (flat_e[:, None] == jnp.arange(n_exp, dtype=jnp.int32)[None, :]).astype(F32)
    blocks = onehot.reshape(-1, rb, n_exp)
    tri = np.tril(np.ones((rb, rb), np.float32))
    within = jnp.einsum('ij,bjk->bik', tri, blocks, precision=HI)
    totals = within[:, -1, :]
    before = jnp.cumsum(totals, axis=0) - totals
    csum = within + before[:, None, :]
    rank = (jnp.sum(csum * blocks, axis=-1).reshape(-1) - 1.0).astype(jnp.int32)
    counts = (before[-1] + totals[-1]).astype(jnp.int32)
    padded = (counts + tm - 1) // tm * tm
    pad_end = jnp.cumsum(padded)
    pad_start = pad_end - padded
    dest = pad_start[flat_e] + rank
    n_blocks = -(-(n * TOP_K) // tm) + n_exp
    cap = n_blocks * tm
    slot_src = jnp.full((cap,), -1, jnp.int32).at[dest].set(jnp.arange(n * TOP_K, dtype=jnp.int32))
    filled = slot_src >= 0
    slot_tok = jnp.where(filled, slot_src // TOP_K, 0)
    slot_gate = jnp.where(filled, gates.reshape(-1)[jnp.maximum(slot_src, 0)], 0.0)
    block_start = jnp.arange(n_blocks, dtype=jnp.int32) * tm
    block_exp = jnp.minimum(jnp.sum(pad_end[None, :] <= block_start[:, None], axis=1), n_exp - 1).astype(jnp.int32)
    n_used = (pad_end[-1] // tm).astype(jnp.int32).reshape(1)
    x_sorted = h.astype(BF16)[slot_tok]

    b1g = b1[:, None, 0::2].astype(F32)
    b1l = b1[:, None, 1::2].astype(F32)
    b2r = b2[:, None, :].astype(F32)

    grid_spec = pltpu.PrefetchScalarGridSpec(
        num_scalar_prefetch=2,
        grid=(n_blocks,),
        in_specs=[pl.BlockSpec((tm, dm), lambda i, be, nb: (i, 0)),
                  pl.BlockSpec((tm, 1), lambda i, be, nb: (i, 0)),
                  pl.BlockSpec((1, dm, 2 * de), lambda i, be, nb: (be[i], 0, 0)),
                  pl.BlockSpec((1, 1, de), lambda i, be, nb: (be[i], 0, 0)),
                  pl.BlockSpec((1, 1, de), lambda i, be, nb: (be[i], 0, 0)),
                  pl.BlockSpec((1, de, dm), lambda i, be, nb: (layer * n_exp + be[i], 0, 0)),
                  pl.BlockSpec((1, 1, dm), lambda i, be, nb: (be[i], 0, 0))],
        out_specs=pl.BlockSpec((tm, dm), lambda i, be, nb: (i, 0)),
        scratch_shapes=[pltpu.VMEM((de, dm), BF16)],
    )
    yb = pl.pallas_call(
        _moe_kernel,
        grid_spec=grid_spec,
        out_shape=jax.ShapeDtypeStruct((cap, dm), BF16),
        compiler_params=_params("arbitrary"),
        name="moe_experts",
    )(block_exp, n_used, x_sorted, slot_gate[:, None], w1p, b1g, b1l, w2_all.reshape(-1, de, dm), b2r)
    slots = dest.reshape(n, TOP_K)
    return [yb[slots[:, kk]] for kk in range(TOP_K)]


def _expert_sum(refs, n_parts, half_blocks):
    def total(parts):
        y = parts[0][...].astype(F32)
        for p in parts[1:]:
            y = y + p[...].astype(F32)
        return y

    first = pl.program_id(0) < half_blocks
    return jnp.where(first, total(refs[:n_parts]), total(refs[n_parts:2 * n_parts]))


def _combine_kernel(*refs, n_parts, half_blocks):
    h_ref, g_ref, nw_ref, sc_ref, sh_ref, hn_ref, a_ref = refs[2 * n_parts:]
    hn = h_ref[...] + g_ref[0] * _expert_sum(refs, n_parts, half_blocks)
    hn_ref[...] = hn
    normed = hn * lax.rsqrt(jnp.mean(hn * hn, axis=-1, keepdims=True) + NORM_EPS) * nw_ref[...]
    a_ref[...] = (normed * (1.0 + sc_ref[0]) + sh_ref[0]).astype(a_ref.dtype)


def _combine_final_kernel(*refs, n_parts, half_blocks):
    h_ref, g_ref, nw_ref, o_ref = refs[2 * n_parts:]
    hn = h_ref[...] + g_ref[0] * _expert_sum(refs, n_parts, half_blocks)
    o_ref[...] = (hn * lax.rsqrt(jnp.mean(hn * hn, axis=-1, keepdims=True) + NORM_EPS)
                  * nw_ref[...]).astype(o_ref.dtype)


def _half_specs(n_parts, tm, dm, half_blocks):
    lo = pl.BlockSpec((tm, dm), lambda i: (jnp.minimum(i, half_blocks - 1), 0))
    hi = pl.BlockSpec((tm, dm), lambda i: (jnp.maximum(i - half_blocks, 0), 0))
    return [lo] * n_parts + [hi] * n_parts


def combine_residual_norm(parts_lo, parts_hi, h, n_ctx, gate, norm_w, scale, shift):
    bsz, t, dm = h.shape
    tm = _tile(n_ctx, 256, 16)
    m = bsz * t
    assert t % tm == 0 and (m // 2) % tm == 0
    nbt, ncb = t // tm, n_ctx // tm
    half_blocks = m // 2 // tm
    n_parts = len(parts_lo)
    rows = pl.BlockSpec((tm, dm), lambda i: (i, 0))
    mod = pl.BlockSpec((1, 1, dm), lambda i: (jnp.where(i % nbt < ncb, bsz, i // nbt), 0, 0))
    hn, a = pl.pallas_call(
        functools.partial(_combine_kernel, n_parts=n_parts, half_blocks=half_blocks),
        grid=(m // tm,),
        in_specs=_half_specs(n_parts, tm, dm, half_blocks)
        + [rows, mod, pl.BlockSpec((1, dm), lambda i: (0, 0)), mod, mod],
        out_specs=[rows, rows],
        out_shape=[jax.ShapeDtypeStruct((m, dm), F32), jax.ShapeDtypeStruct((m, dm), BF16)],
        compiler_params=_params("parallel"),
        name="combine_residual_norm",
    )(*parts_lo, *parts_hi, h.reshape(m, dm), gate[:, None, :], norm_w.astype(F32)[None, :],
      scale[:, None, :], shift[:, None, :])
    return hn.reshape(bsz, t, dm), a.reshape(bsz, t, dm)


def combine_final_norm(parts_lo, parts_hi, h, n_ctx, gate, norm_w, out_dtype):
    bsz, t, dm = h.shape
    seq = t - n_ctx
    tm = _tile(n_ctx, 256, 16)
    m = bsz * seq
    assert t % tm == 0 and seq % tm == 0 and (m // 2) % tm == 0
    nbt, ncb, nbs = t // tm, n_ctx // tm, seq // tm
    half_blocks = m // 2 // tm
    n_parts = len(parts_lo)
    out = pl.pallas_call(
        functools.partial(_combine_final_kernel, n_parts=n_parts, half_blocks=half_blocks),
        grid=(m // tm,),
        in_specs=_half_specs(n_parts, tm, dm, half_blocks)
        + [pl.BlockSpec((tm, dm), lambda i: ((i // nbs) * nbt + ncb + i % nbs, 0)),
           pl.BlockSpec((1, 1, dm), lambda i: (i // nbs, 0, 0)),
           pl.BlockSpec((1, dm), lambda i: (0, 0))],
        out_specs=pl.BlockSpec((tm, dm), lambda i: (i, 0)),
        out_shape=jax.ShapeDtypeStruct((m, dm), out_dtype),
        compiler_params=_params("parallel"),
        name="combine_final_norm",
    )(*parts_lo, *parts_hi, h.reshape(bsz * t, dm), gate[:, None, :], norm_w.astype(F32)[None, :])
    return out.reshape(bsz, seq, dm)


def _rms(x, g, eps=NORM_EPS):
    xf = x.astype(F32)
    return xf * lax.rsqrt(jnp.mean(xf * xf, axis=-1, keepdims=True) + eps) * g.astype(F32)


def _modulated(h, n_ctx, norm_w, scale_c, shift_c, scale_x, shift_x):
    hn = _rms(h, norm_w)
    ac = hn[:, :n_ctx] * (1.0 + scale_c) + shift_c
    ax = hn[:, n_ctx:] * (1.0 + scale_x[:, None, :]) + shift_x[:, None, :]
    return jnp.concatenate([ac, ax], axis=1)


def _s5_glu_kernel(u_ref, ys_ref, d_ref, w_ref, b_ref, o_ref):
    y = d_ref[...] * u_ref[...] + ys_ref[...].astype(F32)
    g = y * jax.nn.sigmoid(2.0 * math.sqrt(2.0 / math.pi) * (y + 0.044715 * (y * y * y)))
    z = jnp.dot(g.astype(BF16), w_ref[...], preferred_element_type=F32) + b_ref[...]
    o_ref[...] = (g * jax.nn.sigmoid(z)).astype(o_ref.dtype)


def s5_glu(u, ys, d_skip, glu_w, glu_b):
    bsz, t, width = u.shape
    m = bsz * t
    tm = _tile(m, 512, 16)
    rows = pl.BlockSpec((tm, width), lambda i: (i, 0))
    vec = pl.BlockSpec((1, width), lambda i: (0, 0))
    out = pl.pallas_call(
        _s5_glu_kernel,
        grid=(m // tm,),
        in_specs=[rows, rows, vec, pl.BlockSpec((width, width), lambda i: (0, 0)), vec],
        out_specs=rows,
        out_shape=jax.ShapeDtypeStruct((m, width), BF16),
        compiler_params=_params("parallel"),
        name="s5_glu",
    )(u.reshape(m, width), ys.reshape(m, width), d_skip.astype(F32)[None, :], glu_w.astype(BF16),
      glu_b.astype(F32)[None, :])
    return out.reshape(bsz, t, width)


def _even_mixer(a, n_ctx, w_in, lam_re, lam_im, log_dt, b_re, b_im, c_re, c_im, d_skip,
                glu_w, glu_b, rpb):
    bsz, t, dm = a.shape
    mix_w = dm // 2
    a2 = a.reshape(bsz * t, dm).astype(BF16)
    u = matmul(a2, w_in[:, :mix_w]).reshape(bsz, t, mix_w)
    qkv = matmul(a2, w_in[:, mix_w:], out_dtype=BF16).reshape(bsz, t, 3 * mix_w)
    ys = s5_scan(u, n_ctx, lam_re, lam_im, log_dt, b_re, b_im, c_re, c_im)
    ya = s5_glu(u, ys, d_skip, glu_w, glu_b)
    yb = neighborhood_attention(qkv, rpb, n_ctx)
    return ya, yb


def _rope_tables(n_ctx, seq, hd):
    half = hd // 2
    inv = ROPE_THETA ** (-jnp.arange(0, half, 2, dtype=F32) / half)
    pos = jnp.arange(seq)
    ang_r = (pos // GRID_W).astype(F32)[:, None] * inv[None, :]
    ang_c = (pos % GRID_W).astype(F32)[:, None] * inv[None, :]
    cos = jnp.concatenate([jnp.cos(ang_r), jnp.cos(ang_r), jnp.cos(ang_c), jnp.cos(ang_c)], axis=-1)
    sin = jnp.concatenate([-jnp.sin(ang_r), jnp.sin(ang_r), -jnp.sin(ang_c), jnp.sin(ang_c)], axis=-1)
    cos = jnp.concatenate([jnp.ones((n_ctx, hd), F32), cos], axis=0)
    sin = jnp.concatenate([jnp.zeros((n_ctx, hd), F32), sin], axis=0)
    return cos, sin


def _qk_prep_kernel(z_ref, cos_ref, sin_ref, gq_ref, gk_ref, o_ref, *, n_q_heads, hd):
    cosv = cos_ref[...]
    sinv = sin_ref[...]
    lane = lax.broadcasted_iota(jnp.int32, (1, hd), 1)
    first_quarter = (lane % (hd // 2)) < hd // 4
    for h in range(z_ref.shape[2] // hd):
        x = z_ref[0, :, h * hd:(h + 1) * hd]
        g = gq_ref[...] if h < n_q_heads else gk_ref[...]
        xn = x * lax.rsqrt(jnp.mean(x * x, axis=-1, keepdims=True) + NORM_EPS) * g
        partner = jnp.where(first_quarter, pltpu.roll(xn, hd - hd // 4, axis=1), pltpu.roll(xn, hd // 4, axis=1))
        o_ref[0, :, h * hd:(h + 1) * hd] = (xn * cosv + partner * sinv).astype(o_ref.dtype)


def qk_prepare(zqk, n_ctx, n_q_heads, q_norm, k_norm):
    bsz, t, width = zqk.shape
    hd = GQ_HEAD_DIM
    tm = _tile(t, 256, 16)
    cos, sin = _rope_tables(n_ctx, t - n_ctx, hd)
    rows = pl.BlockSpec((1, tm, width), lambda b, i: (b, i, 0))
    tab = pl.BlockSpec((tm, hd), lambda b, i: (i, 0))
    gain = pl.BlockSpec((1, hd), lambda b, i: (0, 0))
    return pl.pallas_call(
        functools.partial(_qk_prep_kernel, n_q_heads=n_q_heads, hd=hd),
        grid=(bsz, t // tm),
        in_specs=[rows, tab, tab, gain, gain],
        out_specs=rows,
        out_shape=jax.ShapeDtypeStruct((bsz, t, width), BF16),
        compiler_params=_params("parallel", "parallel"),
        name="qk_prepare",
    )(zqk, cos, sin, q_norm.astype(F32)[None, :], k_norm.astype(F32)[None, :])


def _split_bf16(x, terms):
    parts = []
    for _ in range(terms):
        p = x.astype(BF16)
        parts.append(p)
        x = x - p.astype(F32)
    return parts


def _head_sums(x, ones_ref):
    gw = ones_ref.shape[0]
    hi, lo = _split_bf16(x, 2)
    outs = []
    for g in range(x.shape[1] // gw):
        sl = slice(g * gw, (g + 1) * gw)
        outs.append(jnp.dot(hi[:, sl], ones_ref[...], preferred_element_type=F32)
                    + jnp.dot(lo[:, sl], ones_ref[...], preferred_element_type=F32))
    return jnp.concatenate(outs, axis=1)


def _rwkv_prep_kernel(z_ref, zp_ref, zn_ref, mu_ref, gup_ref, wup_ref, aup_ref, vec_ref, rk_ref, tri_ref,
                      ones_ref, at_ref, rt_ref, bt_ref, kt_ref, v_ref, pc_ref, gate_ref, bonus_ref,
                      *, width, tb, c, ncb, nb):
    i = pl.program_id(1)
    keep_prev = jnp.where((i == 0) | (i == ncb), 0.0, 1.0)
    keep_next = jnp.where((i == ncb - 1) | (i == nb - 1), 0.0, 1.0)
    z = z_ref[0]
    rid = lax.broadcasted_iota(jnp.int32, (tb, 1), 0)
    zp = jnp.where(rid == 0, zp_ref[0, 7:8, :] * keep_prev, pltpu.roll(z, 1, axis=0))
    zn = jnp.where(rid == tb - 1, zn_ref[0, 0:1, :] * keep_next, pltpu.roll(z, tb - 1, axis=0))
    zs = z + mu_ref[0:1, :] * (zp - z) + mu_ref[1:2, :] * (zn - z)
    w = width
    r, k, v = zs[:, :w], zs[:, w:2 * w], zs[:, 2 * w:3 * w]
    g_lo = zs[:, 3 * w:3 * w + RK_GATE_RANK]
    o = 3 * w + RK_GATE_RANK
    wl = jnp.tanh(zs[:, o:o + 2 * RK_DECAY_RANK]).astype(BF16)
    al = zs[:, o + 2 * RK_DECAY_RANK:o + 2 * RK_DECAY_RANK + 2 * RK_ICLR_RANK].astype(BF16)
    v_ref[0] = v.astype(v_ref.dtype)
    gate_ref[0] = jnp.dot(jax.nn.sigmoid(g_lo).astype(BF16), gup_ref[...], preferred_element_type=F32)
    ksum = jnp.zeros_like(k)
    for d in range(2):
        w0, a0 = vec_ref[d, 0:1, :], vec_ref[d, 1:2, :]
        k_k, k_a = vec_ref[d, 2:3, :], vec_ref[d, 3:4, :]
        neg = -(w0 + jnp.dot(wl, wup_ref[d], preferred_element_type=F32))
        softplus = jnp.maximum(neg, 0.0) + jnp.log(1.0 + jnp.exp(-jnp.abs(neg)))
        lw = -jnp.exp(-softplus - 0.5)
        iclr = jax.nn.sigmoid(a0 + jnp.dot(al, aup_ref[d], preferred_element_type=F32))
        kk = k * k_k
        kk = kk / jnp.maximum(jnp.sqrt(_head_sums(kk * kk, ones_ref)), 1e-12)
        k_d = k * (1.0 + (iclr - 1.0) * k_a)
        ksum = ksum + k_d
        cum = sum(jnp.dot(tri_ref[d], part, preferred_element_type=F32) for part in _split_bf16(lw, 3))
        p_inv = jnp.exp(-cum)
        at_ref[d, 0] = (-kk * jnp.exp(cum - lw)).astype(at_ref.dtype)
        rt_ref[d, 0] = (r * jnp.exp(cum)).astype(rt_ref.dtype)
        bt_ref[d, 0] = (kk * iclr * p_inv).astype(bt_ref.dtype)
        kt_ref[d, 0] = (k_d * p_inv).astype(kt_ref.dtype)
        for j in range(tb // c):
            row = j * c + (c - 1 if d == 0 else 0)
            pc_ref[d, 0, j] = jnp.exp(cum[row:row + 1, :])
    bonus_ref[0] = _head_sums(r * ksum * rk_ref[...], ones_ref) * v


def rwkv7_prepare(zr, n_ctx, mu, g_up, w0, w_up, a0, a_up, k_k, k_a, r_k):
    bsz, t, rk_in = zr.shape
    width = g_up.shape[1]
    c, tb = RK_CHUNK, RK_TBLOCK
    assert n_ctx % tb == 0 and t % tb == 0 and tb % c == 0 and tb % V7X_SUBLANES == 0
    ncb, nb = n_ctx // tb, t // tb
    rows_per_tile = V7X_SUBLANES

    def pad_rows(up, d, rank):
        return jnp.pad(up, ((d * rank, (1 - d) * rank), (0, 0)))

    wup = jnp.stack([pad_rows(w_up[d], d, RK_DECAY_RANK) for d in range(2)]).astype(BF16)
    aup = jnp.stack([pad_rows(a_up[d], d, RK_ICLR_RANK) for d in range(2)]).astype(BF16)
    vec = jnp.stack([w0, a0, k_k, k_a], axis=1).astype(F32)
    step = np.arange(tb)
    same = (step[:, None] // c) == (step[None, :] // c)
    tri = np.stack([same & (step[None, :] <= step[:, None]), same & (step[None, :] >= step[:, None])])
    seg = np.arange(V7X_MXU_DIM) // RK_HEAD
    ones = (seg[:, None] == seg[None, :])

    def full(shape):
        return pl.BlockSpec(shape, lambda b, i: (0,) * len(shape))

    per_dir = pl.BlockSpec((2, 1, tb, width), lambda b, i: (0, b, i, 0))
    rows = pl.BlockSpec((1, tb, width), lambda b, i: (b, i, 0))
    kern = functools.partial(_rwkv_prep_kernel, width=width, tb=tb, c=c, ncb=ncb, nb=nb)
    stream = jax.ShapeDtypeStruct((2, bsz, t, width), BF16)
    return pl.pallas_call(
        kern,
        grid=(bsz, nb),
        in_specs=[pl.BlockSpec((1, tb, rk_in), lambda b, i: (b, i, 0)),
                  pl.BlockSpec((1, rows_per_tile, rk_in),
                               lambda b, i: (b, jnp.maximum(i * (tb // rows_per_tile) - 1, 0), 0)),
                  pl.BlockSpec((1, rows_per_tile, rk_in),
                               lambda b, i: (b, jnp.minimum((i + 1) * (tb // rows_per_tile),
                                                            t // rows_per_tile - 1), 0)),
                  full((2, rk_in)), full((RK_GATE_RANK, width)), full((2, 2 * RK_DECAY_RANK, width)),
                  full((2, 2 * RK_ICLR_RANK, width)), full((2, 4, width)), full((1, width)),
                  full((2, tb, tb)), full((V7X_MXU_DIM, V7X_MXU_DIM))],
        out_specs=[per_dir, per_dir, per_dir, per_dir, rows,
                   pl.BlockSpec((2, 1, tb // c, 1, width), lambda b, i: (0, b, i, 0, 0)), rows, rows],
        out_shape=[stream, stream, stream, stream, jax.ShapeDtypeStruct((bsz, t, width), BF16),
                   jax.ShapeDtypeStruct((2, bsz, t // c, 1, width), F32),
                   jax.ShapeDtypeStruct((bsz, t, width), F32), jax.ShapeDtypeStruct((bsz, t, width), F32)],
        compiler_params=_params("parallel", "parallel"),
        name="rwkv7_prepare",
    )(zr, zr, zr, mu.astype(F32), g_up.astype(BF16), wup, aup, vec, r_k.astype(F32).reshape(1, width),
      jnp.asarray(tri, BF16), jnp.asarray(ones, BF16))


def _rwkv_finish_kernel(ys_ref, bonus_ref, gate_ref, lnw_ref, lnb_ref, ones_ref, o_ref):
    y = ys_ref[0, 0] + ys_ref[1, 0]
    inv_n = 1.0 / RK_HEAD
    mean = _head_sums(y, ones_ref) * inv_n
    dev = y - mean
    var = _head_sums(dev * dev, ones_ref) * inv_n
    yn = dev * lax.rsqrt(var + RK_GN_EPS) * lnw_ref[...] + lnb_ref[...]
    o_ref[0] = ((yn + bonus_ref[0]) * gate_ref[0]).astype(o_ref.dtype)


def rwkv7_finish(ys, bonus, gate, ln_w, ln_b):
    _, bsz, t, width = ys.shape
    tb = RK_TBLOCK
    seg = np.arange(V7X_MXU_DIM) // RK_HEAD
    ones = (seg[:, None] == seg[None, :])
    rows = pl.BlockSpec((1, tb, width), lambda b, i: (b, i, 0))
    vec = pl.BlockSpec((1, width), lambda b, i: (0, 0))
    return pl.pallas_call(
        _rwkv_finish_kernel,
        grid=(bsz, t // tb),
        in_specs=[pl.BlockSpec((2, 1, tb, width), lambda b, i: (0, b, i, 0)), rows, rows, vec, vec,
                  pl.BlockSpec((V7X_MXU_DIM, V7X_MXU_DIM), lambda b, i: (0, 0))],
        out_specs=rows,
        out_shape=jax.ShapeDtypeStruct((bsz, t, width), BF16),
        compiler_params=_params("parallel", "parallel"),
        name="rwkv7_finish",
    )(ys, bonus, gate, ln_w.astype(F32)[None, :], ln_b.astype(F32)[None, :], jnp.asarray(ones, BF16))


def _odd_mixer(a, n_ctx, w_in, mu, g_up, w0, w_up, a0, a_up, k_k, k_a, r_k, ln_w, ln_b,
               q_norm, k_norm):
    bsz, t, dm = a.shape
    mix_w = dm // 2
    rk_in = 3 * mix_w + RK_GATE_RANK + 2 * RK_DECAY_RANK + 2 * RK_ICLR_RANK
    gq_heads = mix_w // GQ_HEAD_DIM
    kv_heads = gq_heads // GQ_GROUP
    kv_w = kv_heads * GQ_HEAD_DIM
    a2 = a.reshape(bsz * t, dm).astype(BF16)
    zr = matmul(a2, w_in[:, :rk_in]).reshape(bsz, t, rk_in)
    zqk = matmul(a2, w_in[:, rk_in:rk_in + mix_w + kv_w]).reshape(bsz, t, mix_w + kv_w)
    vv = matmul(a2, w_in[:, rk_in + mix_w + kv_w:], out_dtype=BF16).reshape(bsz, t, kv_w)

    at, rt, bt, kt, v, pc, gate, bonus = rwkv7_prepare(zr, n_ctx, mu, g_up, w0, w_up, a0, a_up, k_k, k_a,
                                                       r_k.reshape(-1))
    ys = rwkv7_scan(at, rt, bt, kt, v, pc, n_ctx)
    yc = rwkv7_finish(ys, bonus, gate, ln_w, ln_b)
    qk = qk_prepare(zqk, n_ctx, gq_heads, q_norm, k_norm)
    yd = gqa_attention(qk, vv, n_ctx, gq_heads, kv_heads)
    return yc, yd


def kernel(x, c, ctx, c_ctx, w_mod, b_mod, norm1, norm2, final_norm, ev_w_in, ev_w_out, s5_lam_re, s5_lam_im, s5_log_dt, s5_b_re, s5_b_im, s5_c_re, s5_c_im, s5_d, s5_glu_w, s5_glu_b, na_rpb, od_w_in, od_w_out, rk_mu, rk_g_up, rk_w0, rk_w_up, rk_a0, rk_a_up, rk_k_k, rk_k_a, rk_r_k, rk_ln_w, rk_ln_b, gq_q_norm, gq_k_norm, moe_w_router, moe_b_router, moe_w1, moe_b1, moe_w2, moe_b2):
    bsz, seq, dm = x.shape
    n_ctx = ctx.shape[1]
    depth = w_mod.shape[0]
    t = n_ctx + seq
    h = jnp.concatenate([ctx, x], axis=1).astype(F32)
    cond = jnp.concatenate([jax.nn.silu(c), jax.nn.silu(c_ctx)[None, :]], axis=0)
    mods = [jnp.split(matmul(cond, w_mod[i]) + b_mod[i].astype(F32), N_MOD, axis=-1) for i in range(depth)]
    sh1, sc1 = mods[0][0], mods[0][1]
    a = _modulated(h, n_ctx, norm1[0], sc1[bsz], sh1[bsz], sc1[:bsz], sh1[:bsz])
    for i in range(depth):
        last = i == depth - 1
        j = i // 2
        _, _, g1, sh2, sc2, g2 = mods[i]
        if i % 2 == 0:
            ya, yb = _even_mixer(a, n_ctx, ev_w_in[j], s5_lam_re[j], s5_lam_im[j], s5_log_dt[j],
                                 s5_b_re[j], s5_b_im[j], s5_c_re[j], s5_c_im[j], s5_d[j], s5_glu_w[j],
                                 s5_glu_b[j], na_rpb[j])
            w_out = ev_w_out[j]
        else:
            ya, yb = _odd_mixer(a, n_ctx, od_w_in[j], rk_mu[j], rk_g_up[j], rk_w0[j], rk_w_up[j],
                                rk_a0[j], rk_a_up[j], rk_k_k[j], rk_k_a[j], rk_r_k[j], rk_ln_w[j], rk_ln_b[j],
                                gq_q_norm[j], gq_k_norm[j])
            w_out = od_w_out[j]
        h, f = out_proj_residual_norm(ya, yb, w_out, h, n_ctx, g1, norm2[i], sc2, sh2)
        n_exp, de = moe_w2.shape[1], moe_w2.shape[2]
        w1p = deinterleave_columns(moe_w1.reshape(depth, n_exp * dm, 2 * de), i).reshape(n_exp, dm, 2 * de)
        moe_args = (moe_w_router[i], moe_b_router[i], w1p, moe_w2, i, moe_b1[i], moe_b2[i])
        tokens = f[:, n_ctx:].reshape(bsz * seq, dm) if last else f.reshape(bsz * t, dm)
        half = tokens.shape[0] // 2
        parts_lo = moe_ffn(tokens[:half], *moe_args)
        parts_hi = moe_ffn(tokens[half:], *moe_args)
        if last:
            return combine_final_norm(parts_lo, parts_hi, h, n_ctx, g2, final_norm, x.dtype)
        h, a = combine_residual_norm(parts_lo, parts_hi, h, n_ctx, g2, norm1[i + 1], mods[i + 1][1],
                                     mods[i + 1][0])
```
